```python
import math
import jax, jax.numpy as jnp
from jax import lax
import numpy as np

D_MODEL = 1024
BATCH = 2
SEQ = 16384
DEPTH = 4

CHUNK = 64
N_EVEN = (DEPTH + 1) // 2
N_ODD = DEPTH // 2
N_VRES = max(N_ODD - 1, 0)
A_HEADS = D_MODEL // 256
A_DIM = 64
A_VDIM = 2 * A_DIM
ROPE_DIM = A_DIM // 4
ROPE_THETA = 500000.0
Q_BLOCK = 128
SUBLN_EPS = 1e-5
B_HEADS = D_MODEL // 128
B_DIM = 64
B_CHUNKS_BACK = 8
REL_CLIP = 128
A_QK_WIDTH = A_HEADS * 2 * A_DIM
A_V_WIDTH = A_HEADS * A_VDIM
B_WIDTH = B_HEADS * B_DIM
IN_WIDTH = 2 * A_QK_WIDTH + A_V_WIDTH + 3 * B_WIDTH
MIX_WIDTH = A_V_WIDTH + B_WIDTH
C_DIM = 64
C_HEADS = D_MODEL // C_DIM
LORA_W = max(32, round(1.8 * D_MODEL ** 0.5 / 32) * 32)
LORA_A = max(32, round(1.8 * D_MODEL ** 0.5 / 32) * 32)
LORA_V = max(32, round(1.3 * D_MODEL ** 0.5 / 32) * 32)
LORA_G = max(32, round(0.6 * D_MODEL ** 0.8 / 32) * 32)
LNX_EPS = 1e-5 * C_DIM
N_EXPERTS = 32
N_GROUPS = 4
EXPERTS_PER_GROUP = N_EXPERTS // N_GROUPS
TOP_K = 2
D_EXPERT = D_MODEL // 2
MOE_BLOCK = 256
LN_EPS = 1e-5
ALPHA = (2 * DEPTH) ** 0.25
BETA = (8 * DEPTH) ** -0.25
NEG_INF = -1e30

kernel_name = 'hybrid_diffattn_chunkattn_rwkv7_grouped_moe'


def layer_norm(x, g, b):
    xf = x.astype(jnp.float32)
    mu = xf.mean(-1, keepdims=True)
    var = jnp.square(xf - mu).mean(-1, keepdims=True)
    return ((xf - mu) * lax.rsqrt(var + LN_EPS) * g + b).astype(x.dtype)


def partial_rotary(t, pos):
    half = ROPE_DIM // 2
    inv = ROPE_THETA ** (-jnp.arange(half, dtype=jnp.float32) / half)
    ang = pos.astype(jnp.float32)[:, None] * inv
    cos = jnp.cos(ang)[:, None, None, :]
    sin = jnp.sin(ang)[:, None, None, :]
    tr = t[..., :ROPE_DIM].astype(jnp.float32)
    x1, x2 = tr[..., :half], tr[..., half:]
    rot = jnp.concatenate([x1 * cos - x2 * sin, x2 * cos + x1 * sin], axis=-1).astype(t.dtype)
    return jnp.concatenate([rot, t[..., ROPE_DIM:]], axis=-1)


def diff_attention(q, k, v, lam, lam_init, subln_g):
    bn, s_len = q.shape[0], q.shape[1]
    n_blk = s_len // Q_BLOCK
    scale = A_DIM ** -0.5
    key_chunk = jnp.arange(s_len) // CHUNK
    qb = q.reshape(bn, n_blk, Q_BLOCK, A_HEADS, 2, A_DIM).transpose(1, 0, 2, 3, 4, 5)

    def block(args):
        qi, i = args
        s = jnp.einsum('bqhcd,bkhcd->bhcqk', qi, k).astype(jnp.float32) * scale
        q_chunk = (i * Q_BLOCK + jnp.arange(Q_BLOCK)) // CHUNK
        mask = key_chunk[None, :] <= q_chunk[:, None]
        p = jax.nn.softmax(jnp.where(mask, s, NEG_INF), axis=-1)
        pd = p[:, :, 0] - lam * p[:, :, 1]
        o = jnp.einsum('bhqk,bkhe->bqhe', pd.astype(v.dtype), v).astype(jnp.float32)
        o = o * lax.rsqrt(jnp.mean(jnp.square(o), -1, keepdims=True) + SUBLN_EPS) * subln_g * (1.0 - lam_init)
        return o.astype(v.dtype)

    o = lax.map(block, (qb, jnp.arange(n_blk)))
    return o.transpose(1, 0, 2, 3, 4).reshape(bn, s_len, A_V_WIDTH)


def chunk_rel_attention(q, k, v, rel_bias):
    bn, s_len = q.shape[0], q.shape[1]
    n_chunks = s_len // CHUNK
    back = B_CHUNKS_BACK * CHUNK
    band = back + CHUNK
    scale = B_DIM ** -0.5
    kp = jnp.pad(k, ((0, 0), (back, 0), (0, 0), (0, 0)))
    vp = jnp.pad(v, ((0, 0), (back, 0), (0, 0), (0, 0)))
    rel = jnp.clip(jnp.arange(CHUNK)[:, None] + back - jnp.arange(band)[None, :], -REL_CLIP, REL_CLIP) + REL_CLIP
    bias = rel_bias[:, rel].astype(jnp.float32)

    def chunk(c):
        start = c * CHUNK
        qc = lax.dynamic_slice_in_dim(q, start, CHUNK, axis=1)
        kc = lax.dynamic_slice_in_dim(kp, start, band, axis=1)
        vc = lax.dynamic_slice_in_dim(vp, start, band, axis=1)
        s = jnp.einsum('bqhd,bkhd->bhqk', qc, kc).astype(jnp.float32) * scale + bias
        valid = (start - back + jnp.arange(band)) >= 0
        p = jax.nn.softmax(jnp.where(valid, s, NEG_INF), axis=-1)
        return jnp.einsum('bhqk,bkhd->bqhd', p.astype(v.dtype), vc)

    o = lax.map(chunk, jnp.arange(n_chunks))
    return o.transpose(1, 0, 2, 3, 4).reshape(bn, s_len, B_WIDTH)


def even_mixer(x, pos, w_in, w_out, lam_p, subln_g, rel_bias, lam_init):
    bn, s_len, _ = x.shape
    proj = x @ w_in
    cuts = [A_QK_WIDTH, 2 * A_QK_WIDTH, 2 * A_QK_WIDTH + A_V_WIDTH,
            2 * A_QK_WIDTH + A_V_WIDTH + B_WIDTH, 2 * A_QK_WIDTH + A_V_WIDTH + 2 * B_WIDTH]
    aq, ak, av, bq, bk, bv = jnp.split(proj, cuts, axis=-1)
    aq = partial_rotary(aq.reshape(bn, s_len, A_HEADS, 2, A_DIM), pos)
    ak = partial_rotary(ak.reshape(bn, s_len, A_HEADS, 2, A_DIM), pos)
    av = av.reshape(bn, s_len, A_HEADS, A_VDIM)
    lp = lam_p.astype(jnp.float32)
    lam = jnp.exp(jnp.sum(lp[0] * lp[1])) - jnp.exp(jnp.sum(lp[2] * lp[3])) + lam_init
    ya = diff_attention(aq, ak, av, lam, lam_init, subln_g)
    yb = chunk_rel_attention(bq.reshape(bn, s_len, B_HEADS, B_DIM), bk.reshape(bn, s_len, B_HEADS, B_DIM),
                             bv.reshape(bn, s_len, B_HEADS, B_DIM), rel_bias)
    return jnp.concatenate([ya, yb], axis=-1) @ w_out


def wkv7_scan(r, w, k, v, a, b):
    bn = r.shape[0]
    seq_first = lambda t: jnp.moveaxis(t.astype(jnp.float32), 1, 0)
    s0 = jnp.zeros((bn, C_HEADS, C_DIM, C_DIM), jnp.float32)

    def step(state, inp):
        rt, wt, kt, vt, at, bt = inp
        sa = jnp.einsum('bhvk,bhk->bhv', state, at)
        state = state * wt[:, :, None, :] + sa[..., None] * bt[:, :, None, :] + vt[..., None] * kt[:, :, None, :]
        return state, jnp.einsum('bhvk,bhk->bhv', state, rt)

    _, y = lax.scan(step, s0, (seq_first(r), seq_first(w), seq_first(k), seq_first(v), seq_first(a), seq_first(b)))
    return jnp.moveaxis(y, 0, 1)


def rwkv7_mixer(x, v_first, mix, w_rkv, w0, w1, w2, a0, a1, a2, g1, g2, k_k, k_a, r_k, lnx_g, lnx_b, w_out, vres):
    bn, s_len, d = x.shape
    hs = (bn, s_len, C_HEADS, C_DIM)
    xx = jnp.pad(x, ((0, 0), (1, 0), (0, 0)))[:, :-1] - x
    xr, xw, xk, xv, xa, xg = [x + xx * mix[i] for i in range(6)]
    r = xr @ w_rkv[0]
    k = xk @ w_rkv[1]
    v = xv @ w_rkv[2]
    w = -jax.nn.softplus(-(w0 + jnp.tanh(xw @ w1) @ w2).astype(jnp.float32)) - 0.5
    decay = jnp.exp(-jnp.exp(w))
    if vres is None:
        v_first = v
    else:
        v0, v1, v2 = vres
        v = v + (v_first - v) * jax.nn.sigmoid(v0 + (xv @ v1) @ v2)
    a = jax.nn.sigmoid(a0 + (xa @ a1) @ a2)
    g = jax.nn.sigmoid(xg @ g1) @ g2
    kk = (k * k_k).reshape(hs).astype(jnp.float32)
    kk = kk / jnp.maximum(jnp.sqrt(jnp.sum(jnp.square(kk), -1, keepdims=True)), 1e-12)
    k = k * (1.0 + (a - 1.0) * k_a)
    rh, kh, vh, ah = r.reshape(hs), k.reshape(hs), v.reshape(hs), a.reshape(hs)
    y = wkv7_scan(rh, decay.reshape(hs), kh, vh, -kk, kk * ah.astype(jnp.float32))
    mu = y.mean(-1, keepdims=True)
    var = jnp.square(y - mu).mean(-1, keepdims=True)
    y = ((y - mu) * lax.rsqrt(var + LNX_EPS)).reshape(bn, s_len, d) * lnx_g + lnx_b
    bonus = jnp.sum(rh * kh * r_k, -1, keepdims=True) * vh
    y = y.astype(x.dtype) + bonus.reshape(bn, s_len, d)
    return (y * g) @ w_out, v_first


def route(h2, router_w, router_b):
    n = h2.shape[0]
    scores = jax.nn.sigmoid((h2 @ router_w).astype(jnp.float32))
    sel = scores + router_b.astype(jnp.float32)
    grp_score = lax.top_k(sel.reshape(n, N_GROUPS, EXPERTS_PER_GROUP), 2)[0].sum(-1)
    best = jnp.argmax(grp_score, axis=-1)
    in_group = (jnp.arange(N_EXPERTS) // EXPERTS_PER_GROUP)[None, :] == best[:, None]
    _, idx = lax.top_k(jnp.where(in_group, sel, -jnp.inf), TOP_K)
    gate = jnp.take_along_axis(scores, idx, axis=-1)
    return gate / gate.sum(-1, keepdims=True), idx


def moe_ffn(h, router_w, router_b, w_gate, w_up, w_down):
    bn, s_len, d = h.shape
    n = bn * s_len
    h2 = h.reshape(n, d)
    gate, idx = route(h2, router_w, router_b)
    n_rows = n * TOP_K
    flat_e = idx.reshape(-1)
    flat_tok = jnp.arange(n_rows) // TOP_K
    flat_gate = gate.reshape(-1)
    order = jnp.argsort(flat_e)
    e_sorted = flat_e[order]
    counts = jnp.bincount(flat_e, length=N_EXPERTS)
    padded = (counts + MOE_BLOCK - 1) // MOE_BLOCK * MOE_BLOCK
    pad_end = jnp.cumsum(padded)
    pad_start = pad_end - padded
    start = jnp.cumsum(counts) - counts
    dest = pad_start[e_sorted] + jnp.arange(n_rows) - start[e_sorted]
    cap = n_rows + N_EXPERTS * MOE_BLOCK
    n_blocks = cap // MOE_BLOCK
    row_tok = jnp.full((cap,), n, jnp.int32).at[dest].set(flat_tok[order])
    row_gate = jnp.zeros((cap,), jnp.float32).at[dest].set(flat_gate[order])
    block_e = jnp.minimum(jnp.searchsorted(pad_end, jnp.arange(n_blocks) * MOE_BLOCK, side='right'), N_EXPERTS - 1)
    xb = jnp.concatenate([h2, jnp.zeros((1, d), h2.dtype)], axis=0)[row_tok].reshape(n_blocks, MOE_BLOCK, d)

    def expert_block(args):
        xe, e = args
        u = jax.nn.silu(xe @ w_gate[e]) * (xe @ w_up[e])
        return u @ w_down[e]

    yb = lax.map(expert_block, (xb, block_e)).reshape(cap, d)
    yb = yb * row_gate[:, None].astype(yb.dtype)
    out = jax.ops.segment_sum(yb, row_tok, num_segments=n + 1)[:n]
    return out.reshape(bn, s_len, d)


def setup_inputs(seed: int = 0) -> dict:
    key = jax.random.key(seed)
    ks = iter(jax.random.split(key, 40))
    f32 = jnp.float32
    nrm = lambda shape, scale: jax.random.normal(next(ks), shape, f32) * scale
    uni = lambda shape, lo, hi: jax.random.uniform(next(ks), shape, f32, lo, hi)
    D = D_MODEL
    return {
        'x': nrm((BATCH, SEQ, D), 1.0),
        'ln_g': 1.0 + nrm((DEPTH, 2, D), 0.02),
        'ln_b': nrm((DEPTH, 2, D), 0.02),
        'even_w_in': nrm((N_EVEN, D, IN_WIDTH), D ** -0.5),
        'even_w_out': nrm((N_EVEN, MIX_WIDTH, D), MIX_WIDTH ** -0.5 * BETA),
        'even_lambda': nrm((N_EVEN, 4, A_DIM), 0.1),
        'even_subln_g': 1.0 + nrm((N_EVEN, A_VDIM), 0.02),
        'even_rel_bias': nrm((N_EVEN, B_HEADS, 2 * REL_CLIP + 1), 0.2),
        'odd_mix': uni((N_ODD, 6, D), 0.0, 1.0),
        'odd_w_rkv': nrm((N_ODD, 3, D, D), D ** -0.5),
        'odd_w0': uni((N_ODD, D), -6.0, 1.0),
        'odd_w1': nrm((N_ODD, D, LORA_W), D ** -0.5),
        'odd_w2': nrm((N_ODD, LORA_W, D), 0.5 * LORA_W ** -0.5),
        'odd_a0': nrm((N_ODD, D), 0.1),
        'odd_a1': nrm((N_ODD, D, LORA_A), D ** -0.5),
        'odd_a2': nrm((N_ODD, LORA_A, D), 0.5 * LORA_A ** -0.5),
        'odd_g1': nrm((N_ODD, D, LORA_G), D ** -0.5),
        'odd_g2': nrm((N_ODD, LORA_G, D), LORA_G ** -0.5),
        'odd_k_k': 0.85 + nrm((N_ODD, D), 0.05),
        'odd_k_a': 1.0 + nrm((N_ODD, D), 0.05),
        'odd_r_k': nrm((N_ODD, C_HEADS, C_DIM), 0.1),
        'odd_lnx_g': 1.0 + nrm((N_ODD, D), 0.02),
        'odd_lnx_b': nrm((N_ODD, D), 0.02),
        'odd_w_out': nrm((N_ODD, D, D), D ** -0.5 * BETA),
        'vres_v0': 1.0 + nrm((N_VRES, D), 0.1),
        'vres_v1': nrm((N_VRES, D, LORA_V), D ** -0.5),
        'vres_v2': nrm((N_VRES, LORA_V, D), 0.5 * LORA_V ** -0.5),
        'router_w': nrm((D, N_EXPERTS), D ** -0.5),
        'router_b': nrm((N_EXPERTS,), 0.01),
        'moe_w_gate': nrm((DEPTH, N_EXPERTS, D, D_EXPERT), D ** -0.5),
        'moe_w_up': nrm((DEPTH, N_EXPERTS, D, D_EXPERT), D ** -0.5),
        'moe_w_down': nrm((DEPTH, N_EXPERTS, D_EXPERT, D), D_EXPERT ** -0.5 * BETA),
    }


def reference(x, ln_g, ln_b, even_w_in, even_w_out, even_lambda, even_subln_g, even_rel_bias,
              odd_mix, odd_w_rkv, odd_w0, odd_w1, odd_w2, odd_a0, odd_a1, odd_a2, odd_g1, odd_g2,
              odd_k_k, odd_k_a, odd_r_k, odd_lnx_g, odd_lnx_b, odd_w_out, vres_v0, vres_v1, vres_v2,
              router_w, router_b, moe_w_gate, moe_w_up, moe_w_down):
    pos = jnp.arange(x.shape[1])
    v_first = None
    for layer in range(DEPTH):
        if layer % 2 == 0:
            e = layer // 2
            lam_init = 0.8 - 0.6 * math.exp(-0.3 * layer)
            mix_out = even_mixer(x, pos, even_w_in[e], even_w_out[e], even_lambda[e], even_subln_g[e],
                                 even_rel_bias[e], lam_init)
        else:
            o = layer // 2
            vres = None if o == 0 else (vres_v0[o - 1], vres_v1[o - 1], vres_v2[o - 1])
            mix_out, v_first = rwkv7_mixer(x, v_first, odd_mix[o], odd_w_rkv[o], odd_w0[o], odd_w1[o], odd_w2[o],
                                           odd_a0[o], odd_a1[o], odd_a2[o], odd_g1[o], odd_g2[o], odd_k_k[o],
                                           odd_k_a[o], odd_r_k[o], odd_lnx_g[o], odd_lnx_b[o], odd_w_out[o], vres)
        h = layer_norm(ALPHA * x + mix_out, ln_g[layer, 0], ln_b[layer, 0])
        ffn = moe_ffn(h, router_w, router_b, moe_w_gate[layer], moe_w_up[layer], moe_w_down[layer])
        x = layer_norm(ALPHA * h + ffn, ln_g[layer, 1], ln_b[layer, 1])
    return x
```

```python
import functools
import math

import jax
import jax.numpy as jnp
from jax import lax
from jax.experimental import pallas as pl
from jax.experimental.pallas import tpu as pltpu

F32 = jnp.float32
BF16 = jnp.bfloat16
I32 = jnp.int32

LANES = 128
VMEM_LIMIT_BYTES = 56 * 1024 * 1024

D_MODEL = 1024
DEPTH = 4
CHUNK = 64
A_HEADS = 4
A_DIM = 64
A_VDIM = 128
ROPE_DIM = 16
ROPE_THETA = 500000.0
SUBLN_EPS = 1e-5
B_HEADS = 8
B_DIM = 64
B_CHUNKS_BACK = 8
REL_CLIP = 128
BAND = (B_CHUNKS_BACK + 1) * CHUNK
SEG = 512
C_DIM = 64
LNX_EPS = 1e-5 * C_DIM
N_EXPERTS = 32
N_GROUPS = 4
EXPERTS_PER_GROUP = 8
D_EXPERT = 512
LN_EPS = 1e-5
ALPHA = (2 * DEPTH) ** 0.25
NEG_INF = -1e30

ROW_TILE = 256
ATT_TILE = 512
WKV_ROWS = 256
MOE_TILE = 256
ROUTE_TILE = 512
GATHER_TILE = 256
TOKEN_CHUNKS = D_MODEL // LANES

_NT = (((1,), (1,)), ((), ()))
_TN = (((0,), (0,)), ((), ()))


def _params(*sem):
    return pltpu.CompilerParams(dimension_semantics=sem, vmem_limit_bytes=VMEM_LIMIT_BYTES)


def _dot(a, b):
    return jnp.dot(a.astype(BF16), b.astype(BF16), preferred_element_type=F32)


def _layer_norm(x, g, b):
    mu = jnp.mean(x, axis=-1, keepdims=True)
    xc = x - mu
    var = jnp.mean(xc * xc, axis=-1, keepdims=True)
    return xc * lax.rsqrt(var + LN_EPS) * g + b


def _sigmoid(x):
    return 1.0 / (1.0 + jnp.exp(-x))


def _row_call(body, rows, fulls, outs, tm, name):
    m = rows[0].shape[0] if not isinstance(rows[0], tuple) else None
    in_specs, args = [], []
    for r in rows:
        if isinstance(r, tuple):
            arr, fn = r
            in_specs.append(pl.BlockSpec((tm, arr.shape[1]), lambda i, fn=fn: (fn(i), 0)))
        else:
            arr = r
            m = arr.shape[0]
            in_specs.append(pl.BlockSpec((tm, arr.shape[1]), lambda i: (i, 0)))
        args.append(arr)
    for f in fulls:
        in_specs.append(pl.BlockSpec(f.shape, lambda i, nd=f.ndim: (0,) * nd))
        args.append(f)
    out_specs = [pl.BlockSpec((tm, w), lambda i: (i, 0)) for w, _ in outs]
    out_shape = [jax.ShapeDtypeStruct((m, w), dt) for w, dt in outs]
    return pl.pallas_call(
        body, grid=(m // tm,), in_specs=in_specs, out_specs=out_specs, out_shape=out_shape,
        compiler_params=_params("parallel"), name=name)(*args)


def _out_call(body, rows, fulls, name):
    n, tm = rows[0].shape[0], ROW_TILE
    in_specs = [pl.BlockSpec((tm, r.shape[1]), lambda i: (i, 0)) for r in rows]
    in_specs += [pl.BlockSpec(f.shape, lambda i, nd=f.ndim: (0,) * nd) for f in fulls]
    return pl.pallas_call(
        body, grid=(n // tm,), in_specs=in_specs,
        out_specs=[pl.BlockSpec((tm, D_MODEL), lambda i: (i, 0)), pl.BlockSpec((tm * TOKEN_CHUNKS, LANES), lambda i: (i, 0))],
        out_shape=[jax.ShapeDtypeStruct((n, D_MODEL), F32), jax.ShapeDtypeStruct((n * TOKEN_CHUNKS, LANES), F32)],
        compiler_params=_params("parallel"), name=name)(*rows, *fulls)


def _even_proj_body(x_ref, c_ref, sm_ref, sp_ref, w_ref, aq_ref, ak_ref, av_ref, bq_ref, bk_ref, bv_ref):
    x = x_ref[...].astype(BF16)
    cos, s_minus, s_plus = c_ref[...], sm_ref[...], sp_ref[...]

    def seg(j):
        return jnp.dot(x, w_ref[:, j * SEG:(j + 1) * SEG], preferred_element_type=F32)

    def rotary(t, scale):
        for j in range(SEG // LANES):
            u = t[:, j * LANES:(j + 1) * LANES]
            rot = u * cos + pltpu.roll(u, LANES - ROPE_DIM // 2, 1) * s_minus + pltpu.roll(u, ROPE_DIM // 2, 1) * s_plus
            yield j, rot * scale

    for j, rot in rotary(seg(0), A_DIM ** -0.5):
        aq_ref[:, j * LANES:(j + 1) * LANES] = rot.astype(aq_ref.dtype)
    for j, rot in rotary(seg(1), 1.0):
        ak_ref[:, j * LANES:(j + 1) * LANES] = rot.astype(ak_ref.dtype)
    av_ref[...] = seg(2).astype(av_ref.dtype)
    bq_ref[...] = (seg(3) * B_DIM ** -0.5).astype(bq_ref.dtype)
    bk_ref[...] = seg(4).astype(bk_ref.dtype)
    bv_ref[...] = seg(5).astype(bv_ref.dtype)


def _rotary_tables(seq):
    half = ROPE_DIM // 2
    inv = ROPE_THETA ** (-jnp.arange(half, dtype=F32) / half)
    ang = jnp.arange(seq, dtype=F32)[:, None] * inv
    cos, sin = jnp.cos(ang), jnp.sin(ang)
    d = jnp.arange(LANES) % A_DIM
    first, second = d < half, (d >= half) & (d < ROPE_DIM)
    col = jnp.where(first, d, d - half) % half
    cos_t = jnp.where(first | second, cos[:, col], 1.0)
    s_minus = jnp.where(first, -sin[:, col], 0.0)
    s_plus = jnp.where(second, sin[:, col], 0.0)
    return cos_t, s_minus, s_plus


def _even_proj(x2, w_in, seq):
    tabs = _rotary_tables(seq)
    blocks = seq // ROW_TILE
    rows = [x2] + [(t, lambda i: i % blocks) for t in tabs]
    return _row_call(_even_proj_body, rows, [w_in.astype(BF16)], [(SEG, BF16)] * 6, ROW_TILE, "even_proj")


def _diff_attn_body(qi_ref, kj_ref, lam_ref, q_ref, k_ref, v_ref, g_ref, o_ref, q2_s, m_s, l_s, acc_s, *, lam_init):
    p = pl.program_id(2)
    qi, kj = qi_ref[p], kj_ref[p]
    t = ATT_TILE

    @pl.when(kj == 0)
    def _():
        q = q_ref[0]
        lane = lax.broadcasted_iota(I32, q.shape, 1)
        zero = jnp.zeros_like(q)
        q2_s[0:t, :] = jnp.where(lane < A_DIM, q, zero)
        q2_s[t:, :] = jnp.where(lane >= A_DIM, q, zero)
        m_s[...] = jnp.full(m_s.shape, -jnp.inf, F32)
        l_s[...] = jnp.zeros(l_s.shape, F32)
        acc_s[...] = jnp.zeros(acc_s.shape, F32)

    s = lax.dot_general(q2_s[...], k_ref[0], _NT, preferred_element_type=F32)
    row = lax.broadcasted_iota(I32, s.shape, 0)
    col = lax.broadcasted_iota(I32, s.shape, 1)
    keep = (kj != qi) | ((col // CHUNK) <= ((row % t) // CHUNK))
    s = jnp.where(keep, s, NEG_INF)
    m_prev = m_s[...]
    m_new = jnp.maximum(m_prev, jnp.max(s, axis=1, keepdims=True))
    alpha = jnp.exp(m_prev - m_new)
    pr = jnp.exp(s - m_new)
    l_s[...] = alpha * l_s[...] + jnp.sum(pr, axis=1, keepdims=True)
    acc_s[...] = alpha * acc_s[...] + jnp.dot(pr.astype(BF16), v_ref[0], preferred_element_type=F32)
    m_s[...] = m_new

    @pl.when(kj == qi)
    def _():
        o = acc_s[...] / l_s[...]
        od = o[0:t] - lam_ref[0] * o[t:]
        od = od * lax.rsqrt(jnp.mean(od * od, axis=-1, keepdims=True) + SUBLN_EPS) * g_ref[...] * (1.0 - lam_init)
        o_ref[0] = od.astype(o_ref.dtype)


def _diff_attn(aq, ak, av, lam, subln_g, lam_init):
    bn, seq, _ = aq.shape
    nblk = seq // ATT_TILE
    pairs = [(i, j) for i in range(nblk) for j in range(i + 1)]
    qi = jnp.asarray([p[0] for p in pairs], I32)
    kj = jnp.asarray([p[1] for p in pairs], I32)
    t = ATT_TILE
    grid_spec = pltpu.PrefetchScalarGridSpec(
        num_scalar_prefetch=3, grid=(bn, A_HEADS, len(pairs)),
        in_specs=[
            pl.BlockSpec((1, t, LANES), lambda b, h, p, qi, kj, lam: (b, qi[p], h)),
            pl.BlockSpec((1, t, LANES), lambda b, h, p, qi, kj, lam: (b, kj[p], h)),
            pl.BlockSpec((1, t, LANES), lambda b, h, p, qi, kj, lam: (b, kj[p], h)),
            pl.BlockSpec((1, LANES), lambda b, h, p, qi, kj, lam: (0, 0)),
        ],
        out_specs=pl.BlockSpec((1, t, LANES), lambda b, h, p, qi, kj, lam: (b, qi[p], h)),
        scratch_shapes=[pltpu.VMEM((2 * t, LANES), BF16), pltpu.VMEM((2 * t, 1), F32),
                        pltpu.VMEM((2 * t, 1), F32), pltpu.VMEM((2 * t, LANES), F32)])
    return pl.pallas_call(
        functools.partial(_diff_attn_body, lam_init=lam_init), grid_spec=grid_spec,
        out_shape=jax.ShapeDtypeStruct((bn, seq, A_HEADS * A_VDIM), BF16),
        compiler_params=_params("parallel", "parallel", "arbitrary"), name="diff_attn",
    )(qi, kj, lam.reshape(1).astype(F32), aq, ak, av, subln_g.reshape(1, A_VDIM))


def _chunk_attn_body(q_ref, kp_ref, kc_ref, vp_ref, vc_ref, bias_ref, o_ref, kcat, vcat):
    i = pl.program_id(1)
    t = ATT_TILE
    kcat[0:t, :] = kp_ref[0]
    kcat[t:, :] = kc_ref[0]
    vcat[0:t, :] = vp_ref[0]
    vcat[t:, :] = vc_ref[0]
    lane = lax.broadcasted_iota(I32, (CHUNK, LANES), 1)
    col = lax.broadcasted_iota(I32, (CHUNK, BAND), 1)
    for c in range(t // CHUNK):
        valid = (i > 0) | (col + c * CHUNK >= t)
        for hp in range(B_HEADS // 2):
            lanes = slice(hp * LANES, (hp + 1) * LANES)
            q = q_ref[0, c * CHUNK:(c + 1) * CHUNK, lanes]
            kw = kcat[c * CHUNK:c * CHUNK + BAND, lanes]
            vw = vcat[c * CHUNK:c * CHUNK + BAND, lanes]
            outs = []
            for hh in range(2):
                in_head = (lane < B_DIM) if hh == 0 else (lane >= B_DIM)
                qm = jnp.where(in_head, q, jnp.zeros_like(q))
                s = lax.dot_general(qm, kw, _NT, preferred_element_type=F32) + bias_ref[2 * hp + hh]
                s = jnp.where(valid, s, NEG_INF)
                m = jnp.max(s, axis=1, keepdims=True)
                pr = jnp.exp(s - m)
                den = jnp.sum(pr, axis=1, keepdims=True)
                outs.append(jnp.dot(pr.astype(BF16), vw, preferred_element_type=F32) / den)
            o_ref[0, c * CHUNK:(c + 1) * CHUNK, lanes] = jnp.where(lane < B_DIM, outs[0], outs[1]).astype(o_ref.dtype)


def _chunk_attn(bq, bk, bv, rel_bias):
    bn, seq, width = bq.shape
    t = ATT_TILE
    back = B_CHUNKS_BACK * CHUNK
    rel = jnp.clip(jnp.arange(CHUNK)[:, None] + back - jnp.arange(BAND)[None, :], -REL_CLIP, REL_CLIP) + REL_CLIP
    bias = rel_bias[:, rel].astype(F32)
    cur = pl.BlockSpec((1, t, width), lambda b, i: (b, i, 0))
    prev = pl.BlockSpec((1, t, width), lambda b, i: (b, jnp.maximum(i - 1, 0), 0))
    return pl.pallas_call(
        _chunk_attn_body, grid=(bn, seq // t),
        in_specs=[cur, prev, cur, prev, cur, pl.BlockSpec(bias.shape, lambda b, i: (0, 0, 0))],
        out_specs=cur, out_shape=jax.ShapeDtypeStruct((bn, seq, width), BF16),
        scratch_shapes=[pltpu.VMEM((2 * t, width), BF16), pltpu.VMEM((2 * t, width), BF16)],
        compiler_params=_params("parallel", "parallel"), name="chunk_attn",
    )(bq, bk, bk, bv, bv, bias)


def _even_out_body(x_ref, ya_ref, yb_ref, w_ref, g_ref, b_ref, h_ref, ht_ref):
    mix = jnp.dot(ya_ref[...], w_ref[0:SEG, :], preferred_element_type=F32)
    mix = mix + jnp.dot(yb_ref[...], w_ref[SEG:, :], preferred_element_type=F32)
    h = _layer_norm(ALPHA * x_ref[...] + mix, g_ref[...], b_ref[...])
    h_ref[...] = h
    _store_token_tiles(ht_ref, h)


def _even_mixer(x2, bn, seq, w_in, w_out, lam_p, subln_g, rel_bias, lam_init, ln_g, ln_b):
    aq, ak, av, bq, bk, bv = _even_proj(x2, w_in, seq)
    lp = lam_p.astype(F32)
    lam = jnp.exp(jnp.sum(lp[0] * lp[1])) - jnp.exp(jnp.sum(lp[2] * lp[3])) + lam_init
    r3 = lambda a: a.reshape(bn, seq, SEG)
    ya = _diff_attn(r3(aq), r3(ak), r3(av), lam, subln_g, lam_init)
    yb = _chunk_attn(r3(bq), r3(bk), r3(bv), rel_bias)
    return _out_call(_even_out_body, [x2, ya.reshape(-1, SEG), yb.reshape(-1, SEG)],
                     [w_out.astype(BF16), ln_g.reshape(1, -1), ln_b.reshape(1, -1)], "even_out")


def _rwkv_pre_body(*refs, has_vres):
    if has_vres:
        (x_ref, xp_ref, vf_ref, mix_ref, wr_ref, wk_ref, wv_ref, w0_ref, w1_ref, w2_ref, a0_ref, a1_ref, a2_ref,
         g1_ref, g2_ref, kk_ref, ka_ref, v0_ref, v1_ref, v2_ref, r_o, lw_o, k_o, v_o, kkr_o, a_o, g_o) = refs
    else:
        (x_ref, xp_ref, mix_ref, wr_ref, wk_ref, wv_ref, w0_ref, w1_ref, w2_ref, a0_ref, a1_ref, a2_ref,
         g1_ref, g2_ref, kk_ref, ka_ref, r_o, lw_o, k_o, v_o, kkr_o, a_o, g_o) = refs
    x = x_ref[...]
    xx = xp_ref[...] - x
    xr, xw, xk, xv, xa, xg = [x + xx * mix_ref[i:i + 1, :] for i in range(6)]
    r = _dot(xr, wr_ref[...])
    k = _dot(xk, wk_ref[...])
    v = _dot(xv, wv_ref[...])
    z = w0_ref[...] + _dot(jnp.tanh(_dot(xw, w1_ref[...])), w2_ref[...])
    softplus = jnp.maximum(-z, 0.0) + jnp.log(1.0 + jnp.exp(-jnp.abs(z)))
    lw_o[...] = -jnp.exp(-softplus - 0.5)
    if has_vres:
        v = v + (vf_ref[...] - v) * _sigmoid(v0_ref[...] + _dot(_dot(xv, v1_ref[...]), v2_ref[...]))
    a = _sigmoid(a0_ref[...] + _dot(_dot(xa, a1_ref[...]), a2_ref[...]))
    g_o[...] = _dot(_sigmoid(_dot(xg, g1_ref[...])), g2_ref[...])
    r_o[...] = r
    v_o[...] = v
    a_o[...] = a
    kkr_o[...] = k * kk_ref[...]
    k_o[...] = k * (1.0 + (a - 1.0) * ka_ref[...])


def _head_sum(x, first):
    zero = jnp.zeros_like(x)
    s0 = jnp.sum(jnp.where(first, x, zero), axis=1, keepdims=True)
    s1 = jnp.sum(jnp.where(first, zero, x), axis=1, keepdims=True)
    return jnp.where(first, s0, s1)


def _wkv_body(r_ref, lw_ref, k_ref, v_ref, kk_ref, a_ref, rk_ref, lg_ref, lb_ref, y_ref, z_s):
    n = CHUNK

    @pl.when(pl.program_id(2) == 0)
    def _():
        z_s[...] = jnp.zeros(z_s.shape, F32)

    first = lax.broadcasted_iota(I32, (n, LANES), 1) < C_DIM
    row = lax.broadcasted_iota(I32, (2 * n, 2 * n), 0)
    col = lax.broadcasted_iota(I32, (2 * n, 2 * n), 1)
    same = (row // n) == (col // n)
    strict = same & ((col % n) < (row % n))
    incl = same & ((col % n) <= (row % n))
    eye = (row == col).astype(F32)
    tri = (lax.broadcasted_iota(I32, (n, n), 1) <= lax.broadcasted_iota(I32, (n, n), 0)).astype(BF16)

    def stack(x):
        zero = jnp.zeros_like(x)
        return jnp.concatenate([jnp.where(first, x, zero), jnp.where(first, zero, x)], axis=0)

    for c in range(WKV_ROWS // n):
        rows = slice(c * n, (c + 1) * n)
        r, lw, k, v, kk, a = (ref[0, rows, :] for ref in (r_ref, lw_ref, k_ref, v_ref, kk_ref, a_ref))
        kk = kk / jnp.maximum(jnp.sqrt(_head_sum(kk * kk, first)), 1e-12)
        hi = lw.astype(BF16)
        rem = lw - hi.astype(F32)
        mid = rem.astype(BF16)
        low = (rem - mid.astype(F32)).astype(BF16)
        cum = (jnp.dot(tri, hi, preferred_element_type=F32) + jnp.dot(tri, mid, preferred_element_type=F32)
               + jnp.dot(tri, low, preferred_element_type=F32))
        total = cum[n - 1:n, :]
        g_inv = jnp.exp(-cum)
        g_tail = jnp.exp(total - cum)
        a2 = stack(-kk * jnp.exp(cum - lw))
        r2 = stack(r * jnp.exp(cum))
        b2 = stack(kk * a * g_inv)
        k2 = stack(k * g_inv)
        v2 = stack(v)
        b2t = stack(kk * a * g_tail)
        k2t = stack(k * g_tail)
        m = lax.dot_general(jnp.concatenate([a2, r2], axis=0).astype(BF16),
                            jnp.concatenate([b2, k2], axis=0).astype(BF16), _NT, preferred_element_type=F32)
        zero = jnp.zeros((2 * n, 2 * n), F32)
        a_ab = jnp.where(strict, m[0:2 * n, 0:2 * n], zero)
        a_ak = jnp.where(strict, m[0:2 * n, 2 * n:], zero)
        a_rb = jnp.where(incl, m[2 * n:, 0:2 * n], zero)
        a_rk = jnp.where(incl, m[2 * n:, 2 * n:], zero)
        power = a_ab.astype(BF16)
        inv = eye + a_ab
        for _ in range(int(math.log2(n)) - 1):
            sq = jnp.dot(power, power, preferred_element_type=F32)
            power = sq.astype(BF16)
            inv = inv + jnp.dot(inv.astype(BF16), power, preferred_element_type=F32)
        akv = _dot(a_ak, v2)
        wu = _dot(inv, jnp.concatenate([a2, akv], axis=1))
        lower = jnp.concatenate([zero, v2], axis=1)
        rhs = jnp.concatenate([wu, lower], axis=0).astype(BF16)
        ry = jnp.dot(jnp.concatenate([a_rb, a_rk], axis=1).astype(BF16), rhs, preferred_element_type=F32)
        qs = lax.dot_general(jnp.concatenate([b2t, k2t], axis=0).astype(BF16), rhs, _TN, preferred_element_type=F32)
        z = z_s[...]
        zb = z.astype(BF16)
        y2 = jnp.dot((r2 + ry[:, 0:2 * n]).astype(BF16), zb, preferred_element_type=F32) + ry[:, 2 * n:]
        decay_col = jnp.sum(eye * jnp.exp(total), axis=1, keepdims=True)
        z_s[...] = decay_col * z + jnp.dot(qs[:, 0:2 * n].astype(BF16), zb, preferred_element_type=F32) + qs[:, 2 * n:]
        y = y2[0:n] + y2[n:]
        mu = _head_sum(y, first) * (1.0 / C_DIM)
        yc = y - mu
        var = _head_sum(yc * yc, first) * (1.0 / C_DIM)
        bonus = _head_sum(r * k * rk_ref[...], first) * v
        y_ref[0, rows, :] = yc * lax.rsqrt(var + LNX_EPS) * lg_ref[...] + lb_ref[...] + bonus


def _wkv(r, lw, k, v, kkr, a, r_k, lnx_g, lnx_b):
    bn, seq, d = r.shape
    blk = pl.BlockSpec((1, WKV_ROWS, LANES), lambda b, h, c: (b, c, h))
    par = pl.BlockSpec((1, LANES), lambda b, h, c: (0, h))
    return pl.pallas_call(
        _wkv_body, grid=(bn, d // LANES, seq // WKV_ROWS),
        in_specs=[blk] * 6 + [par] * 3, out_specs=blk,
        out_shape=jax.ShapeDtypeStruct((bn, seq, d), F32),
        scratch_shapes=[pltpu.VMEM((LANES, LANES), F32)],
        compiler_params=_params("parallel", "parallel", "arbitrary"), name="wkv7",
    )(r, lw, k, v, kkr, a, r_k.reshape(1, d), lnx_g.reshape(1, d), lnx_b.reshape(1, d))


def _odd_out_body(x_ref, y_ref, g_ref, w_ref, lg_ref, lb_ref, h_ref, ht_ref):
    mix = _dot(y_ref[...] * g_ref[...], w_ref[...])
    h = _layer_norm(ALPHA * x_ref[...] + mix, lg_ref[...], lb_ref[...])
    h_ref[...] = h
    _store_token_tiles(ht_ref, h)


def _rwkv_mixer(x2, bn, seq, v_first, mix, w_rkv, w0, w1, w2, a0, a1, a2, g1, g2, k_k, k_a, r_k, lnx_g, lnx_b,
                w_out, vres, ln_g, ln_b):
    d = x2.shape[1]
    x3 = x2.reshape(bn, seq, d)
    x_prev = jnp.pad(x3, ((0, 0), (1, 0), (0, 0)))[:, :-1].reshape(-1, d)
    row = lambda p: p.reshape(1, -1)
    bf = lambda p: p.astype(BF16)
    rows = [x2, x_prev] + ([v_first] if vres is not None else [])
    fulls = [mix, bf(w_rkv[0]), bf(w_rkv[1]), bf(w_rkv[2]), row(w0), bf(w1), bf(w2), row(a0), bf(a1), bf(a2),
             bf(g1), bf(g2), row(k_k), row(k_a)]
    if vres is not None:
        fulls += [row(vres[0]), bf(vres[1]), bf(vres[2])]
    r, lw, k, v, kkr, a, g = _row_call(functools.partial(_rwkv_pre_body, has_vres=vres is not None), rows, fulls,
                                       [(d, F32)] * 7, ROW_TILE, "rwkv_pre")
    if vres is None:
        v_first = v
    r3 = lambda t: t.reshape(bn, seq, d)
    y = _wkv(r3(r), r3(lw), r3(k), r3(v), r3(kkr), r3(a), r_k, lnx_g, lnx_b)
    h, ht = _out_call(_odd_out_body, [x2, y.reshape(-1, d), g], [bf(w_out), row(ln_g), row(ln_b)], "odd_out")
    return h, ht, v_first


def _router_body(h_ref, wt_ref, b_ref, upper_ref, idx_ref, gate_ref, rank_ref, cnt_ref, carry_s):
    @pl.when(pl.program_id(0) == 0)
    def _():
        carry_s[...] = jnp.zeros(carry_s.shape, F32)

    logits = lax.dot_general(wt_ref[...], h_ref[...], _NT, precision=lax.Precision.HIGHEST,
                             preferred_element_type=F32)
    scores = _sigmoid(logits)
    sel = scores + b_ref[...]
    sub = lax.broadcasted_iota(I32, sel.shape, 0)
    grp = sub // EXPERTS_PER_GROUP
    ninf = jnp.full(sel.shape, -jnp.inf, F32)
    big = jnp.full(sel.shape, N_EXPERTS, I32)

    def top2(vals):
        m1 = jnp.max(vals, axis=0, keepdims=True)
        i1 = jnp.min(jnp.where(vals == m1, sub, big), axis=0, keepdims=True)
        rest = jnp.where(sub == i1, ninf, vals)
        m2 = jnp.max(rest, axis=0, keepdims=True)
        i2 = jnp.min(jnp.where(rest == m2, sub, big), axis=0, keepdims=True)
        return m1, i1, m2, i2

    best = jnp.zeros((1, sel.shape[1]), I32)
    best_score = None
    for g in range(N_GROUPS):
        m1, _, m2, _ = top2(jnp.where(grp == g, sel, ninf))
        score = m1 + m2
        if g == 0:
            best_score = score
        else:
            better = score > best_score
            best = jnp.where(better, g, best)
            best_score = jnp.where(better, score, best_score)
    _, i1, _, i2 = top2(jnp.where(grp == best, sel, ninf))
    oh1, oh2 = sub == i1, sub == i2
    zero = jnp.zeros_like(scores)
    g1 = jnp.sum(jnp.where(oh1, scores, zero), axis=0, keepdims=True)
    g2 = jnp.sum(jnp.where(oh2, scores, zero), axis=0, keepdims=True)
    den = g1 + g2
    idx_ref[...] = jnp.concatenate([i1, i2], axis=0)
    gate_ref[...] = jnp.concatenate([g1 / den, g2 / den], axis=0)
    both = (oh1 | oh2).astype(BF16)
    before = jnp.dot(both, upper_ref[...], preferred_element_type=F32) + carry_s[...]
    rank1 = jnp.sum(jnp.where(oh1, before, zero), axis=0, keepdims=True)
    rank2 = jnp.sum(jnp.where(oh2, before, zero), axis=0, keepdims=True)
    rank_ref[...] = jnp.concatenate([rank1, rank2], axis=0).astype(I32)
    carry_s[...] = carry_s[...] + jnp.sum(both.astype(F32), axis=1, keepdims=True)
    cnt_ref[...] = carry_s[...]


def _router(h, router_w, router_b):
    n, d = h.shape
    tm = ROUTE_TILE
    upper = (jnp.arange(tm)[:, None] < jnp.arange(tm)[None, :]).astype(BF16)
    tok = pl.BlockSpec((2, tm), lambda i: (0, i))
    return pl.pallas_call(
        _router_body, grid=(n // tm,),
        in_specs=[pl.BlockSpec((tm, d), lambda i: (i, 0)), pl.BlockSpec((N_EXPERTS, d), lambda i: (0, 0)),
                  pl.BlockSpec((N_EXPERTS, 1), lambda i: (0, 0)), pl.BlockSpec((tm, tm), lambda i: (0, 0))],
        out_specs=[tok, tok, tok, pl.BlockSpec((N_EXPERTS, 1), lambda i: (0, 0))],
        out_shape=[jax.ShapeDtypeStruct((2, n), I32), jax.ShapeDtypeStruct((2, n), F32),
                   jax.ShapeDtypeStruct((2, n), I32), jax.ShapeDtypeStruct((N_EXPERTS, 1), F32)],
        scratch_shapes=[pltpu.VMEM((N_EXPERTS, 1), F32)],
        compiler_params=_params("arbitrary"), name="router",
    )(h, router_w.T.astype(F32), router_b.reshape(N_EXPERTS, 1).astype(F32), upper)


def _load_token_tiles(ref, first_row, tokens):
    chunks = [ref[pl.ds(first_row + j, tokens, stride=TOKEN_CHUNKS), :] for j in range(TOKEN_CHUNKS)]
    return jnp.concatenate(chunks, axis=1)


def _store_token_tiles(ref, val):
    tokens = val.shape[0]
    for j in range(TOKEN_CHUNKS):
        ref[pl.ds(j, tokens, stride=TOKEN_CHUNKS), :] = val[:, j * LANES:(j + 1) * LANES].astype(ref.dtype)


def _tile_copy(src, dst, src_tok, dst_tok, sem):
    src_row = pl.multiple_of(src_tok * TOKEN_CHUNKS, TOKEN_CHUNKS)
    dst_row = pl.multiple_of(dst_tok * TOKEN_CHUNKS, TOKEN_CHUNKS)
    return pltpu.make_async_copy(src.at[pl.ds(src_row, TOKEN_CHUNKS), :], dst.at[pl.ds(dst_row, TOKEN_CHUNKS), :], sem)


def _dispatch_body(pad_end_ref, padded_ref, dest_ref, ht_ref, xb_ref, zero_s, sem):
    block_rows = MOE_TILE * TOKEN_CHUNKS
    n_blocks = xb_ref.shape[0] // block_rows

    def block_copy(b):
        return pltpu.make_async_copy(zero_s, xb_ref.at[pl.ds(pl.multiple_of(b * block_rows, block_rows), block_rows), :], sem)

    @pl.when(pl.program_id(0) == 0)
    def _():
        zero_s[...] = jnp.zeros(zero_s.shape, zero_s.dtype)

        def start(e, carry):
            @pl.when(padded_ref[e] > 0)
            def _():
                block_copy(pad_end_ref[e] // MOE_TILE - 1).start()
            return carry

        def wait(e, carry):
            @pl.when(padded_ref[e] > 0)
            def _():
                block_copy(0).wait()
            return carry

        lax.fori_loop(0, N_EXPERTS, start, 0)
        lax.fori_loop(0, N_EXPERTS, wait, 0)
        first_unused = pad_end_ref[N_EXPERTS - 1] // MOE_TILE
        lax.fori_loop(first_unused, n_blocks, lambda b, c: (block_copy(b).start(), c)[1], 0)
        lax.fori_loop(first_unused, n_blocks, lambda b, c: (block_copy(0).wait(), c)[1], 0)

    base = pl.program_id(0) * GATHER_TILE

    def start_row(t, carry):
        _tile_copy(ht_ref, xb_ref, base + t % GATHER_TILE, dest_ref[0, 0, t], sem).start()
        return carry

    def wait_row(t, carry):
        _tile_copy(ht_ref, xb_ref, 0, 0, sem).wait()
        return carry

    lax.fori_loop(0, 2 * GATHER_TILE, start_row, 0, unroll=8)
    lax.fori_loop(0, 2 * GATHER_TILE, wait_row, 0, unroll=8)


def _dispatch(ht, dest_blocks, pad_end, padded, cap):
    n = ht.shape[0] // TOKEN_CHUNKS
    grid_spec = pltpu.PrefetchScalarGridSpec(
        num_scalar_prefetch=2, grid=(n // GATHER_TILE,),
        in_specs=[pl.BlockSpec((1, 1, 2 * GATHER_TILE), lambda i, pe, pd: (i, 0, 0), memory_space=pltpu.SMEM),
                  pl.BlockSpec(memory_space=pl.ANY)],
        out_specs=pl.BlockSpec(memory_space=pl.ANY),
        scratch_shapes=[pltpu.VMEM((MOE_TILE * TOKEN_CHUNKS, LANES), F32), pltpu.SemaphoreType.DMA(())])
    return pl.pallas_call(
        _dispatch_body, grid_spec=grid_spec, out_shape=jax.ShapeDtypeStruct((cap * TOKEN_CHUNKS, LANES), F32),
        compiler_params=_params("arbitrary"), name="moe_dispatch",
    )(pad_end, padded, dest_blocks, ht)


def _expert_body(blk_ref, exp_ref, used_ref, x_ref, wg_ref, wu_ref, wd_ref, y_ref):
    @pl.when(pl.program_id(0) < used_ref[0])
    def _():
        x = _load_token_tiles(x_ref, 0, MOE_TILE).astype(BF16)
        gate = jnp.dot(x, wg_ref[0], preferred_element_type=F32)
        up = jnp.dot(x, wu_ref[0], preferred_element_type=F32)
        u = gate * _sigmoid(gate) * up
        _store_token_tiles(y_ref, jnp.dot(u.astype(BF16), wd_ref[0], preferred_element_type=F32))

    @pl.when(pl.program_id(0) >= used_ref[0])
    def _():
        y_ref[...] = jnp.zeros(y_ref.shape, y_ref.dtype)


def _experts(xb, blk, blk_expert, n_used, w_gate, w_up, w_down):
    d = D_MODEL
    rows = pl.BlockSpec((MOE_TILE * TOKEN_CHUNKS, LANES), lambda i, blk, ex, used: (blk[i], 0))
    out_rows = pl.BlockSpec((MOE_TILE * TOKEN_CHUNKS, LANES), lambda i, blk, ex, used: (i, 0))
    grid_spec = pltpu.PrefetchScalarGridSpec(
        num_scalar_prefetch=3, grid=(xb.shape[0] // (MOE_TILE * TOKEN_CHUNKS),),
        in_specs=[rows,
                  pl.BlockSpec((1, d, D_EXPERT), lambda i, blk, ex, used: (ex[i], 0, 0)),
                  pl.BlockSpec((1, d, D_EXPERT), lambda i, blk, ex, used: (ex[i], 0, 0)),
                  pl.BlockSpec((1, D_EXPERT, d), lambda i, blk, ex, used: (ex[i], 0, 0))],
        out_specs=out_rows)
    return pl.pallas_call(
        _expert_body, grid_spec=grid_spec, out_shape=jax.ShapeDtypeStruct(xb.shape, F32),
        compiler_params=_params("arbitrary"), name="moe_experts",
    )(blk, blk_expert, n_used, xb, w_gate, w_up, w_down)


def _combine_body(dest_ref, h_ref, gate_ref, lg_ref, lb_ref, yb_ref, o_ref, rows_s, sem):
    def start_row(t, carry):
        _tile_copy(yb_ref, rows_s, dest_ref[0, 0, t], t, sem).start()
        return carry

    def wait_row(t, carry):
        _tile_copy(yb_ref, rows_s, 0, 0, sem).wait()
        return carry

    lax.fori_loop(0, 2 * GATHER_TILE, start_row, 0, unroll=8)
    lax.fori_loop(0, 2 * GATHER_TILE, wait_row, 0, unroll=8)
    gate = gate_ref[...]
    ffn = (gate[:, 0:1] * _load_token_tiles(rows_s, 0, GATHER_TILE)
           + gate[:, 1:2] * _load_token_tiles(rows_s, GATHER_TILE * TOKEN_CHUNKS, GATHER_TILE))
    o_ref[...] = _layer_norm(ALPHA * h_ref[...] + ffn, lg_ref[...], lb_ref[...])


def _combine(h, gate_rows, dest_blocks, yb, ln_g, ln_b):
    n, d = h.shape
    tm = GATHER_TILE
    return pl.pallas_call(
        _combine_body, grid=(n // tm,),
        in_specs=[pl.BlockSpec((1, 1, 2 * tm), lambda i: (i, 0, 0), memory_space=pltpu.SMEM),
                  pl.BlockSpec((tm, d), lambda i: (i, 0)), pl.BlockSpec((tm, 2), lambda i: (i, 0)),
                  pl.BlockSpec((1, d), lambda i: (0, 0)), pl.BlockSpec((1, d), lambda i: (0, 0)),
                  pl.BlockSpec(memory_space=pl.ANY)],
        out_specs=pl.BlockSpec((tm, d), lambda i: (i, 0)), out_shape=jax.ShapeDtypeStruct((n, d), F32),
        scratch_shapes=[pltpu.VMEM((2 * tm * TOKEN_CHUNKS, LANES), F32), pltpu.SemaphoreType.DMA(())],
        compiler_params=_params("arbitrary"), name="moe_combine",
    )(dest_blocks, h, gate_rows, ln_g.reshape(1, d), ln_b.reshape(1, d), yb)


def _moe_layer(h, ht, router_w, router_b, w_gate, w_up, w_down, ln_g, ln_b):
    n, d = h.shape
    idx, gate, rank, cnt = _router(h, router_w, router_b)
    counts = cnt[:, 0].astype(I32)
    padded = (counts + MOE_TILE - 1) // MOE_TILE * MOE_TILE
    pad_end = jnp.cumsum(padded).astype(I32)
    pad_start = pad_end - padded
    dest = pad_start[idx] + rank
    cap = 2 * n + N_EXPERTS * MOE_TILE
    n_blocks = cap // MOE_TILE
    n_used = pad_end[-1] // MOE_TILE
    blk = jnp.minimum(jnp.arange(n_blocks, dtype=I32), n_used - 1)
    blk_expert = jnp.minimum(jnp.searchsorted(pad_end, blk * MOE_TILE, side="right"), N_EXPERTS - 1).astype(I32)
    tm = GATHER_TILE
    dest_blocks = dest.reshape(2, n // tm, tm).transpose(1, 0, 2).reshape(n // tm, 1, 2 * tm)
    xb = _dispatch(ht, dest_blocks, pad_end, padded, cap)
    yb = _experts(xb, blk, blk_expert, n_used.reshape(1).astype(I32), w_gate.astype(BF16), w_up.astype(BF16),
                  w_down.astype(BF16))
    return _combine(h, gate.T, dest_blocks, yb, ln_g, ln_b)


def kernel(x, ln_g, ln_b, even_w_in, even_w_out, even_lambda, even_subln_g, even_rel_bias, odd_mix, odd_w_rkv, odd_w0, odd_w1, odd_w2, odd_a0, odd_a1, odd_a2, odd_g1, odd_g2, odd_k_k, odd_k_a, odd_r_k, odd_lnx_g, odd_lnx_b, odd_w_out, vres_v0, vres_v1, vres_v2, router_w, router_b, moe_w_gate, moe_w_up, moe_w_down):
    bn, seq, d = x.shape
    x2 = x.reshape(bn * seq, d)
    v_first = None
    for layer in range(DEPTH):
        if layer % 2 == 0:
            e = layer // 2
            lam_init = 0.8 - 0.6 * math.exp(-0.3 * layer)
            h, ht = _even_mixer(x2, bn, seq, even_w_in[e], even_w_out[e], even_lambda[e], even_subln_g[e],
                            even_rel_bias[e], lam_init, ln_g[layer, 0], ln_b[layer, 0])
        else:
            o = layer // 2
            vres = None if o == 0 else (vres_v0[o - 1], vres_v1[o - 1], vres_v2[o - 1])
            h, ht, v_first = _rwkv_mixer(x2, bn, seq, v_first, odd_mix[o], odd_w_rkv[o], odd_w0[o], odd_w1[o], odd_w2[o],
                                     odd_a0[o], odd_a1[o], odd_a2[o], odd_g1[o], odd_g2[o], odd_k_k[o], odd_k_a[o],
                                     odd_r_k[o], odd_lnx_g[o], odd_lnx_b[o], odd_w_out[o], vres,
                                     ln_g[layer, 0], ln_b[layer, 0])
        x2 = _moe_layer(h, ht, router_w, router_b, moe_w_gate[layer], moe_w_up[layer], moe_w_down[layer],
                        ln_g[layer, 1], ln_b[layer, 1])
    return x2.reshape(bn, seq, d)
```

```python
import functools
import math

import jax
import jax.numpy as jnp
from jax import lax
from jax.experimental import pallas as pl
from jax.experimental.pallas import tpu as pltpu

F32 = jnp.float32
BF16 = jnp.bfloat16
I32 = jnp.int32

LANES = 128
VMEM_LIMIT_BYTES = 56 * 1024 * 1024

D_MODEL = 1024
DEPTH = 4
CHUNK = 64
A_HEADS = 4
A_DIM = 64
A_VDIM = 128
ROPE_DIM = 16
ROPE_THETA = 500000.0
SUBLN_EPS = 1e-5
B_HEADS = 8
B_DIM = 64
B_CHUNKS_BACK = 8
REL_CLIP = 128
BAND = (B_CHUNKS_BACK + 1) * CHUNK
SEG = 512
C_DIM = 64
LNX_EPS = 1e-5 * C_DIM
N_EXPERTS = 32
N_GROUPS = 4
EXPERTS_PER_GROUP = 8
D_EXPERT = 512
LN_EPS = 1e-5
ALPHA = (2 * DEPTH) ** 0.25
NEG_INF = -1e30

ROW_TILE = 256
ATT_TILE = 1024
V_ROWS = A_VDIM + 16
Q_SCALE = A_DIM ** -0.5 * math.log2(math.e)
BAND_TILE = B_CHUNKS_BACK * CHUNK
WKV_ROWS = 512
MOE_TILE = 256
ROUTE_TILE = 512
GATHER_TILE = 256
TOKEN_CHUNKS = D_MODEL // LANES

_NT = (((1,), (1,)), ((), ()))
_TN = (((0,), (0,)), ((), ()))


def _params(*sem):
    return pltpu.CompilerParams(dimension_semantics=sem, vmem_limit_bytes=VMEM_LIMIT_BYTES)


def _dot(a, b):
    return jnp.dot(a.astype(BF16), b.astype(BF16), preferred_element_type=F32)


def _layer_norm(x, g, b):
    mu = jnp.mean(x, axis=-1, keepdims=True)
    xc = x - mu
    var = jnp.mean(xc * xc, axis=-1, keepdims=True)
    return xc * lax.rsqrt(var + LN_EPS) * g + b


def _sigmoid(x):
    return 1.0 / (1.0 + jnp.exp(-x))


def _row_call(body, rows, fulls, outs, tm, name):
    m = rows[0].shape[0] if not isinstance(rows[0], tuple) else None
    in_specs, args = [], []
    for r in rows:
        if isinstance(r, tuple):
            arr, fn = r
            in_specs.append(pl.BlockSpec((tm, arr.shape[1]), lambda i, fn=fn: (fn(i), 0)))
        else:
            arr = r
            m = arr.shape[0]
            in_specs.append(pl.BlockSpec((tm, arr.shape[1]), lambda i: (i, 0)))
        args.append(arr)
    for f in fulls:
        in_specs.append(pl.BlockSpec(f.shape, lambda i, nd=f.ndim: (0,) * nd))
        args.append(f)
    out_specs = [pl.BlockSpec((tm, w), lambda i: (i, 0)) for w, _ in outs]
    out_shape = [jax.ShapeDtypeStruct((m, w), dt) for w, dt in outs]
    return pl.pallas_call(
        body, grid=(m // tm,), in_specs=in_specs, out_specs=out_specs, out_shape=out_shape,
        compiler_params=_params("parallel"), name=name)(*args)


def _out_call(body, rows, fulls, name):
    n, tm = rows[0].shape[0], ROW_TILE
    in_specs = [pl.BlockSpec((tm, r.shape[1]), lambda i: (i, 0)) for r in rows]
    in_specs += [pl.BlockSpec(f.shape, lambda i, nd=f.ndim: (0,) * nd) for f in fulls]
    return pl.pallas_call(
        body, grid=(n // tm,), in_specs=in_specs,
        out_specs=[pl.BlockSpec((tm, D_MODEL), lambda i: (i, 0)), pl.BlockSpec((tm * TOKEN_CHUNKS, LANES), lambda i: (i, 0))],
        out_shape=[jax.ShapeDtypeStruct((n, D_MODEL), F32), jax.ShapeDtypeStruct((n * TOKEN_CHUNKS, LANES), F32)],
        compiler_params=_params("parallel"), name=name)(*rows, *fulls)


def _even_proj_body(x_ref, c_ref, sm_ref, sp_ref, ct_ref, smt_ref, spt_ref, w_ref, wt_ref,
                    aq_ref, ak_ref, av_ref, bq_ref, bk_ref, bv_ref):
    x = x_ref[...]
    xb = x.astype(BF16)
    half = ROPE_DIM // 2

    def seg(j):
        return jnp.dot(xb, w_ref[:, j * SEG:(j + 1) * SEG], preferred_element_type=F32)

    qv_t = jnp.dot(wt_ref[...], x.T.astype(BF16), preferred_element_type=F32)
    cos, s_minus, s_plus = ct_ref[...], smt_ref[...], spt_ref[...]
    for j in range(SEG // LANES):
        u = qv_t[j * LANES:(j + 1) * LANES, :]
        rot = u * cos + pltpu.roll(u, LANES - half, 0) * s_minus + pltpu.roll(u, half, 0) * s_plus
        aq_ref[j * LANES:(j + 1) * LANES, :] = (rot * Q_SCALE).astype(aq_ref.dtype)
    for h in range(A_HEADS):
        av_ref[h * V_ROWS:h * V_ROWS + A_VDIM, :] = qv_t[SEG + h * A_VDIM:SEG + (h + 1) * A_VDIM, :].astype(av_ref.dtype)
        av_ref[h * V_ROWS + A_VDIM:(h + 1) * V_ROWS, :] = jnp.ones((V_ROWS - A_VDIM, x.shape[0]), av_ref.dtype)
    cos, s_minus, s_plus = c_ref[...], sm_ref[...], sp_ref[...]
    k = seg(1)
    for j in range(SEG // LANES):
        u = k[:, j * LANES:(j + 1) * LANES]
        rot = u * cos + pltpu.roll(u, LANES - half, 1) * s_minus + pltpu.roll(u, half, 1) * s_plus
        ak_ref[:, j * LANES:(j + 1) * LANES] = rot.astype(ak_ref.dtype)
    bq_ref[...] = (seg(3) * B_DIM ** -0.5).astype(bq_ref.dtype)
    bk_ref[...] = seg(4).astype(bk_ref.dtype)
    bv_ref[...] = seg(5).astype(bv_ref.dtype)


def _rotary_tables(seq):
    half = ROPE_DIM // 2
    inv = ROPE_THETA ** (-jnp.arange(half, dtype=F32) / half)
    ang = jnp.arange(seq, dtype=F32)[:, None] * inv
    cos, sin = jnp.cos(ang), jnp.sin(ang)
    d = jnp.arange(LANES) % A_DIM
    first, second = d < half, (d >= half) & (d < ROPE_DIM)
    col = jnp.where(first, d, d - half) % half
    cos_t = jnp.where(first | second, cos[:, col], 1.0)
    s_minus = jnp.where(first, -sin[:, col], 0.0)
    s_plus = jnp.where(second, sin[:, col], 0.0)
    return cos_t, s_minus, s_plus


def _even_proj(x2, w_in, tabs, seq):
    n, d = x2.shape
    tm = ROW_TILE
    blocks = seq // tm
    w = w_in.astype(BF16)
    w_qv_t = jnp.concatenate([w[:, 0:SEG], w[:, 2 * SEG:3 * SEG]], axis=1).T
    tabs_t = [t.T for t in tabs]
    rows = pl.BlockSpec((tm, SEG), lambda i: (i, 0))
    cols = lambda r: pl.BlockSpec((r, tm), lambda i: (0, i))
    full = lambda a: pl.BlockSpec(a.shape, lambda i: (0, 0))
    shapes = [(SEG, n), (n, SEG), (A_HEADS * V_ROWS, n), (n, SEG), (n, SEG), (n, SEG)]
    return pl.pallas_call(
        _even_proj_body, grid=(n // tm,),
        in_specs=[pl.BlockSpec((tm, d), lambda i: (i, 0))]
        + [pl.BlockSpec((tm, LANES), lambda i: (i % blocks, 0))] * 3
        + [pl.BlockSpec((LANES, tm), lambda i: (0, i % blocks))] * 3 + [full(w), full(w_qv_t)],
        out_specs=[cols(SEG), rows, cols(A_HEADS * V_ROWS), rows, rows, rows],
        out_shape=[jax.ShapeDtypeStruct(s, BF16) for s in shapes],
        compiler_params=_params("parallel"), name="even_proj")(x2, *tabs, *tabs_t, w, w_qv_t)


def _diff_attn_body(qi_ref, kj_ref, lam_ref, q_ref, k_ref, v_ref, g_ref, o_ref, q2_s, m_s, l_s, acc_s, *, lam_init):
    p = pl.program_id(2)
    qi, kj = qi_ref[p], kj_ref[p]
    groups = ATT_TILE // LANES

    @pl.when(kj == 0)
    def _():
        feat = lax.broadcasted_iota(I32, (LANES, LANES), 0)
        for j in range(groups):
            q = q_ref[:, j * LANES:(j + 1) * LANES]
            zero = jnp.zeros_like(q)
            q2_s[j, :, 0:LANES] = jnp.where(feat < A_DIM, q, zero)
            q2_s[j, :, LANES:] = jnp.where(feat >= A_DIM, q, zero)
        m_s[...] = jnp.full(m_s.shape, -jnp.inf, F32)
        l_s[...] = jnp.zeros(l_s.shape, F32)
        acc_s[...] = jnp.zeros(acc_s.shape, F32)

    def scores(j):
        return jnp.dot(k_ref[0], q2_s[j], preferred_element_type=F32)

    def step(diagonal):
        s_next = scores(0)
        for j in range(groups):
            s = s_next
            if j + 1 < groups:
                s_next = scores(j + 1)
            if diagonal:
                key_chunk = lax.broadcasted_iota(I32, s.shape, 0) // CHUNK
                query_chunk = (lax.broadcasted_iota(I32, s.shape, 1) % LANES + j * LANES) // CHUNK
                s = jnp.where(key_chunk <= query_chunk, s, NEG_INF)
            m = m_s[j]
            m_new = jnp.maximum(m, jnp.max(s, axis=0, keepdims=True))
            alpha = jnp.exp2(m - m_new)
            pr = jnp.exp2(s - m_new).astype(BF16)
            pv = jnp.dot(v_ref[...], pr, preferred_element_type=F32)
            acc_s[j] = alpha * acc_s[j] + pv[0:A_VDIM]
            l_s[j] = alpha * l_s[j] + pv[A_VDIM:A_VDIM + 1]
            m_s[j] = m_new

    @pl.when(kj != qi)
    def _():
        step(False)

    @pl.when(kj == qi)
    def _():
        step(True)
        for j in range(groups):
            o = acc_s[j] / l_s[j]
            o = o[:, 0:LANES] - lam_ref[0] * o[:, LANES:]
            od = o.T
            od = od * lax.rsqrt(jnp.mean(od * od, axis=-1, keepdims=True) + SUBLN_EPS) * g_ref[...] * (1.0 - lam_init)
            o_ref[0, j * LANES:(j + 1) * LANES, :] = od.astype(o_ref.dtype)


def _diff_attn(aq_t, ak, av_t, lam, subln_g, lam_init):
    bn, seq, _ = ak.shape
    t = ATT_TILE
    nblk = seq // t
    pairs = [(i, j) for i in range(nblk) for j in range(i + 1)]
    qi = jnp.asarray([p[0] for p in pairs], I32)
    kj = jnp.asarray([p[1] for p in pairs], I32)
    groups, width = t // LANES, 2 * LANES
    grid_spec = pltpu.PrefetchScalarGridSpec(
        num_scalar_prefetch=3, grid=(bn, A_HEADS, len(pairs)),
        in_specs=[
            pl.BlockSpec((LANES, t), lambda b, h, p, qi, kj, lam: (h, b * nblk + qi[p])),
            pl.BlockSpec((1, t, LANES), lambda b, h, p, qi, kj, lam: (b, kj[p], h)),
            pl.BlockSpec((V_ROWS, t), lambda b, h, p, qi, kj, lam: (h, b * nblk + kj[p])),
            pl.BlockSpec((1, LANES), lambda b, h, p, qi, kj, lam: (0, 0)),
        ],
        out_specs=pl.BlockSpec((1, t, LANES), lambda b, h, p, qi, kj, lam: (b, qi[p], h)),
        scratch_shapes=[pltpu.VMEM((groups, LANES, width), BF16), pltpu.VMEM((groups, 1, width), F32),
                        pltpu.VMEM((groups, 1, width), F32), pltpu.VMEM((groups, A_VDIM, width), F32)])
    return pl.pallas_call(
        functools.partial(_diff_attn_body, lam_init=lam_init), grid_spec=grid_spec,
        out_shape=jax.ShapeDtypeStruct((bn, seq, A_HEADS * A_VDIM), BF16),
        compiler_params=_params("parallel", "parallel", "arbitrary"), name="diff_attn",
    )(qi, kj, lam.reshape(1).astype(F32), aq_t, ak, av_t, subln_g.reshape(1, A_VDIM))


def _chunk_attn_body(q_ref, kp_ref, kc_ref, vp_ref, vc_ref, bias_ref, o_ref, kcat, vcat):
    i = pl.program_id(1)
    t = BAND_TILE
    kcat[0:t, :] = kp_ref[0]
    kcat[t:, :] = kc_ref[0]
    vcat[0:t, :] = vp_ref[0]
    vcat[t:, :] = vc_ref[0]
    lane = lax.broadcasted_iota(I32, (CHUNK, LANES), 1)
    col = lax.broadcasted_iota(I32, (CHUNK, BAND), 1)
    for c in range(t // CHUNK):
        valid = (i > 0) | (col + c * CHUNK >= t)
        for hp in range(B_HEADS // 2):
            lanes = slice(hp * LANES, (hp + 1) * LANES)
            q = q_ref[0, c * CHUNK:(c + 1) * CHUNK, lanes]
            kw = kcat[c * CHUNK:c * CHUNK + BAND, lanes]
            vw = vcat[c * CHUNK:c * CHUNK + BAND, lanes]
            outs = []
            for hh in range(2):
                in_head = (lane < B_DIM) if hh == 0 else (lane >= B_DIM)
                qm = jnp.where(in_head, q, jnp.zeros_like(q))
                s = lax.dot_general(qm, kw, _NT, preferred_element_type=F32) + bias_ref[2 * hp + hh]
                s = jnp.where(valid, s, NEG_INF)
                m = jnp.max(s, axis=1, keepdims=True)
                pr = jnp.exp(s - m)
                den = jnp.sum(pr, axis=1, keepdims=True)
                outs.append(jnp.dot(pr.astype(BF16), vw, preferred_element_type=F32) / den)
            o_ref[0, c * CHUNK:(c + 1) * CHUNK, lanes] = jnp.where(lane < B_DIM, outs[0], outs[1]).astype(o_ref.dtype)


def _chunk_attn(bq, bk, bv, rel_bias):
    bn, seq, width = bq.shape
    t = BAND_TILE
    back = B_CHUNKS_BACK * CHUNK
    rel = jnp.clip(jnp.arange(CHUNK)[:, None] + back - jnp.arange(BAND)[None, :], -REL_CLIP, REL_CLIP) + REL_CLIP
    bias = rel_bias[:, rel].astype(F32)
    cur = pl.BlockSpec((1, t, width), lambda b, i: (b, i, 0))
    prev = pl.BlockSpec((1, t, width), lambda b, i: (b, jnp.maximum(i - 1, 0), 0))
    return pl.pallas_call(
        _chunk_attn_body, grid=(bn, seq // t),
        in_specs=[cur, prev, cur, prev, cur, pl.BlockSpec(bias.shape, lambda b, i: (0, 0, 0))],
        out_specs=cur, out_shape=jax.ShapeDtypeStruct((bn, seq, width), BF16),
        scratch_shapes=[pltpu.VMEM((2 * t, width), BF16), pltpu.VMEM((2 * t, width), BF16)],
        compiler_params=_params("parallel", "parallel"), name="chunk_attn",
    )(bq, bk, bk, bv, bv, bias)


def _even_out_body(x_ref, ya_ref, yb_ref, w_ref, g_ref, b_ref, h_ref, ht_ref):
    mix = jnp.dot(ya_ref[...], w_ref[0:SEG, :], preferred_element_type=F32)
    mix = mix + jnp.dot(yb_ref[...], w_ref[SEG:, :], preferred_element_type=F32)
    h = _layer_norm(ALPHA * x_ref[...] + mix, g_ref[...], b_ref[...])
    h_ref[...] = h
    _store_token_tiles(ht_ref, h)


def _even_mixer(x2, bn, seq, rotary_tables, w_in, w_out, lam_p, subln_g, rel_bias, lam_init, ln_g, ln_b):
    aq_t, ak, av_t, bq, bk, bv = _even_proj(x2, w_in, rotary_tables, seq)
    lp = lam_p.astype(F32)
    lam = jnp.exp(jnp.sum(lp[0] * lp[1])) - jnp.exp(jnp.sum(lp[2] * lp[3])) + lam_init
    r3 = lambda a: a.reshape(bn, seq, SEG)
    ya = _diff_attn(aq_t, r3(ak), av_t, lam, subln_g, lam_init)
    yb = _chunk_attn(r3(bq), r3(bk), r3(bv), rel_bias)
    return _out_call(_even_out_body, [x2, ya.reshape(-1, SEG), yb.reshape(-1, SEG)],
                     [w_out.astype(BF16), ln_g.reshape(1, -1), ln_b.reshape(1, -1)], "even_out")


def _rwkv_pre_body(*refs, has_vres):
    if has_vres:
        (x_ref, xp_ref, vf_ref, mix_ref, wr_ref, wk_ref, wv_ref, w0_ref, w1_ref, w2_ref, a0_ref, a1_ref, a2_ref,
         g1_ref, g2_ref, kk_ref, ka_ref, v0_ref, v1_ref, v2_ref, r_o, lw_o, k_o, v_o, kkr_o, a_o, g_o) = refs
    else:
        (x_ref, xp_ref, mix_ref, wr_ref, wk_ref, wv_ref, w0_ref, w1_ref, w2_ref, a0_ref, a1_ref, a2_ref,
         g1_ref, g2_ref, kk_ref, ka_ref, r_o, lw_o, k_o, v_o, kkr_o, a_o, g_o) = refs
    x = x_ref[...]
    xx = xp_ref[...] - x
    xr, xw, xk, xv, xa, xg = [x + xx * mix_ref[i:i + 1, :] for i in range(6)]
    r = _dot(xr, wr_ref[...])
    k = _dot(xk, wk_ref[...])
    v = _dot(xv, wv_ref[...])
    z = w0_ref[...] + _dot(jnp.tanh(_dot(xw, w1_ref[...])), w2_ref[...])
    softplus = jnp.maximum(-z, 0.0) + jnp.log(1.0 + jnp.exp(-jnp.abs(z)))
    lw_o[...] = -jnp.exp(-softplus - 0.5)
    if has_vres:
        v = v + (vf_ref[...] - v) * _sigmoid(v0_ref[...] + _dot(_dot(xv, v1_ref[...]), v2_ref[...]))
    a = _sigmoid(a0_ref[...] + _dot(_dot(xa, a1_ref[...]), a2_ref[...]))
    g_o[...] = _dot(_sigmoid(_dot(xg, g1_ref[...])), g2_ref[...])
    r_o[...] = r
    v_o[...] = v
    a_o[...] = a
    kkr_o[...] = k * kk_ref[...]
    k_o[...] = k * (1.0 + (a - 1.0) * ka_ref[...])


def _head_sum(x, first):
    zero = jnp.zeros_like(x)
    s0 = jnp.sum(jnp.where(first, x, zero), axis=1, keepdims=True)
    s1 = jnp.sum(jnp.where(first, zero, x), axis=1, keepdims=True)
    return jnp.where(first, s0, s1)


def _wkv_body(r_ref, lw_ref, k_ref, v_ref, kk_ref, a_ref, rk_ref, lg_ref, lb_ref, y_ref, z_s):
    n = CHUNK

    @pl.when(pl.program_id(2) == 0)
    def _():
        z_s[...] = jnp.zeros(z_s.shape, F32)

    first = lax.broadcasted_iota(I32, (n, LANES), 1) < C_DIM
    row = lax.broadcasted_iota(I32, (2 * n, 2 * n), 0)
    col = lax.broadcasted_iota(I32, (2 * n, 2 * n), 1)
    same = (row // n) == (col // n)
    strict = same & ((col % n) < (row % n))
    incl = same & ((col % n) <= (row % n))
    eye = (row == col).astype(F32)
    tri = (lax.broadcasted_iota(I32, (n, n), 1) <= lax.broadcasted_iota(I32, (n, n), 0)).astype(BF16)

    def stack(x):
        zero = jnp.zeros_like(x)
        return jnp.concatenate([jnp.where(first, x, zero), jnp.where(first, zero, x)], axis=0)

    zero = jnp.zeros((2 * n, 2 * n), F32)
    chunks = range(WKV_ROWS // n)
    prep = []
    for c in chunks:
        rows = slice(c * n, (c + 1) * n)
        r, lw, k, v, kk, a = (ref[0, rows, :] for ref in (r_ref, lw_ref, k_ref, v_ref, kk_ref, a_ref))
        kk = kk / jnp.maximum(jnp.sqrt(_head_sum(kk * kk, first)), 1e-12)
        hi = lw.astype(BF16)
        rem = lw - hi.astype(F32)
        mid = rem.astype(BF16)
        low = (rem - mid.astype(F32)).astype(BF16)
        cum = (jnp.dot(tri, hi, preferred_element_type=F32) + jnp.dot(tri, mid, preferred_element_type=F32)
               + jnp.dot(tri, low, preferred_element_type=F32))
        total = cum[n - 1:n, :]
        g_inv = jnp.exp(-cum)
        g_tail = jnp.exp(total - cum)
        a2 = stack(-kk * jnp.exp(cum - lw))
        r2 = stack(r * jnp.exp(cum))
        b2 = stack(kk * a * g_inv)
        k2 = stack(k * g_inv)
        m = lax.dot_general(jnp.concatenate([a2, r2], axis=0).astype(BF16),
                            jnp.concatenate([b2, k2], axis=0).astype(BF16), _NT, preferred_element_type=F32)
        prep.append(dict(
            a2=a2, r2=r2, v2=stack(v), total=total,
            tails=jnp.concatenate([stack(kk * a * g_tail), stack(k * g_tail)], axis=0).astype(BF16),
            a_ab=jnp.where(strict, m[0:2 * n, 0:2 * n], zero), a_ak=jnp.where(strict, m[0:2 * n, 2 * n:], zero),
            a_r=jnp.concatenate([jnp.where(incl, m[2 * n:, 0:2 * n], zero),
                                 jnp.where(incl, m[2 * n:, 2 * n:], zero)], axis=1).astype(BF16)))
    power = [p["a_ab"].astype(BF16) for p in prep]
    inv = [eye + p["a_ab"] for p in prep]
    for _ in range(int(math.log2(n)) - 1):
        power = [jnp.dot(pw, pw, preferred_element_type=F32).astype(BF16) for pw in power]
        inv = [iv + jnp.dot(iv.astype(BF16), pw, preferred_element_type=F32) for iv, pw in zip(inv, power)]
    akv = [_dot(p["a_ak"], p["v2"]) for p in prep]
    wu = [_dot(iv, jnp.concatenate([p["a2"], x], axis=1)) for iv, p, x in zip(inv, prep, akv)]
    rhs = [jnp.concatenate([w, jnp.concatenate([zero, p["v2"]], axis=1)], axis=0).astype(BF16)
           for w, p in zip(wu, prep)]
    ry = [jnp.dot(p["a_r"], x, preferred_element_type=F32) for p, x in zip(prep, rhs)]
    qs = [lax.dot_general(p["tails"], x, _TN, preferred_element_type=F32) for p, x in zip(prep, rhs)]
    r_eff = [(p["r2"] + x[:, 0:2 * n]).astype(BF16) for p, x in zip(prep, ry)]
    decay_col = [jnp.sum(eye * jnp.exp(p["total"]), axis=1, keepdims=True) for p in prep]
    z = z_s[...]
    y2 = []
    for c in chunks:
        zb = z.astype(BF16)
        y2.append(jnp.dot(r_eff[c], zb, preferred_element_type=F32) + ry[c][:, 2 * n:])
        z = decay_col[c] * z + jnp.dot(qs[c][:, 0:2 * n].astype(BF16), zb, preferred_element_type=F32) + qs[c][:, 2 * n:]
    z_s[...] = z
    for c in chunks:
        rows = slice(c * n, (c + 1) * n)
        y = y2[c][0:n] + y2[c][n:]
        mu = _head_sum(y, first) * (1.0 / C_DIM)
        yc = y - mu
        var = _head_sum(yc * yc, first) * (1.0 / C_DIM)
        bonus = _head_sum(r_ref[0, rows, :] * k_ref[0, rows, :] * rk_ref[...], first) * v_ref[0, rows, :]
        y_ref[0, rows, :] = yc * lax.rsqrt(var + LNX_EPS) * lg_ref[...] + lb_ref[...] + bonus


def _wkv(r, lw, k, v, kkr, a, r_k, lnx_g, lnx_b):
    bn, seq, d = r.shape
    blk = pl.BlockSpec((1, WKV_ROWS, LANES), lambda b, h, c: (b, c, h))
    par = pl.BlockSpec((1, LANES), lambda b, h, c: (0, h))
    return pl.pallas_call(
        _wkv_body, grid=(bn, d // LANES, seq // WKV_ROWS),
        in_specs=[blk] * 6 + [par] * 3, out_specs=blk,
        out_shape=jax.ShapeDtypeStruct((bn, seq, d), F32),
        scratch_shapes=[pltpu.VMEM((LANES, LANES), F32)],
        compiler_params=_params("parallel", "parallel", "arbitrary"), name="wkv7",
    )(r, lw, k, v, kkr, a, r_k.reshape(1, d), lnx_g.reshape(1, d), lnx_b.reshape(1, d))


def _odd_out_body(x_ref, y_ref, g_ref, w_ref, lg_ref, lb_ref, h_ref, ht_ref):
    mix = _dot(y_ref[...] * g_ref[...], w_ref[...])
    h = _layer_norm(ALPHA * x_ref[...] + mix, lg_ref[...], lb_ref[...])
    h_ref[...] = h
    _store_token_tiles(ht_ref, h)


def _rwkv_mixer(x2, bn, seq, v_first, mix, w_rkv, w0, w1, w2, a0, a1, a2, g1, g2, k_k, k_a, r_k, lnx_g, lnx_b,
                w_out, vres, ln_g, ln_b):
    d = x2.shape[1]
    x3 = x2.reshape(bn, seq, d)
    x_prev = jnp.pad(x3, ((0, 0), (1, 0), (0, 0)))[:, :-1].reshape(-1, d)
    row = lambda p: p.reshape(1, -1)
    bf = lambda p: p.astype(BF16)
    rows = [x2, x_prev] + ([v_first] if vres is not None else [])
    fulls = [mix, bf(w_rkv[0]), bf(w_rkv[1]), bf(w_rkv[2]), row(w0), bf(w1), bf(w2), row(a0), bf(a1), bf(a2),
             bf(g1), bf(g2), row(k_k), row(k_a)]
    if vres is not None:
        fulls += [row(vres[0]), bf(vres[1]), bf(vres[2])]
    r, lw, k, v, kkr, a, g = _row_call(functools.partial(_rwkv_pre_body, has_vres=vres is not None), rows, fulls,
                                       [(d, F32)] * 7, ROW_TILE, "rwkv_pre")
    if vres is None:
        v_first = v
    r3 = lambda t: t.reshape(bn, seq, d)
    y = _wkv(r3(r), r3(lw), r3(k), r3(v), r3(kkr), r3(a), r_k, lnx_g, lnx_b)
    h, ht = _out_call(_odd_out_body, [x2, y.reshape(-1, d), g], [bf(w_out), row(ln_g), row(ln_b)], "odd_out")
    return h, ht, v_first


def _router_body(h_ref, wt_ref, b_ref, upper_ref, idx_ref, gate_ref, rank_ref, cnt_ref, carry_s):
    @pl.when(pl.program_id(0) == 0)
    def _():
        carry_s[...] = jnp.zeros(carry_s.shape, F32)

    logits = lax.dot_general(wt_ref[...], h_ref[...], _NT, precision=lax.Precision.HIGHEST,
                             preferred_element_type=F32)
    scores = _sigmoid(logits)
    sel = scores + b_ref[...]
    sub = lax.broadcasted_iota(I32, sel.shape, 0)
    grp = sub // EXPERTS_PER_GROUP
    ninf = jnp.full(sel.shape, -jnp.inf, F32)
    big = jnp.full(sel.shape, N_EXPERTS, I32)

    def top2(vals):
        m1 = jnp.max(vals, axis=0, keepdims=True)
        i1 = jnp.min(jnp.where(vals == m1, sub, big), axis=0, keepdims=True)
        rest = jnp.where(sub == i1, ninf, vals)
        m2 = jnp.max(rest, axis=0, keepdims=True)
        i2 = jnp.min(jnp.where(rest == m2, sub, big), axis=0, keepdims=True)
        return m1, i1, m2, i2

    best = jnp.zeros((1, sel.shape[1]), I32)
    best_score = None
    for g in range(N_GROUPS):
        m1, _, m2, _ = top2(jnp.where(grp == g, sel, ninf))
        score = m1 + m2
        if g == 0:
            best_score = score
        else:
            better = score > best_score
            best = jnp.where(better, g, best)
            best_score = jnp.where(better, score, best_score)
    _, i1, _, i2 = top2(jnp.where(grp == best, sel, ninf))
    oh1, oh2 = sub == i1, sub == i2
    zero = jnp.zeros_like(scores)
    g1 = jnp.sum(jnp.where(oh1, scores, zero), axis=0, keepdims=True)
    g2 = jnp.sum(jnp.where(oh2, scores, zero), axis=0, keepdims=True)
    den = g1 + g2
    idx_ref[...] = jnp.concatenate([i1, i2], axis=0)
    gate_ref[...] = jnp.concatenate([g1 / den, g2 / den], axis=0)
    both = (oh1 | oh2).astype(BF16)
    before = jnp.dot(both, upper_ref[...], preferred_element_type=F32) + carry_s[...]
    rank1 = jnp.sum(jnp.where(oh1, before, zero), axis=0, keepdims=True)
    rank2 = jnp.sum(jnp.where(oh2, before, zero), axis=0, keepdims=True)
    rank_ref[...] = jnp.concatenate([rank1, rank2], axis=0).astype(I32)
    carry_s[...] = carry_s[...] + jnp.sum(both.astype(F32), axis=1, keepdims=True)
    cnt_ref[...] = carry_s[...]


def _router(h, router_w, router_b):
    n, d = h.shape
    tm = ROUTE_TILE
    upper = (jnp.arange(tm)[:, None] < jnp.arange(tm)[None, :]).astype(BF16)
    tok = pl.BlockSpec((2, tm), lambda i: (0, i))
    return pl.pallas_call(
        _router_body, grid=(n // tm,),
        in_specs=[pl.BlockSpec((tm, d), lambda i: (i, 0)), pl.BlockSpec((N_EXPERTS, d), lambda i: (0, 0)),
                  pl.BlockSpec((N_EXPERTS, 1), lambda i: (0, 0)), pl.BlockSpec((tm, tm), lambda i: (0, 0))],
        out_specs=[tok, tok, tok, pl.BlockSpec((N_EXPERTS, 1), lambda i: (0, 0))],
        out_shape=[jax.ShapeDtypeStruct((2, n), I32), jax.ShapeDtypeStruct((2, n), F32),
                   jax.ShapeDtypeStruct((2, n), I32), jax.ShapeDtypeStruct((N_EXPERTS, 1), F32)],
        scratch_shapes=[pltpu.VMEM((N_EXPERTS, 1), F32)],
        compiler_params=_params("arbitrary"), name="router",
    )(h, router_w.T.astype(F32), router_b.reshape(N_EXPERTS, 1).astype(F32), upper)


def _load_token_tiles(ref, first_row, tokens):
    chunks = [ref[pl.ds(first_row + j, tokens, stride=TOKEN_CHUNKS), :] for j in range(TOKEN_CHUNKS)]
    return jnp.concatenate(chunks, axis=1)


def _store_token_tiles(ref, val):
    tokens = val.shape[0]
    for j in range(TOKEN_CHUNKS):
        ref[pl.ds(j, tokens, stride=TOKEN_CHUNKS), :] = val[:, j * LANES:(j + 1) * LANES].astype(ref.dtype)


def _tile_copy(src, dst, src_tok, dst_tok, sem):
    src_row = pl.multiple_of(src_tok * TOKEN_CHUNKS, TOKEN_CHUNKS)
    dst_row = pl.multiple_of(dst_tok * TOKEN_CHUNKS, TOKEN_CHUNKS)
    return pltpu.make_async_copy(src.at[pl.ds(src_row, TOKEN_CHUNKS), :], dst.at[pl.ds(dst_row, TOKEN_CHUNKS), :], sem)


def _dispatch_body(pad_end_ref, padded_ref, dest_ref, ht_ref, xb_ref, zero_s, sem):
    block_rows = MOE_TILE * TOKEN_CHUNKS
    n_blocks = xb_ref.shape[0] // block_rows

    def block_copy(b):
        return pltpu.make_async_copy(zero_s, xb_ref.at[pl.ds(pl.multiple_of(b * block_rows, block_rows), block_rows), :], sem)

    @pl.when(pl.program_id(0) == 0)
    def _():
        zero_s[...] = jnp.zeros(zero_s.shape, zero_s.dtype)

        def start(e, carry):
            @pl.when(padded_ref[e] > 0)
            def _():
                block_copy(pad_end_ref[e] // MOE_TILE - 1).start()
            return carry

        def wait(e, carry):
            @pl.when(padded_ref[e] > 0)
            def _():
                block_copy(0).wait()
            return carry

        lax.fori_loop(0, N_EXPERTS, start, 0)
        lax.fori_loop(0, N_EXPERTS, wait, 0)
        first_unused = pad_end_ref[N_EXPERTS - 1] // MOE_TILE
        lax.fori_loop(first_unused, n_blocks, lambda b, c: (block_copy(b).start(), c)[1], 0)
        lax.fori_loop(first_unused, n_blocks, lambda b, c: (block_copy(0).wait(), c)[1], 0)

    def start_row(t, carry):
        _tile_copy(ht_ref, xb_ref, t % GATHER_TILE, dest_ref[0, 0, t], sem).start()
        return carry

    def wait_row(t, carry):
        _tile_copy(ht_ref, xb_ref, 0, 0, sem).wait()
        return carry

    lax.fori_loop(0, 2 * GATHER_TILE, start_row, 0, unroll=8)
    lax.fori_loop(0, 2 * GATHER_TILE, wait_row, 0, unroll=8)


def _dispatch(ht, dest_blocks, pad_end, padded, cap):
    n = ht.shape[0] // TOKEN_CHUNKS
    grid_spec = pltpu.PrefetchScalarGridSpec(
        num_scalar_prefetch=2, grid=(n // GATHER_TILE,),
        in_specs=[pl.BlockSpec((1, 1, 2 * GATHER_TILE), lambda i, pe, pd: (i, 0, 0), memory_space=pltpu.SMEM),
                  pl.BlockSpec((GATHER_TILE * TOKEN_CHUNKS, LANES), lambda i, pe, pd: (i, 0))],
        out_specs=pl.BlockSpec(memory_space=pl.ANY),
        scratch_shapes=[pltpu.VMEM((MOE_TILE * TOKEN_CHUNKS, LANES), F32), pltpu.SemaphoreType.DMA(())])
    return pl.pallas_call(
        _dispatch_body, grid_spec=grid_spec, out_shape=jax.ShapeDtypeStruct((cap * TOKEN_CHUNKS, LANES), F32),
        compiler_params=_params("arbitrary"), name="moe_dispatch",
    )(pad_end, padded, dest_blocks, ht)


def _expert_body(blk_ref, exp_ref, used_ref, x_ref, wg_ref, wu_ref, wd_ref, y_ref):
    @pl.when(pl.program_id(0) < used_ref[0])
    def _():
        x = _load_token_tiles(x_ref, 0, MOE_TILE).astype(BF16)
        gate = jnp.dot(x, wg_ref[0], preferred_element_type=F32)
        up = jnp.dot(x, wu_ref[0], preferred_element_type=F32)
        u = gate * _sigmoid(gate) * up
        _store_token_tiles(y_ref, jnp.dot(u.astype(BF16), wd_ref[0], preferred_element_type=F32))

    @pl.when(pl.program_id(0) >= used_ref[0])
    def _():
        y_ref[...] = jnp.zeros(y_ref.shape, y_ref.dtype)


def _experts(xb, blk, blk_expert, n_used, w_gate, w_up, w_down):
    d = D_MODEL
    rows = pl.BlockSpec((MOE_TILE * TOKEN_CHUNKS, LANES), lambda i, blk, ex, used: (blk[i], 0))
    out_rows = pl.BlockSpec((MOE_TILE * TOKEN_CHUNKS, LANES), lambda i, blk, ex, used: (i, 0))
    grid_spec = pltpu.PrefetchScalarGridSpec(
        num_scalar_prefetch=3, grid=(xb.shape[0] // (MOE_TILE * TOKEN_CHUNKS),),
        in_specs=[rows,
                  pl.BlockSpec((1, d, D_EXPERT), lambda i, blk, ex, used: (ex[i], 0, 0)),
                  pl.BlockSpec((1, d, D_EXPERT), lambda i, blk, ex, used: (ex[i], 0, 0)),
                  pl.BlockSpec((1, D_EXPERT, d), lambda i, blk, ex, used: (ex[i], 0, 0))],
        out_specs=out_rows)
    return pl.pallas_call(
        _expert_body, grid_spec=grid_spec, out_shape=jax.ShapeDtypeStruct(xb.shape, F32),
        compiler_params=_params("arbitrary"), name="moe_experts",
    )(blk, blk_expert, n_used, xb, w_gate, w_up, w_down)


def _combine_body(dest_ref, h_ref, gate_ref, lg_ref, lb_ref, yb_ref, o_ref, rows_s, sem):
    def start_row(t, carry):
        _tile_copy(yb_ref, rows_s, dest_ref[0, 0, t], t, sem).start()
        return carry

    def wait_row(t, carry):
        _tile_copy(yb_ref, rows_s, 0, 0, sem).wait()
        return carry

    lax.fori_loop(0, 2 * GATHER_TILE, start_row, 0, unroll=8)
    lax.fori_loop(0, 2 * GATHER_TILE, wait_row, 0, unroll=8)
    gate = gate_ref[...]
    ffn = (gate[:, 0:1] * _load_token_tiles(rows_s, 0, GATHER_TILE)
           + gate[:, 1:2] * _load_token_tiles(rows_s, GATHER_TILE * TOKEN_CHUNKS, GATHER_TILE))
    o_ref[...] = _layer_norm(ALPHA * h_ref[...] + ffn, lg_ref[...], lb_ref[...])


def _combine(h, gate_rows, dest_blocks, yb, ln_g, ln_b):
    n, d = h.shape
    tm = GATHER_TILE
    return pl.pallas_call(
        _combine_body, grid=(n // tm,),
        in_specs=[pl.BlockSpec((1, 1, 2 * tm), lambda i: (i, 0, 0), memory_space=pltpu.SMEM),
                  pl.BlockSpec((tm, d), lambda i: (i, 0)), pl.BlockSpec((tm, 2), lambda i: (i, 0)),
                  pl.BlockSpec((1, d), lambda i: (0, 0)), pl.BlockSpec((1, d), lambda i: (0, 0)),
                  pl.BlockSpec(memory_space=pl.ANY)],
        out_specs=pl.BlockSpec((tm, d), lambda i: (i, 0)), out_shape=jax.ShapeDtypeStruct((n, d), F32),
        scratch_shapes=[pltpu.VMEM((2 * tm * TOKEN_CHUNKS, LANES), F32), pltpu.SemaphoreType.DMA(())],
        compiler_params=_params("arbitrary"), name="moe_combine",
    )(dest_blocks, h, gate_rows, ln_g.reshape(1, d), ln_b.reshape(1, d), yb)


def _moe_layer(h, ht, router_w, router_b, w_gate, w_up, w_down, ln_g, ln_b):
    n, d = h.shape
    idx, gate, rank, cnt = _router(h, router_w, router_b)
    counts = cnt[:, 0].astype(I32)
    padded = (counts + MOE_TILE - 1) // MOE_TILE * MOE_TILE
    pad_end = jnp.cumsum(padded).astype(I32)
    pad_start = pad_end - padded
    experts = jnp.arange(N_EXPERTS, dtype=I32)
    dest = jnp.sum(jnp.where(idx[..., None] == experts, pad_start, 0), axis=-1) + rank
    cap = 2 * n + N_EXPERTS * MOE_TILE
    n_blocks = cap // MOE_TILE
    n_used = pad_end[-1] // MOE_TILE
    blk = jnp.minimum(jnp.arange(n_blocks, dtype=I32), n_used - 1)
    blk_expert = jnp.minimum(jnp.sum(pad_end[None, :] <= (blk * MOE_TILE)[:, None], axis=1), N_EXPERTS - 1).astype(I32)
    tm = GATHER_TILE
    dest_blocks = dest.reshape(2, n // tm, tm).transpose(1, 0, 2).reshape(n // tm, 1, 2 * tm)
    xb = _dispatch(ht, dest_blocks, pad_end, padded, cap)
    yb = _experts(xb, blk, blk_expert, n_used.reshape(1).astype(I32), w_gate.astype(BF16), w_up.astype(BF16),
                  w_down.astype(BF16))
    return _combine(h, gate.T, dest_blocks, yb, ln_g, ln_b)


def kernel(x, ln_g, ln_b, even_w_in, even_w_out, even_lambda, even_subln_g, even_rel_bias, odd_mix, odd_w_rkv, odd_w0, odd_w1, odd_w2, odd_a0, odd_a1, odd_a2, odd_g1, odd_g2, odd_k_k, odd_k_a, odd_r_k, odd_lnx_g, odd_lnx_b, odd_w_out, vres_v0, vres_v1, vres_v2, router_w, router_b, moe_w_gate, moe_w_up, moe_w_down):
    bn, seq, d = x.shape
    x2 = x.reshape(bn * seq, d)
    v_first = None
    rotary_tables = _rotary_tables(seq)
    for layer in range(DEPTH):
        if layer % 2 == 0:
            e = layer // 2
            lam_init = 0.8 - 0.6 * math.exp(-0.3 * layer)
            h, ht = _even_mixer(x2, bn, seq, rotary_tables, even_w_in[e], even_w_out[e], even_lambda[e],
                                even_subln_g[e], even_rel_bias[e], lam_init, ln_g[layer, 0], ln_b[layer, 0])
        else:
            o = layer // 2
            vres = None if o == 0 else (vres_v0[o - 1], vres_v1[o - 1], vres_v2[o - 1])
            h, ht, v_first = _rwkv_mixer(x2, bn, seq, v_first, odd_mix[o], odd_w_rkv[o], odd_w0[o], odd_w1[o], odd_w2[o],
                                     odd_a0[o], odd_a1[o], odd_a2[o], odd_g1[o], odd_g2[o], odd_k_k[o], odd_k_a[o],
                                     odd_r_k[o], odd_lnx_g[o], odd_lnx_b[o], odd_w_out[o], vres,
                                     ln_g[layer, 0], ln_b[layer, 0])
        x2 = _moe_layer(h, ht, router_w, router_b, moe_w_gate[layer], moe_w_up[layer], moe_w_down[layer],
                        ln_g[layer, 1], ln_b[layer, 1])
    return x2.reshape(bn, seq, d)
```

```python
import functools
import math

import jax
import jax.numpy as jnp
from jax import lax
from jax.experimental import pallas as pl
from jax.experimental.pallas import tpu as pltpu

F32 = jnp.float32
BF16 = jnp.bfloat16
I32 = jnp.int32

LANES = 128
SUBLANES = 8
VMEM_LIMIT_BYTES = 56 * 1024 * 1024

D_MODEL = 1024
DEPTH = 4
CHUNK = 64
A_HEADS = 4
A_DIM = 64
A_VDIM = 128
ROPE_DIM = 16
ROPE_THETA = 500000.0
SUBLN_EPS = 1e-5
B_HEADS = 8
B_DIM = 64
B_CHUNKS_BACK = 8
REL_CLIP = 128
BAND = (B_CHUNKS_BACK + 1) * CHUNK
SEG = 512
C_DIM = 64
LNX_EPS = 1e-5 * C_DIM
N_EXPERTS = 32
N_GROUPS = 4
EXPERTS_PER_GROUP = 8
D_EXPERT = 512
LN_EPS = 1e-5
ALPHA = (2 * DEPTH) ** 0.25
NEG_INF = -1e30

ROW_TILE = 256
ATT_TILE = 1024
V_ROWS = A_VDIM + 16
LOG2_E = math.log2(math.e)
Q_SCALE = A_DIM ** -0.5 * LOG2_E
BAND_TILE = B_CHUNKS_BACK * CHUNK
WKV_ROWS = 1024
MOE_TILE = 256
ROUTE_TILE = 512
GATHER_TILE = 256
TOKEN_CHUNKS = D_MODEL // LANES

_NT = (((1,), (1,)), ((), ()))
_TN = (((0,), (0,)), ((), ()))


def _params(*sem):
    return pltpu.CompilerParams(dimension_semantics=sem, vmem_limit_bytes=VMEM_LIMIT_BYTES)


def _dot(a, b):
    return jnp.dot(a.astype(BF16), b.astype(BF16), preferred_element_type=F32)


def _layer_norm(x, g, b):
    mu = jnp.mean(x, axis=-1, keepdims=True)
    xc = x - mu
    var = jnp.mean(xc * xc, axis=-1, keepdims=True)
    return xc * lax.rsqrt(var + LN_EPS) * g + b


def _sigmoid(x):
    return 1.0 / (1.0 + jnp.exp(-x))


def _row_call(body, rows, fulls, outs, tm, name):
    m = rows[0].shape[0] if not isinstance(rows[0], tuple) else None
    in_specs, args = [], []
    for r in rows:
        if isinstance(r, tuple):
            arr, fn, block_rows = r
            in_specs.append(pl.BlockSpec((block_rows, arr.shape[1]), lambda i, fn=fn: (fn(i), 0)))
        else:
            arr = r
            m = arr.shape[0]
            in_specs.append(pl.BlockSpec((tm, arr.shape[1]), lambda i: (i, 0)))
        args.append(arr)
    for f in fulls:
        in_specs.append(pl.BlockSpec(f.shape, lambda i, nd=f.ndim: (0,) * nd))
        args.append(f)
    out_specs = [pl.BlockSpec((tm, w), lambda i: (i, 0)) for w, _ in outs]
    out_shape = [jax.ShapeDtypeStruct((m, w), dt) for w, dt in outs]
    return pl.pallas_call(
        body, grid=(m // tm,), in_specs=in_specs, out_specs=out_specs, out_shape=out_shape,
        compiler_params=_params("parallel"), name=name)(*args)


def _out_call(body, rows, fulls, name):
    n, tm = rows[0].shape[0], ROW_TILE
    in_specs = [pl.BlockSpec((tm, r.shape[1]), lambda i: (i, 0)) for r in rows]
    in_specs += [pl.BlockSpec(f.shape, lambda i, nd=f.ndim: (0,) * nd) for f in fulls]
    return pl.pallas_call(
        body, grid=(n // tm,), in_specs=in_specs,
        out_specs=[pl.BlockSpec((tm, D_MODEL), lambda i: (i, 0)), pl.BlockSpec((tm * TOKEN_CHUNKS, LANES), lambda i: (i, 0))],
        out_shape=[jax.ShapeDtypeStruct((n, D_MODEL), F32), jax.ShapeDtypeStruct((n * TOKEN_CHUNKS, LANES), F32)],
        compiler_params=_params("parallel"), name=name)(*rows, *fulls)


def _even_proj_body(x_ref, c_ref, sm_ref, sp_ref, ct_ref, smt_ref, spt_ref, w_ref, wt_ref,
                    aq_ref, ak_ref, av_ref, bq_ref, bk_ref, bv_ref):
    x = x_ref[...]
    xb = x.astype(BF16)
    half = ROPE_DIM // 2

    def seg(j):
        return jnp.dot(xb, w_ref[:, j * SEG:(j + 1) * SEG], preferred_element_type=F32)

    qv_t = jnp.dot(wt_ref[...], x.T.astype(BF16), preferred_element_type=F32)
    cos, s_minus, s_plus = ct_ref[...], smt_ref[...], spt_ref[...]
    for j in range(SEG // LANES):
        u = qv_t[j * LANES:(j + 1) * LANES, :]
        rot = u * cos + pltpu.roll(u, LANES - half, 0) * s_minus + pltpu.roll(u, half, 0) * s_plus
        aq_ref[j * LANES:(j + 1) * LANES, :] = (rot * Q_SCALE).astype(aq_ref.dtype)
    for v_ref, base in ((av_ref, SEG), (bv_ref, 3 * SEG)):
        for h in range(SEG // LANES):
            v_ref[h * V_ROWS:h * V_ROWS + LANES, :] = qv_t[base + h * LANES:base + (h + 1) * LANES, :].astype(v_ref.dtype)
            v_ref[h * V_ROWS + LANES:(h + 1) * V_ROWS, :] = jnp.ones((V_ROWS - LANES, x.shape[0]), v_ref.dtype)
    bq_ref[...] = (qv_t[2 * SEG:3 * SEG, :] * (B_DIM ** -0.5 * LOG2_E)).astype(bq_ref.dtype)
    cos, s_minus, s_plus = c_ref[...], sm_ref[...], sp_ref[...]
    k = seg(1)
    for j in range(SEG // LANES):
        u = k[:, j * LANES:(j + 1) * LANES]
        rot = u * cos + pltpu.roll(u, LANES - half, 1) * s_minus + pltpu.roll(u, half, 1) * s_plus
        ak_ref[:, j * LANES:(j + 1) * LANES] = rot.astype(ak_ref.dtype)
    bk_ref[...] = seg(4).astype(bk_ref.dtype)


def _rotary_tables(seq):
    half = ROPE_DIM // 2
    inv = ROPE_THETA ** (-jnp.arange(half, dtype=F32) / half)
    ang = jnp.arange(seq, dtype=F32)[:, None] * inv
    cos, sin = jnp.cos(ang), jnp.sin(ang)
    d = jnp.arange(LANES) % A_DIM
    first, second = d < half, (d >= half) & (d < ROPE_DIM)
    col = jnp.where(first, d, d - half) % half
    cos_t = jnp.where(first | second, cos[:, col], 1.0)
    s_minus = jnp.where(first, -sin[:, col], 0.0)
    s_plus = jnp.where(second, sin[:, col], 0.0)
    return cos_t, s_minus, s_plus


def _even_proj(x2, w_in, tabs, seq):
    n, d = x2.shape
    tm = ROW_TILE
    blocks = seq // tm
    w = w_in.astype(BF16)
    seg = lambda j: w[:, j * SEG:(j + 1) * SEG]
    w_qv_t = jnp.concatenate([seg(0), seg(2), seg(3), seg(5)], axis=1).T
    tabs_t = [t.T for t in tabs]
    rows = pl.BlockSpec((tm, SEG), lambda i: (i, 0))
    cols = lambda r: pl.BlockSpec((r, tm), lambda i: (0, i))
    full = lambda a: pl.BlockSpec(a.shape, lambda i: (0, 0))
    v_rows = SEG // LANES * V_ROWS
    shapes = [(SEG, n), (n, SEG), (v_rows, n), (SEG, n), (n, SEG), (v_rows, n)]
    return pl.pallas_call(
        _even_proj_body, grid=(n // tm,),
        in_specs=[pl.BlockSpec((tm, d), lambda i: (i, 0))]
        + [pl.BlockSpec((tm, LANES), lambda i: (i % blocks, 0))] * 3
        + [pl.BlockSpec((LANES, tm), lambda i: (0, i % blocks))] * 3 + [full(w), full(w_qv_t)],
        out_specs=[cols(SEG), rows, cols(v_rows), cols(SEG), rows, cols(v_rows)],
        out_shape=[jax.ShapeDtypeStruct(s, BF16) for s in shapes],
        compiler_params=_params("parallel"), name="even_proj")(x2, *tabs, *tabs_t, w, w_qv_t)


def _diff_attn_body(qi_ref, kj_ref, lam_ref, q_ref, k_ref, v_ref, g_ref, o_ref, q2_s, m_s, l_s, acc_s, *, lam_init):
    p = pl.program_id(2)
    qi, kj = qi_ref[p], kj_ref[p]
    groups = ATT_TILE // LANES

    @pl.when(kj == 0)
    def _():
        feat = lax.broadcasted_iota(I32, (LANES, LANES), 0)
        for j in range(groups):
            q = q_ref[:, j * LANES:(j + 1) * LANES]
            zero = jnp.zeros_like(q)
            q2_s[j, :, 0:LANES] = jnp.where(feat < A_DIM, q, zero)
            q2_s[j, :, LANES:] = jnp.where(feat >= A_DIM, q, zero)
        m_s[...] = jnp.full(m_s.shape, -jnp.inf, F32)
        l_s[...] = jnp.zeros(l_s.shape, F32)
        acc_s[...] = jnp.zeros(acc_s.shape, F32)

    def scores(j):
        return jnp.dot(k_ref[0], q2_s[j], preferred_element_type=F32)

    def step(diagonal):
        def softmax(j, s):
            if diagonal:
                key_chunk = lax.broadcasted_iota(I32, s.shape, 0) // CHUNK
                query_chunk = (lax.broadcasted_iota(I32, s.shape, 1) % LANES + j * LANES) // CHUNK
                s = jnp.where(key_chunk <= query_chunk, s, NEG_INF)
            m = m_s[j]
            m_new = jnp.maximum(m, jnp.max(s, axis=0, keepdims=True))
            m_s[j] = m_new
            return jnp.exp2(m - m_new), jnp.exp2(s - m_new).astype(BF16)

        def accumulate(j, alpha, pr):
            pv = jnp.dot(v_ref[...], pr, preferred_element_type=F32)
            acc_s[j] = alpha * acc_s[j] + pv[0:A_VDIM]
            l_s[j] = alpha * l_s[j] + pv[A_VDIM:A_VDIM + 1]

        s = {0: scores(0), 1: scores(1)}
        probs = {0: softmax(0, s.pop(0))}
        for j in range(groups):
            if j + 2 < groups:
                s[j + 2] = scores(j + 2)
            if j + 1 < groups:
                probs[j + 1] = softmax(j + 1, s.pop(j + 1))
            accumulate(j, *probs.pop(j))

    @pl.when(kj != qi)
    def _():
        step(False)

    @pl.when(kj == qi)
    def _():
        step(True)
        for j in range(groups):
            o = acc_s[j] / l_s[j]
            o = o[:, 0:LANES] - lam_ref[0] * o[:, LANES:]
            od = o.T
            od = od * lax.rsqrt(jnp.mean(od * od, axis=-1, keepdims=True) + SUBLN_EPS) * g_ref[...] * (1.0 - lam_init)
            o_ref[0, j * LANES:(j + 1) * LANES, :] = od.astype(o_ref.dtype)


def _diff_attn(aq_t, ak, av_t, lam, subln_g, lam_init):
    bn, seq, _ = ak.shape
    t = ATT_TILE
    nblk = seq // t
    pairs = [(i, j) for i in range(nblk) for j in range(i + 1)]
    qi = jnp.asarray([p[0] for p in pairs], I32)
    kj = jnp.asarray([p[1] for p in pairs], I32)
    groups, width = t // LANES, 2 * LANES
    grid_spec = pltpu.PrefetchScalarGridSpec(
        num_scalar_prefetch=3, grid=(bn, A_HEADS, len(pairs)),
        in_specs=[
            pl.BlockSpec((LANES, t), lambda b, h, p, qi, kj, lam: (h, b * nblk + qi[p])),
            pl.BlockSpec((1, t, LANES), lambda b, h, p, qi, kj, lam: (b, kj[p], h)),
            pl.BlockSpec((V_ROWS, t), lambda b, h, p, qi, kj, lam: (h, b * nblk + kj[p])),
            pl.BlockSpec((1, LANES), lambda b, h, p, qi, kj, lam: (0, 0)),
        ],
        out_specs=pl.BlockSpec((1, t, LANES), lambda b, h, p, qi, kj, lam: (b, qi[p], h)),
        scratch_shapes=[pltpu.VMEM((groups, LANES, width), BF16), pltpu.VMEM((groups, 1, width), F32),
                        pltpu.VMEM((groups, 1, width), F32), pltpu.VMEM((groups, A_VDIM, width), F32)])
    return pl.pallas_call(
        functools.partial(_diff_attn_body, lam_init=lam_init), grid_spec=grid_spec,
        out_shape=jax.ShapeDtypeStruct((bn, seq, A_HEADS * A_VDIM), BF16),
        compiler_params=_params("parallel", "parallel", "arbitrary"), name="diff_attn",
    )(qi, kj, lam.reshape(1).astype(F32), aq_t, ak, av_t, subln_g.reshape(1, A_VDIM))


def _chunk_attn_body(q_ref, kp_ref, kc_ref, vp_ref, vc_ref, bias_ref, o_ref, kcat, vcat):
    i = pl.program_id(1)
    t = BAND_TILE
    kcat[0:t, :] = kp_ref[0]
    kcat[t:, :] = kc_ref[0]
    vcat[:, 0:t] = vp_ref[...]
    vcat[:, t:] = vc_ref[...]
    pair = 2 * CHUNK
    keys = BAND + CHUNK
    feat = lax.broadcasted_iota(I32, (LANES, pair), 0)
    key = lax.broadcasted_iota(I32, (keys, pair), 0)
    items = [(cp, hp, hh) for cp in range(t // pair) for hp in range(B_HEADS // 2) for hh in range(2)]

    def scores(item):
        cp, hp, hh = item
        q = q_ref[hp * LANES:(hp + 1) * LANES, cp * pair:(cp + 1) * pair]
        q = jnp.where((feat < B_DIM) if hh == 0 else (feat >= B_DIM), q, jnp.zeros_like(q))
        s = jnp.dot(kcat[cp * pair:cp * pair + keys, hp * LANES:(hp + 1) * LANES], q, preferred_element_type=F32)
        s = s + bias_ref[2 * hp + hh]
        return jnp.where((i > 0) | (key + cp * pair >= t), s, NEG_INF)

    s_next = scores(items[0])
    halves = []
    for n, (cp, hp, hh) in enumerate(items):
        s = s_next
        if n + 1 < len(items):
            s_next = scores(items[n + 1])
        pr = jnp.exp2(s - jnp.max(s, axis=0, keepdims=True)).astype(BF16)
        pv = jnp.dot(vcat[hp * V_ROWS:(hp + 1) * V_ROWS, cp * pair:cp * pair + keys], pr, preferred_element_type=F32)
        halves.append(pv[hh * B_DIM:(hh + 1) * B_DIM] / pv[LANES:LANES + 1])
        if hh == 1:
            out = jnp.concatenate(halves, axis=0).T
            o_ref[0, cp * pair:(cp + 1) * pair, hp * LANES:(hp + 1) * LANES] = out.astype(o_ref.dtype)
            halves = []


def _chunk_attn(bq_t, bk, bv_t, rel_bias):
    bn, seq, width = bk.shape
    t = BAND_TILE
    nblk = seq // t
    far = jnp.broadcast_to(rel_bias[:, 2 * REL_CLIP:], (B_HEADS, BAND - REL_CLIP))
    f = jnp.concatenate([far, rel_bias[:, REL_CLIP - CHUNK + 1:2 * REL_CLIP][:, ::-1]], axis=1) * LOG2_E
    bias_t = jnp.stack([f[:, CHUNK - 1 - i:CHUNK - 1 - i + BAND] for i in range(CHUNK)], axis=2)
    out_of_band = jnp.full((B_HEADS, CHUNK, CHUNK), NEG_INF, F32)
    bias2 = jnp.concatenate([jnp.concatenate([bias_t, out_of_band], axis=1),
                             jnp.concatenate([out_of_band, bias_t], axis=1)], axis=2).astype(F32)
    cur = pl.BlockSpec((1, t, width), lambda b, i: (b, i, 0))
    prev = pl.BlockSpec((1, t, width), lambda b, i: (b, jnp.maximum(i - 1, 0), 0))
    v_rows = bv_t.shape[0]
    cur_t = lambda r: pl.BlockSpec((r, t), lambda b, i: (0, b * nblk + i))
    prev_t = lambda r: pl.BlockSpec((r, t), lambda b, i: (0, b * nblk + jnp.maximum(i - 1, 0)))
    return pl.pallas_call(
        _chunk_attn_body, grid=(bn, nblk),
        in_specs=[cur_t(width), prev, cur, prev_t(v_rows), cur_t(v_rows), pl.BlockSpec(bias2.shape, lambda b, i: (0, 0, 0))],
        out_specs=cur, out_shape=jax.ShapeDtypeStruct((bn, seq, width), BF16),
        scratch_shapes=[pltpu.VMEM((2 * t, width), BF16), pltpu.VMEM((v_rows, 2 * t), BF16)],
        compiler_params=_params("parallel", "parallel"), name="chunk_attn",
    )(bq_t, bk, bk, bv_t, bv_t, bias2)


def _even_out_body(x_ref, ya_ref, yb_ref, w_ref, g_ref, b_ref, h_ref, ht_ref):
    mix = jnp.dot(ya_ref[...], w_ref[0:SEG, :], preferred_element_type=F32)
    mix = mix + jnp.dot(yb_ref[...], w_ref[SEG:, :], preferred_element_type=F32)
    h = _layer_norm(ALPHA * x_ref[...] + mix, g_ref[...], b_ref[...])
    h_ref[...] = h
    _store_token_tiles(ht_ref, h)


def _even_mixer(x2, bn, seq, rotary_tables, w_in, w_out, lam_p, subln_g, rel_bias, lam_init, ln_g, ln_b):
    aq_t, ak, av_t, bq_t, bk, bv_t = _even_proj(x2, w_in, rotary_tables, seq)
    lp = lam_p.astype(F32)
    lam = jnp.exp(jnp.sum(lp[0] * lp[1])) - jnp.exp(jnp.sum(lp[2] * lp[3])) + lam_init
    r3 = lambda a: a.reshape(bn, seq, SEG)
    ya = _diff_attn(aq_t, r3(ak), av_t, lam, subln_g, lam_init)
    yb = _chunk_attn(bq_t, r3(bk), bv_t, rel_bias)
    return _out_call(_even_out_body, [x2, ya.reshape(-1, SEG), yb.reshape(-1, SEG)],
                     [w_out.astype(BF16), ln_g.reshape(1, -1), ln_b.reshape(1, -1)], "even_out")


def _rwkv_pre_body(*refs, has_vres, blocks_per_seq):
    if has_vres:
        (x_ref, xp_ref, vf_ref, mix_ref, wr_ref, wk_ref, wv_ref, w0_ref, w1_ref, w2_ref, a0_ref, a1_ref, a2_ref,
         g1_ref, g2_ref, kk_ref, ka_ref, v0_ref, v1_ref, v2_ref, r_o, lw_o, k_o, v_o, kkr_o, a_o, g_o) = refs
    else:
        (x_ref, xp_ref, mix_ref, wr_ref, wk_ref, wv_ref, w0_ref, w1_ref, w2_ref, a0_ref, a1_ref, a2_ref,
         g1_ref, g2_ref, kk_ref, ka_ref, r_o, lw_o, k_o, v_o, kkr_o, a_o, g_o) = refs
    x = x_ref[...]
    first_of_seq = (pl.program_id(0) % blocks_per_seq) == 0
    before = jnp.where(first_of_seq, 0.0, xp_ref[SUBLANES - 1:SUBLANES, :])
    row = lax.broadcasted_iota(I32, x.shape, 0)
    xx = jnp.where(row == 0, before, pltpu.roll(x, 1, 0)) - x
    xr, xw, xk, xv, xa, xg = [x + xx * mix_ref[i:i + 1, :] for i in range(6)]
    r = _dot(xr, wr_ref[...])
    k = _dot(xk, wk_ref[...])
    v = _dot(xv, wv_ref[...])
    z = w0_ref[...] + _dot(jnp.tanh(_dot(xw, w1_ref[...])), w2_ref[...])
    softplus = jnp.maximum(-z, 0.0) + jnp.log(1.0 + jnp.exp(-jnp.abs(z)))
    lw_o[...] = -jnp.exp(-softplus - 0.5)
    if has_vres:
        v = v + (vf_ref[...] - v) * _sigmoid(v0_ref[...] + _dot(_dot(xv, v1_ref[...]), v2_ref[...]))
    a = _sigmoid(a0_ref[...] + _dot(_dot(xa, a1_ref[...]), a2_ref[...]))
    g_o[...] = _dot(_sigmoid(_dot(xg, g1_ref[...])), g2_ref[...])
    r_o[...] = r
    v_o[...] = v
    a_o[...] = a
    kkr_o[...] = k * kk_ref[...]
    k_o[...] = k * (1.0 + (a - 1.0) * ka_ref[...])


def _head_sum(x, first):
    zero = jnp.zeros_like(x)
    s0 = jnp.sum(jnp.where(first, x, zero), axis=1, keepdims=True)
    s1 = jnp.sum(jnp.where(first, zero, x), axis=1, keepdims=True)
    return jnp.where(first, s0, s1)


def _wkv_body(r_ref, lw_ref, k_ref, v_ref, kk_ref, a_ref, rk_ref, lg_ref, lb_ref, y_ref, z_s):
    n = CHUNK

    @pl.when(pl.program_id(2) == 0)
    def _():
        z_s[...] = jnp.zeros(z_s.shape, F32)

    first = lax.broadcasted_iota(I32, (n, LANES), 1) < C_DIM
    row = lax.broadcasted_iota(I32, (2 * n, 2 * n), 0)
    col = lax.broadcasted_iota(I32, (2 * n, 2 * n), 1)
    same = (row // n) == (col // n)
    strict = same & ((col % n) < (row % n))
    incl = same & ((col % n) <= (row % n))
    eye = (row == col).astype(F32)
    tri = (lax.broadcasted_iota(I32, (n, n), 1) <= lax.broadcasted_iota(I32, (n, n), 0)).astype(BF16)

    def stack(x):
        zero = jnp.zeros_like(x)
        return jnp.concatenate([jnp.where(first, x, zero), jnp.where(first, zero, x)], axis=0)

    zero = jnp.zeros((2 * n, 2 * n), F32)
    chunks = range(WKV_ROWS // n)
    prep = []
    for c in chunks:
        rows = slice(c * n, (c + 1) * n)
        r, lw, k, v, kk, a = (ref[0, rows, :] for ref in (r_ref, lw_ref, k_ref, v_ref, kk_ref, a_ref))
        kk = kk / jnp.maximum(jnp.sqrt(_head_sum(kk * kk, first)), 1e-12)
        hi = lw.astype(BF16)
        rem = lw - hi.astype(F32)
        mid = rem.astype(BF16)
        low = (rem - mid.astype(F32)).astype(BF16)
        cum = (jnp.dot(tri, hi, preferred_element_type=F32) + jnp.dot(tri, mid, preferred_element_type=F32)
               + jnp.dot(tri, low, preferred_element_type=F32))
        total = cum[n - 1:n, :]
        g_inv = jnp.exp(-cum)
        g_tail = jnp.exp(total - cum)
        a2 = stack(-kk * jnp.exp(cum - lw))
        r2 = stack(r * jnp.exp(cum))
        b2 = stack(kk * a * g_inv)
        k2 = stack(k * g_inv)
        m = lax.dot_general(jnp.concatenate([a2, r2], axis=0).astype(BF16),
                            jnp.concatenate([b2, k2], axis=0).astype(BF16), _NT, preferred_element_type=F32)
        prep.append(dict(
            a2=a2, r2=r2, v2=stack(v), total=total,
            tails_t=jnp.concatenate([stack(kk * a * g_tail), stack(k * g_tail)], axis=0).T.astype(BF16),
            a_ab=jnp.where(strict, m[0:2 * n, 0:2 * n], zero), a_ak=jnp.where(strict, m[0:2 * n, 2 * n:], zero),
            a_r=jnp.concatenate([jnp.where(incl, m[2 * n:, 0:2 * n], zero),
                                 jnp.where(incl, m[2 * n:, 2 * n:], zero)], axis=1).astype(BF16)))
    inv = [eye + p["a_ab"] for p in prep]
    power = [p["a_ab"].astype(BF16) for p in prep]
    power = [jnp.dot(pw, pw, preferred_element_type=F32).astype(BF16) for pw in power]
    for _ in range(int(math.log2(n)) - 2):
        both = [jnp.dot(jnp.concatenate([pw, iv.astype(BF16)], axis=0), pw, preferred_element_type=F32)
                for pw, iv in zip(power, inv)]
        power = [b[0:2 * n].astype(BF16) for b in both]
        inv = [iv + b[2 * n:] for iv, b in zip(inv, both)]
    inv = [iv + jnp.dot(iv.astype(BF16), pw, preferred_element_type=F32) for iv, pw in zip(inv, power)]
    akv = [_dot(p["a_ak"], p["v2"]) for p in prep]
    wu = [_dot(iv, jnp.concatenate([p["a2"], x], axis=1)) for iv, p, x in zip(inv, prep, akv)]
    rhs = [jnp.concatenate([w, jnp.concatenate([zero, p["v2"]], axis=1)], axis=0).astype(BF16)
           for w, p in zip(wu, prep)]
    ryqs = [jnp.dot(jnp.concatenate([p["a_r"], p["tails_t"]], axis=0), x, preferred_element_type=F32)
            for p, x in zip(prep, rhs)]
    lhs = [jnp.concatenate([p["r2"] + x[0:2 * n, 0:2 * n], x[2 * n:, 0:2 * n]], axis=0).astype(BF16)
           for p, x in zip(prep, ryqs)]
    decay_col = [jnp.sum(eye * jnp.exp(p["total"]), axis=1, keepdims=True) for p in prep]
    z = z_s[...]
    y2 = []
    for c in chunks:
        both = jnp.dot(lhs[c], z.astype(BF16), preferred_element_type=F32)
        y2.append(both[0:2 * n] + ryqs[c][0:2 * n, 2 * n:])
        z = decay_col[c] * z + both[2 * n:] + ryqs[c][2 * n:, 2 * n:]
    z_s[...] = z
    for c in chunks:
        rows = slice(c * n, (c + 1) * n)
        y = y2[c][0:n] + y2[c][n:]
        mu = _head_sum(y, first) * (1.0 / C_DIM)
        yc = y - mu
        var = _head_sum(yc * yc, first) * (1.0 / C_DIM)
        bonus = _head_sum(r_ref[0, rows, :] * k_ref[0, rows, :] * rk_ref[...], first) * v_ref[0, rows, :]
        y_ref[0, rows, :] = yc * lax.rsqrt(var + LNX_EPS) * lg_ref[...] + lb_ref[...] + bonus


def _wkv(r, lw, k, v, kkr, a, r_k, lnx_g, lnx_b):
    bn, seq, d = r.shape
    blk = pl.BlockSpec((1, WKV_ROWS, LANES), lambda b, h, c: (b, c, h))
    par = pl.BlockSpec((1, LANES), lambda b, h, c: (0, h))
    return pl.pallas_call(
        _wkv_body, grid=(bn, d // LANES, seq // WKV_ROWS),
        in_specs=[blk] * 6 + [par] * 3, out_specs=blk,
        out_shape=jax.ShapeDtypeStruct((bn, seq, d), F32),
        scratch_shapes=[pltpu.VMEM((LANES, LANES), F32)],
        compiler_params=_params("parallel", "parallel", "arbitrary"), name="wkv7",
    )(r, lw, k, v, kkr, a, r_k.reshape(1, d), lnx_g.reshape(1, d), lnx_b.reshape(1, d))


def _odd_out_body(x_ref, y_ref, g_ref, w_ref, lg_ref, lb_ref, h_ref, ht_ref):
    mix = _dot(y_ref[...] * g_ref[...], w_ref[...])
    h = _layer_norm(ALPHA * x_ref[...] + mix, lg_ref[...], lb_ref[...])
    h_ref[...] = h
    _store_token_tiles(ht_ref, h)


def _rwkv_mixer(x2, bn, seq, v_first, mix, w_rkv, w0, w1, w2, a0, a1, a2, g1, g2, k_k, k_a, r_k, lnx_g, lnx_b,
                w_out, vres, ln_g, ln_b):
    d = x2.shape[1]
    row = lambda p: p.reshape(1, -1)
    bf = lambda p: p.astype(BF16)
    groups = ROW_TILE // SUBLANES
    rows = [x2, (x2, lambda i: jnp.maximum(i * groups - 1, 0), SUBLANES)] + ([v_first] if vres is not None else [])
    fulls = [mix, bf(w_rkv[0]), bf(w_rkv[1]), bf(w_rkv[2]), row(w0), bf(w1), bf(w2), row(a0), bf(a1), bf(a2),
             bf(g1), bf(g2), row(k_k), row(k_a)]
    if vres is not None:
        fulls += [row(vres[0]), bf(vres[1]), bf(vres[2])]
    body = functools.partial(_rwkv_pre_body, has_vres=vres is not None, blocks_per_seq=seq // ROW_TILE)
    r, lw, k, v, kkr, a, g = _row_call(body, rows, fulls, [(d, F32)] * 7, ROW_TILE, "rwkv_pre")
    if vres is None:
        v_first = v
    r3 = lambda t: t.reshape(bn, seq, d)
    y = _wkv(r3(r), r3(lw), r3(k), r3(v), r3(kkr), r3(a), r_k, lnx_g, lnx_b)
    h, ht = _out_call(_odd_out_body, [x2, y.reshape(-1, d), g], [bf(w_out), row(ln_g), row(ln_b)], "odd_out")
    return h, ht, v_first


def _router_body(h_ref, wt_ref, b_ref, upper_ref, idx_ref, gate_ref, rank_ref, cnt_ref, carry_s):
    @pl.when(pl.program_id(0) == 0)
    def _():
        carry_s[...] = jnp.zeros(carry_s.shape, F32)

    logits = lax.dot_general(wt_ref[...], h_ref[...], _NT, precision=lax.Precision.HIGHEST,
                             preferred_element_type=F32)
    scores = _sigmoid(logits)
    sel = scores + b_ref[...]
    sub = lax.broadcasted_iota(I32, sel.shape, 0)
    grp = sub // EXPERTS_PER_GROUP
    ninf = jnp.full(sel.shape, -jnp.inf, F32)
    big = jnp.full(sel.shape, N_EXPERTS, I32)

    def top2(vals):
        m1 = jnp.max(vals, axis=0, keepdims=True)
        i1 = jnp.min(jnp.where(vals == m1, sub, big), axis=0, keepdims=True)
        rest = jnp.where(sub == i1, ninf, vals)
        m2 = jnp.max(rest, axis=0, keepdims=True)
        i2 = jnp.min(jnp.where(rest == m2, sub, big), axis=0, keepdims=True)
        return m1, i1, m2, i2

    best = jnp.zeros((1, sel.shape[1]), I32)
    best_score = None
    for g in range(N_GROUPS):
        m1, _, m2, _ = top2(jnp.where(grp == g, sel, ninf))
        score = m1 + m2
        if g == 0:
            best_score = score
        else:
            better = score > best_score
            best = jnp.where(better, g, best)
            best_score = jnp.where(better, score, best_score)
    _, i1, _, i2 = top2(jnp.where(grp == best, sel, ninf))
    oh1, oh2 = sub == i1, sub == i2
    zero = jnp.zeros_like(scores)
    g1 = jnp.sum(jnp.where(oh1, scores, zero), axis=0, keepdims=True)
    g2 = jnp.sum(jnp.where(oh2, scores, zero), axis=0, keepdims=True)
    den = g1 + g2
    idx_ref[...] = jnp.concatenate([i1, i2], axis=0)
    gate_ref[...] = jnp.concatenate([g1 / den, g2 / den], axis=0)
    both = (oh1 | oh2).astype(BF16)
    before = jnp.dot(both, upper_ref[...], preferred_element_type=F32) + carry_s[...]
    rank1 = jnp.sum(jnp.where(oh1, before, zero), axis=0, keepdims=True)
    rank2 = jnp.sum(jnp.where(oh2, before, zero), axis=0, keepdims=True)
    rank_ref[...] = jnp.concatenate([rank1, rank2], axis=0).astype(I32)
    carry_s[...] = carry_s[...] + jnp.sum(both.astype(F32), axis=1, keepdims=True)
    cnt_ref[...] = carry_s[...]


def _router(h, router_w, router_b):
    n, d = h.shape
    tm = ROUTE_TILE
    upper = (jnp.arange(tm)[:, None] < jnp.arange(tm)[None, :]).astype(BF16)
    tok = pl.BlockSpec((2, tm), lambda i: (0, i))
    return pl.pallas_call(
        _router_body, grid=(n // tm,),
        in_specs=[pl.BlockSpec((tm, d), lambda i: (i, 0)), pl.BlockSpec((N_EXPERTS, d), lambda i: (0, 0)),
                  pl.BlockSpec((N_EXPERTS, 1), lambda i: (0, 0)), pl.BlockSpec((tm, tm), lambda i: (0, 0))],
        out_specs=[tok, tok, tok, pl.BlockSpec((N_EXPERTS, 1), lambda i: (0, 0))],
        out_shape=[jax.ShapeDtypeStruct((2, n), I32), jax.ShapeDtypeStruct((2, n), F32),
                   jax.ShapeDtypeStruct((2, n), I32), jax.ShapeDtypeStruct((N_EXPERTS, 1), F32)],
        scratch_shapes=[pltpu.VMEM((N_EXPERTS, 1), F32)],
        compiler_params=_params("arbitrary"), name="router",
    )(h, router_w.T.astype(F32), router_b.reshape(N_EXPERTS, 1).astype(F32), upper)


def _load_token_tiles(ref, first_row, tokens):
    chunks = [ref[pl.ds(first_row + j, tokens, stride=TOKEN_CHUNKS), :] for j in range(TOKEN_CHUNKS)]
    return jnp.concatenate(chunks, axis=1)


def _store_token_tiles(ref, val):
    tokens = val.shape[0]
    for j in range(TOKEN_CHUNKS):
        ref[pl.ds(j, tokens, stride=TOKEN_CHUNKS), :] = val[:, j * LANES:(j + 1) * LANES].astype(ref.dtype)


def _tile_copy(src, dst, src_tok, dst_tok, sem):
    src_row = pl.multiple_of(src_tok * TOKEN_CHUNKS, TOKEN_CHUNKS)
    dst_row = pl.multiple_of(dst_tok * TOKEN_CHUNKS, TOKEN_CHUNKS)
    return pltpu.make_async_copy(src.at[pl.ds(src_row, TOKEN_CHUNKS), :], dst.at[pl.ds(dst_row, TOKEN_CHUNKS), :], sem)


def _dispatch_body(pad_end_ref, padded_ref, dest_ref, ht_ref, xb_ref, zero_s, sem):
    block_rows = MOE_TILE * TOKEN_CHUNKS
    n_blocks = xb_ref.shape[0] // block_rows

    def block_copy(b):
        return pltpu.make_async_copy(zero_s, xb_ref.at[pl.ds(pl.multiple_of(b * block_rows, block_rows), block_rows), :], sem)

    @pl.when(pl.program_id(0) == 0)
    def _():
        zero_s[...] = jnp.zeros(zero_s.shape, zero_s.dtype)

        def start(e, carry):
            @pl.when(padded_ref[e] > 0)
            def _():
                block_copy(pad_end_ref[e] // MOE_TILE - 1).start()
            return carry

        def wait(e, carry):
            @pl.when(padded_ref[e] > 0)
            def _():
                block_copy(0).wait()
            return carry

        lax.fori_loop(0, N_EXPERTS, start, 0)
        lax.fori_loop(0, N_EXPERTS, wait, 0)
        first_unused = pad_end_ref[N_EXPERTS - 1] // MOE_TILE
        lax.fori_loop(first_unused, n_blocks, lambda b, c: (block_copy(b).start(), c)[1], 0)
        lax.fori_loop(first_unused, n_blocks, lambda b, c: (block_copy(0).wait(), c)[1], 0)

    def start_row(t, carry):
        _tile_copy(ht_ref, xb_ref, t % GATHER_TILE, dest_ref[0, 0, t], sem).start()
        return carry

    def wait_row(t, carry):
        _tile_copy(ht_ref, xb_ref, 0, 0, sem).wait()
        return carry

    lax.fori_loop(0, 2 * GATHER_TILE, start_row, 0, unroll=8)
    lax.fori_loop(0, 2 * GATHER_TILE, wait_row, 0, unroll=8)


def _dispatch(ht, dest_blocks, pad_end, padded, cap):
    n = ht.shape[0] // TOKEN_CHUNKS
    grid_spec = pltpu.PrefetchScalarGridSpec(
        num_scalar_prefetch=2, grid=(n // GATHER_TILE,),
        in_specs=[pl.BlockSpec((1, 1, 2 * GATHER_TILE), lambda i, pe, pd: (i, 0, 0), memory_space=pltpu.SMEM),
                  pl.BlockSpec((GATHER_TILE * TOKEN_CHUNKS, LANES), lambda i, pe, pd: (i, 0))],
        out_specs=pl.BlockSpec(memory_space=pl.ANY),
        scratch_shapes=[pltpu.VMEM((MOE_TILE * TOKEN_CHUNKS, LANES), F32), pltpu.SemaphoreType.DMA(())])
    return pl.pallas_call(
        _dispatch_body, grid_spec=grid_spec, out_shape=jax.ShapeDtypeStruct((cap * TOKEN_CHUNKS, LANES), F32),
        compiler_params=_params("arbitrary"), name="moe_dispatch",
    )(pad_end, padded, dest_blocks, ht)


def _expert_body(blk_ref, exp_ref, used_ref, x_ref, wg_ref, wu_ref, wd_ref, y_ref, wg_s, wu_s, wd_s):
    i = pl.program_id(0)

    @pl.when((i == 0) | (exp_ref[i] != exp_ref[jnp.maximum(i - 1, 0)]))
    def _():
        wg_s[...] = wg_ref[0].astype(BF16)
        wu_s[...] = wu_ref[0].astype(BF16)
        wd_s[...] = wd_ref[0].astype(BF16)

    @pl.when(i < used_ref[0])
    def _():
        x = _load_token_tiles(x_ref, 0, MOE_TILE).astype(BF16)
        gate = jnp.dot(x, wg_s[...], preferred_element_type=F32)
        up = jnp.dot(x, wu_s[...], preferred_element_type=F32)
        u = gate * _sigmoid(gate) * up
        _store_token_tiles(y_ref, jnp.dot(u.astype(BF16), wd_s[...], preferred_element_type=F32))

    @pl.when(i >= used_ref[0])
    def _():
        y_ref[...] = jnp.zeros(y_ref.shape, y_ref.dtype)


def _experts(xb, blk, blk_expert, n_used, layer, w_gate, w_up, w_down):
    d = D_MODEL
    rows = pl.BlockSpec((MOE_TILE * TOKEN_CHUNKS, LANES), lambda i, blk, ex, used: (blk[i], 0))
    out_rows = pl.BlockSpec((MOE_TILE * TOKEN_CHUNKS, LANES), lambda i, blk, ex, used: (i, 0))
    grid_spec = pltpu.PrefetchScalarGridSpec(
        num_scalar_prefetch=3, grid=(xb.shape[0] // (MOE_TILE * TOKEN_CHUNKS),),
        in_specs=[rows,
                  pl.BlockSpec((None, 1, d, D_EXPERT), lambda i, blk, ex, used: (layer, ex[i], 0, 0)),
                  pl.BlockSpec((None, 1, d, D_EXPERT), lambda i, blk, ex, used: (layer, ex[i], 0, 0)),
                  pl.BlockSpec((None, 1, D_EXPERT, d), lambda i, blk, ex, used: (layer, ex[i], 0, 0))],
        out_specs=out_rows,
        scratch_shapes=[pltpu.VMEM((d, D_EXPERT), BF16), pltpu.VMEM((d, D_EXPERT), BF16),
                        pltpu.VMEM((D_EXPERT, d), BF16)])
    return pl.pallas_call(
        _expert_body, grid_spec=grid_spec, out_shape=jax.ShapeDtypeStruct(xb.shape, F32),
        compiler_params=_params("arbitrary"), name="moe_experts",
    )(blk, blk_expert, n_used, xb, w_gate, w_up, w_down)


def _combine_body(dest_ref, h_ref, gate_ref, lg_ref, lb_ref, yb_ref, o_ref, rows_s, sem):
    def start_row(t, carry):
        _tile_copy(yb_ref, rows_s, dest_ref[0, 0, t], t, sem).start()
        return carry

    def wait_row(t, carry):
        _tile_copy(yb_ref, rows_s, 0, 0, sem).wait()
        return carry

    lax.fori_loop(0, 2 * GATHER_TILE, start_row, 0, unroll=8)
    lax.fori_loop(0, 2 * GATHER_TILE, wait_row, 0, unroll=8)
    gate = gate_ref[...]
    ffn = (gate[:, 0:1] * _load_token_tiles(rows_s, 0, GATHER_TILE)
           + gate[:, 1:2] * _load_token_tiles(rows_s, GATHER_TILE * TOKEN_CHUNKS, GATHER_TILE))
    o_ref[...] = _layer_norm(ALPHA * h_ref[...] + ffn, lg_ref[...], lb_ref[...])


def _combine(h, gate_rows, dest_blocks, yb, ln_g, ln_b):
    n, d = h.shape
    tm = GATHER_TILE
    return pl.pallas_call(
        _combine_body, grid=(n // tm,),
        in_specs=[pl.BlockSpec((1, 1, 2 * tm), lambda i: (i, 0, 0), memory_space=pltpu.SMEM),
                  pl.BlockSpec((tm, d), lambda i: (i, 0)), pl.BlockSpec((tm, 2), lambda i: (i, 0)),
                  pl.BlockSpec((1, d), lambda i: (0, 0)), pl.BlockSpec((1, d), lambda i: (0, 0)),
                  pl.BlockSpec(memory_space=pl.ANY)],
        out_specs=pl.BlockSpec((tm, d), lambda i: (i, 0)), out_shape=jax.ShapeDtypeStruct((n, d), F32),
        scratch_shapes=[pltpu.VMEM((2 * tm * TOKEN_CHUNKS, LANES), F32), pltpu.SemaphoreType.DMA(())],
        compiler_params=_params("arbitrary"), name="moe_combine",
    )(dest_blocks, h, gate_rows, ln_g.reshape(1, d), ln_b.reshape(1, d), yb)


def _moe_layer(h, ht, router_w, router_b, layer, w_gate, w_up, w_down, ln_g, ln_b):
    n, d = h.shape
    idx, gate, rank, cnt = _router(h, router_w, router_b)
    counts = cnt[:, 0].astype(I32)
    padded = (counts + MOE_TILE - 1) // MOE_TILE * MOE_TILE
    pad_end = jnp.cumsum(padded).astype(I32)
    pad_start = pad_end - padded
    experts = jnp.arange(N_EXPERTS, dtype=I32)
    dest = jnp.sum(jnp.where(idx[..., None] == experts, pad_start, 0), axis=-1) + rank
    cap = 2 * n + N_EXPERTS * MOE_TILE
    n_blocks = cap // MOE_TILE
    n_used = pad_end[-1] // MOE_TILE
    blk = jnp.minimum(jnp.arange(n_blocks, dtype=I32), n_used - 1)
    blk_expert = jnp.minimum(jnp.sum(pad_end[None, :] <= (blk * MOE_TILE)[:, None], axis=1), N_EXPERTS - 1).astype(I32)
    tm = GATHER_TILE
    dest_blocks = dest.reshape(2, n // tm, tm).transpose(1, 0, 2).reshape(n // tm, 1, 2 * tm)
    xb = _dispatch(ht, dest_blocks, pad_end, padded, cap)
    yb = _experts(xb, blk, blk_expert, n_used.reshape(1).astype(I32), layer, w_gate, w_up, w_down)
    return _combine(h, gate.T, dest_blocks, yb, ln_g, ln_b)


def kernel(x, ln_g, ln_b, even_w_in, even_w_out, even_lambda, even_subln_g, even_rel_bias, odd_mix, odd_w_rkv, odd_w0, odd_w1, odd_w2, odd_a0, odd_a1, odd_a2, odd_g1, odd_g2, odd_k_k, odd_k_a, odd_r_k, odd_lnx_g, odd_lnx_b, odd_w_out, vres_v0, vres_v1, vres_v2, router_w, router_b, moe_w_gate, moe_w_up, moe_w_down):
    bn, seq, d = x.shape
    x2 = x.reshape(bn * seq, d)
    v_first = None
    rotary_tables = _rotary_tables(seq)
    for layer in range(DEPTH):
        if layer % 2 == 0:
            e = layer // 2
            lam_init = 0.8 - 0.6 * math.exp(-0.3 * layer)
            h, ht = _even_mixer(x2, bn, seq, rotary_tables, even_w_in[e], even_w_out[e], even_lambda[e],
                                even_subln_g[e], even_rel_bias[e], lam_init, ln_g[layer, 0], ln_b[layer, 0])
        else:
            o = layer // 2
            vres = None if o == 0 else (vres_v0[o - 1], vres_v1[o - 1], vres_v2[o - 1])
            h, ht, v_first = _rwkv_mixer(x2, bn, seq, v_first, odd_mix[o], odd_w_rkv[o], odd_w0[o], odd_w1[o], odd_w2[o],
                                     odd_a0[o], odd_a1[o], odd_a2[o], odd_g1[o], odd_g2[o], odd_k_k[o], odd_k_a[o],
                                     odd_r_k[o], odd_lnx_g[o], odd_lnx_b[o], odd_w_out[o], vres,
                                     ln_g[layer, 0], ln_b[layer, 0])
        x2 = _moe_layer(h, ht, router_w, router_b, layer, moe_w_gate, moe_w_up, moe_w_down,
                        ln_g[layer, 1], ln_b[layer, 1])
    return x2.reshape(bn, seq, d)
```

```python
import functools
import math

import jax
import jax.numpy as jnp
from jax import lax
from jax.experimental import pallas as pl
from jax.experimental.pallas import tpu as pltpu

F32 = jnp.float32
BF16 = jnp.bfloat16
I32 = jnp.int32

LANES = 128
SUBLANES = 8
VMEM_LIMIT_BYTES = 56 * 1024 * 1024

D_MODEL = 1024
DEPTH = 4
CHUNK = 64
A_HEADS = 4
A_DIM = 64
A_VDIM = 128
ROPE_DIM = 16
ROPE_THETA = 500000.0
SUBLN_EPS = 1e-5
B_HEADS = 8
B_DIM = 64
B_CHUNKS_BACK = 8
REL_CLIP = 128
BAND = (B_CHUNKS_BACK + 1) * CHUNK
SEG = 512
C_DIM = 64
LNX_EPS = 1e-5 * C_DIM
N_EXPERTS = 32
N_GROUPS = 4
EXPERTS_PER_GROUP = 8
D_EXPERT = 512
LN_EPS = 1e-5
ALPHA = (2 * DEPTH) ** 0.25
NEG_INF = -1e30

ROW_TILE = 256
ATT_TILE = 1024
ATT_KEYS = 256
V_ROWS = A_VDIM + 16
LOG2_E = math.log2(math.e)
Q_SCALE = A_DIM ** -0.5 * LOG2_E
BAND_TILE = B_CHUNKS_BACK * CHUNK
WKV_ROWS = 512
MOE_TILE = 512
ROUTE_TILE = 512
GATHER_TILE = 256
TOKEN_CHUNKS = D_MODEL // LANES

_NT = (((1,), (1,)), ((), ()))
_TN = (((0,), (0,)), ((), ()))


def _params(*sem):
    return pltpu.CompilerParams(dimension_semantics=sem, vmem_limit_bytes=VMEM_LIMIT_BYTES)


def _dot(a, b):
    return jnp.dot(a.astype(BF16), b.astype(BF16), preferred_element_type=F32)


def _layer_norm(x, g, b):
    mu = jnp.mean(x, axis=-1, keepdims=True)
    xc = x - mu
    var = jnp.mean(xc * xc, axis=-1, keepdims=True)
    return xc * lax.rsqrt(var + LN_EPS) * g + b


def _sigmoid(x):
    return 1.0 / (1.0 + jnp.exp(-x))


def _row_call(body, rows, fulls, outs, tm, name):
    m = rows[0].shape[0] if not isinstance(rows[0], tuple) else None
    in_specs, args = [], []
    for r in rows:
        if isinstance(r, tuple):
            arr, fn, block_rows = r
            in_specs.append(pl.BlockSpec((block_rows, arr.shape[1]), lambda i, fn=fn: (fn(i), 0)))
        else:
            arr = r
            m = arr.shape[0]
            in_specs.append(pl.BlockSpec((tm, arr.shape[1]), lambda i: (i, 0)))
        args.append(arr)
    for f in fulls:
        in_specs.append(pl.BlockSpec(f.shape, lambda i, nd=f.ndim: (0,) * nd))
        args.append(f)
    out_specs = [pl.BlockSpec((tm, w), lambda i: (i, 0)) for w, _ in outs]
    out_shape = [jax.ShapeDtypeStruct((m, w), dt) for w, dt in outs]
    return pl.pallas_call(
        body, grid=(m // tm,), in_specs=in_specs, out_specs=out_specs, out_shape=out_shape,
        compiler_params=_params("parallel"), name=name)(*args)


def _out_call(body, rows, fulls, name):
    n, tm = rows[0].shape[0], ROW_TILE
    in_specs = [pl.BlockSpec((tm, r.shape[1]), lambda i: (i, 0)) for r in rows]
    in_specs += [pl.BlockSpec(f.shape, lambda i, nd=f.ndim: (0,) * nd) for f in fulls]
    return pl.pallas_call(
        body, grid=(n // tm,), in_specs=in_specs,
        out_specs=[pl.BlockSpec((tm, D_MODEL), lambda i: (i, 0)), pl.BlockSpec((tm * TOKEN_CHUNKS, LANES), lambda i: (i, 0))],
        out_shape=[jax.ShapeDtypeStruct((n, D_MODEL), F32), jax.ShapeDtypeStruct((n * TOKEN_CHUNKS, LANES), F32)],
        compiler_params=_params("parallel"), name=name)(*rows, *fulls)


def _even_proj_body(x_ref, c_ref, sm_ref, sp_ref, ct_ref, smt_ref, spt_ref, w_ref, wt_ref,
                    aq_ref, ak_ref, av_ref, bq_ref, bk_ref, bv_ref):
    x = x_ref[...]
    xb = x.astype(BF16)
    half = ROPE_DIM // 2

    def seg(j):
        return jnp.dot(xb, w_ref[:, j * SEG:(j + 1) * SEG], preferred_element_type=F32)

    qv_t = jnp.dot(wt_ref[...], x.T.astype(BF16), preferred_element_type=F32)
    cos, s_minus, s_plus = ct_ref[...], smt_ref[...], spt_ref[...]
    for j in range(SEG // LANES):
        u = qv_t[j * LANES:(j + 1) * LANES, :]
        rot = u * cos + pltpu.roll(u, LANES - half, 0) * s_minus + pltpu.roll(u, half, 0) * s_plus
        aq_ref[j * LANES:(j + 1) * LANES, :] = (rot * Q_SCALE).astype(aq_ref.dtype)
    for v_ref, base in ((av_ref, SEG), (bv_ref, 3 * SEG)):
        for h in range(SEG // LANES):
            v_ref[h * V_ROWS:h * V_ROWS + LANES, :] = qv_t[base + h * LANES:base + (h + 1) * LANES, :].astype(v_ref.dtype)
            v_ref[h * V_ROWS + LANES:(h + 1) * V_ROWS, :] = jnp.ones((V_ROWS - LANES, x.shape[0]), v_ref.dtype)
    bq_ref[...] = (qv_t[2 * SEG:3 * SEG, :] * (B_DIM ** -0.5 * LOG2_E)).astype(bq_ref.dtype)
    cos, s_minus, s_plus = c_ref[...], sm_ref[...], sp_ref[...]
    k = seg(1)
    for j in range(SEG // LANES):
        u = k[:, j * LANES:(j + 1) * LANES]
        rot = u * cos + pltpu.roll(u, LANES - half, 1) * s_minus + pltpu.roll(u, half, 1) * s_plus
        ak_ref[:, j * LANES:(j + 1) * LANES] = rot.astype(ak_ref.dtype)
    bk_ref[...] = seg(4).astype(bk_ref.dtype)


def _rotary_tables(seq):
    half = ROPE_DIM // 2
    inv = ROPE_THETA ** (-jnp.arange(half, dtype=F32) / half)
    ang = jnp.arange(seq, dtype=F32)[:, None] * inv
    cos, sin = jnp.cos(ang), jnp.sin(ang)
    d = jnp.arange(LANES) % A_DIM
    first, second = d < half, (d >= half) & (d < ROPE_DIM)
    col = jnp.where(first, d, d - half) % half
    cos_t = jnp.where(first | second, cos[:, col], 1.0)
    s_minus = jnp.where(first, -sin[:, col], 0.0)
    s_plus = jnp.where(second, sin[:, col], 0.0)
    return cos_t, s_minus, s_plus


def _even_proj(x2, w_in, tabs, seq):
    n, d = x2.shape
    tm = ROW_TILE
    blocks = seq // tm
    w = w_in.astype(BF16)
    seg = lambda j: w[:, j * SEG:(j + 1) * SEG]
    w_qv_t = jnp.concatenate([seg(0), seg(2), seg(3), seg(5)], axis=1).T
    tabs_t = [t.T for t in tabs]
    rows = pl.BlockSpec((tm, SEG), lambda i: (i, 0))
    cols = lambda r: pl.BlockSpec((r, tm), lambda i: (0, i))
    full = lambda a: pl.BlockSpec(a.shape, lambda i: (0, 0))
    v_rows = SEG // LANES * V_ROWS
    shapes = [(SEG, n), (n, SEG), (v_rows, n), (SEG, n), (n, SEG), (v_rows, n)]
    return pl.pallas_call(
        _even_proj_body, grid=(n // tm,),
        in_specs=[pl.BlockSpec((tm, d), lambda i: (i, 0))]
        + [pl.BlockSpec((tm, LANES), lambda i: (i % blocks, 0))] * 3
        + [pl.BlockSpec((LANES, tm), lambda i: (0, i % blocks))] * 3 + [full(w), full(w_qv_t)],
        out_specs=[cols(SEG), rows, cols(v_rows), cols(SEG), rows, cols(v_rows)],
        out_shape=[jax.ShapeDtypeStruct(s, BF16) for s in shapes],
        compiler_params=_params("parallel"), name="even_proj")(x2, *tabs, *tabs_t, w, w_qv_t)


def _diff_attn_body(qi_ref, kj_ref, lam_ref, q_ref, k_ref, v_ref, g_ref, o_ref, q2_s, m_s, l_s, acc_s, *, lam_init):
    p = pl.program_id(2)
    qi, kj = qi_ref[p], kj_ref[p]
    groups = ATT_TILE // LANES

    @pl.when(kj == 0)
    def _():
        feat = lax.broadcasted_iota(I32, (LANES, LANES), 0)
        for j in range(groups):
            q = q_ref[:, j * LANES:(j + 1) * LANES]
            zero = jnp.zeros_like(q)
            q2_s[j, :, 0:LANES] = jnp.where(feat < A_DIM, q, zero)
            q2_s[j, :, LANES:] = jnp.where(feat >= A_DIM, q, zero)
        m_s[...] = jnp.full(m_s.shape, -jnp.inf, F32)
        l_s[...] = jnp.zeros(l_s.shape, F32)
        acc_s[...] = jnp.zeros(acc_s.shape, F32)

    def step(diagonal):
        pieces = range(ATT_TILE // ATT_KEYS)

        def scores(j, i):
            s = jnp.dot(k_ref[0, i * ATT_KEYS:(i + 1) * ATT_KEYS, :], q2_s[j], preferred_element_type=F32)
            if diagonal:
                key_chunk = (lax.broadcasted_iota(I32, s.shape, 0) + i * ATT_KEYS) // CHUNK
                query_chunk = (lax.broadcasted_iota(I32, s.shape, 1) % LANES + j * LANES) // CHUNK
                s = jnp.where(key_chunk <= query_chunk, s, NEG_INF)
            return s

        def stats(j, s):
            m = m_s[j]
            m_new = m
            for piece in s:
                m_new = jnp.maximum(m_new, jnp.max(piece, axis=0, keepdims=True))
            m_s[j] = m_new
            return jnp.exp2(m - m_new), m_new

        def values(pr, i):
            return jnp.dot(v_ref[:, i * ATT_KEYS:(i + 1) * ATT_KEYS], pr[i], preferred_element_type=F32)

        def accumulate(j, alpha, pv):
            pv = functools.reduce(lambda a, b: a + b, pv)
            acc_s[j] = alpha * acc_s[j] + pv[0:A_VDIM]
            l_s[j] = alpha * l_s[j] + pv[A_VDIM:A_VDIM + 1]

        s = [scores(0, i) for i in pieces]
        alpha, m_new = stats(0, s)
        pending = None
        for j in range(groups):
            s_next, pr, pv = [], [], []
            for i in pieces:
                if j + 1 < groups:
                    s_next.append(scores(j + 1, i))
                pr.append(jnp.exp2(s[i] - m_new).astype(BF16))
                if pending is not None:
                    pv.append(values(pending[2], i))
            if pending is not None:
                accumulate(pending[0], pending[1], pv)
            pending = (j, alpha, pr)
            if j + 1 < groups:
                s = s_next
                alpha, m_new = stats(j + 1, s)
        accumulate(pending[0], pending[1], [values(pending[2], i) for i in pieces])

    @pl.when(kj != qi)
    def _():
        step(False)

    @pl.when(kj == qi)
    def _():
        step(True)
        for j in range(groups):
            o = acc_s[j] / l_s[j]
            o = o[:, 0:LANES] - lam_ref[0] * o[:, LANES:]
            od = o.T
            od = od * lax.rsqrt(jnp.mean(od * od, axis=-1, keepdims=True) + SUBLN_EPS) * g_ref[...] * (1.0 - lam_init)
            o_ref[0, j * LANES:(j + 1) * LANES, :] = od.astype(o_ref.dtype)


def _diff_attn(aq_t, ak, av_t, lam, subln_g, lam_init):
    bn, seq, _ = ak.shape
    t = ATT_TILE
    nblk = seq // t
    pairs = [(i, j) for i in range(nblk) for j in range(i + 1)]
    qi = jnp.asarray([p[0] for p in pairs], I32)
    kj = jnp.asarray([p[1] for p in pairs], I32)
    groups, width = t // LANES, 2 * LANES
    grid_spec = pltpu.PrefetchScalarGridSpec(
        num_scalar_prefetch=3, grid=(bn, A_HEADS, len(pairs)),
        in_specs=[
            pl.BlockSpec((LANES, t), lambda b, h, p, qi, kj, lam: (h, b * nblk + qi[p])),
            pl.BlockSpec((1, t, LANES), lambda b, h, p, qi, kj, lam: (b, kj[p], h)),
            pl.BlockSpec((V_ROWS, t), lambda b, h, p, qi, kj, lam: (h, b * nblk + kj[p])),
            pl.BlockSpec((1, LANES), lambda b, h, p, qi, kj, lam: (0, 0)),
        ],
        out_specs=pl.BlockSpec((1, t, LANES), lambda b, h, p, qi, kj, lam: (b, qi[p], h)),
        scratch_shapes=[pltpu.VMEM((groups, LANES, width), BF16), pltpu.VMEM((groups, 1, width), F32),
                        pltpu.VMEM((groups, 1, width), F32), pltpu.VMEM((groups, A_VDIM, width), F32)])
    return pl.pallas_call(
        functools.partial(_diff_attn_body, lam_init=lam_init), grid_spec=grid_spec,
        out_shape=jax.ShapeDtypeStruct((bn, seq, A_HEADS * A_VDIM), BF16),
        compiler_params=_params("parallel", "parallel", "arbitrary"), name="diff_attn",
    )(qi, kj, lam.reshape(1).astype(F32), aq_t, ak, av_t, subln_g.reshape(1, A_VDIM))


def _chunk_attn_body(q_ref, kp_ref, kc_ref, vp_ref, vc_ref, bias_ref, o_ref, kcat, vcat):
    i = pl.program_id(1)
    t = BAND_TILE
    kcat[0:t, :] = kp_ref[0]
    kcat[t:, :] = kc_ref[0]
    vcat[:, 0:t] = vp_ref[...]
    vcat[:, t:] = vc_ref[...]
    pair = 2 * CHUNK
    keys = BAND + CHUNK
    feat = lax.broadcasted_iota(I32, (LANES, pair), 0)
    key = lax.broadcasted_iota(I32, (keys, pair), 0)
    items = [(cp, hp, hh) for cp in range(t // pair) for hp in range(B_HEADS // 2) for hh in range(2)]

    def scores(item):
        cp, hp, hh = item
        q = q_ref[hp * LANES:(hp + 1) * LANES, cp * pair:(cp + 1) * pair]
        q = jnp.where((feat < B_DIM) if hh == 0 else (feat >= B_DIM), q, jnp.zeros_like(q))
        s = jnp.dot(kcat[cp * pair:cp * pair + keys, hp * LANES:(hp + 1) * LANES], q, preferred_element_type=F32)
        s = s + bias_ref[2 * hp + hh]
        return jnp.where((i > 0) | (key + cp * pair >= t), s, NEG_INF)

    s_next = scores(items[0])
    halves = []
    for n, (cp, hp, hh) in enumerate(items):
        s = s_next
        if n + 1 < len(items):
            s_next = scores(items[n + 1])
        pr = jnp.exp2(s - jnp.max(s, axis=0, keepdims=True)).astype(BF16)
        pv = jnp.dot(vcat[hp * V_ROWS:(hp + 1) * V_ROWS, cp * pair:cp * pair + keys], pr, preferred_element_type=F32)
        halves.append(pv[hh * B_DIM:(hh + 1) * B_DIM] / pv[LANES:LANES + 1])
        if hh == 1:
            out = jnp.concatenate(halves, axis=0).T
            o_ref[0, cp * pair:(cp + 1) * pair, hp * LANES:(hp + 1) * LANES] = out.astype(o_ref.dtype)
            halves = []


def _chunk_attn(bq_t, bk, bv_t, rel_bias):
    bn, seq, width = bk.shape
    t = BAND_TILE
    nblk = seq // t
    far = jnp.broadcast_to(rel_bias[:, 2 * REL_CLIP:], (B_HEADS, BAND - REL_CLIP))
    f = jnp.concatenate([far, rel_bias[:, REL_CLIP - CHUNK + 1:2 * REL_CLIP][:, ::-1]], axis=1) * LOG2_E
    span = f.shape[1]
    flat = jnp.broadcast_to(f[:, None, :], (B_HEADS, CHUNK, span)).reshape(B_HEADS, CHUNK * span)
    skew = flat[:, CHUNK - 1:CHUNK - 1 + CHUNK * (span - 1)].reshape(B_HEADS, CHUNK, span - 1)
    bias_t = skew[:, :, 0:BAND].transpose(0, 2, 1)
    out_of_band = jnp.full((B_HEADS, CHUNK, CHUNK), NEG_INF, F32)
    bias2 = jnp.concatenate([jnp.concatenate([bias_t, out_of_band], axis=1),
                             jnp.concatenate([out_of_band, bias_t], axis=1)], axis=2).astype(F32)
    cur = pl.BlockSpec((1, t, width), lambda b, i: (b, i, 0))
    prev = pl.BlockSpec((1, t, width), lambda b, i: (b, jnp.maximum(i - 1, 0), 0))
    v_rows = bv_t.shape[0]
    cur_t = lambda r: pl.BlockSpec((r, t), lambda b, i: (0, b * nblk + i))
    prev_t = lambda r: pl.BlockSpec((r, t), lambda b, i: (0, b * nblk + jnp.maximum(i - 1, 0)))
    return pl.pallas_call(
        _chunk_attn_body, grid=(bn, nblk),
        in_specs=[cur_t(width), prev, cur, prev_t(v_rows), cur_t(v_rows), pl.BlockSpec(bias2.shape, lambda b, i: (0, 0, 0))],
        out_specs=cur, out_shape=jax.ShapeDtypeStruct((bn, seq, width), BF16),
        scratch_shapes=[pltpu.VMEM((2 * t, width), BF16), pltpu.VMEM((v_rows, 2 * t), BF16)],
        compiler_params=_params("parallel", "parallel"), name="chunk_attn",
    )(bq_t, bk, bk, bv_t, bv_t, bias2)


def _even_out_body(x_ref, ya_ref, yb_ref, w_ref, g_ref, b_ref, h_ref, ht_ref):
    mix = jnp.dot(ya_ref[...], w_ref[0:SEG, :], preferred_element_type=F32)
    mix = mix + jnp.dot(yb_ref[...], w_ref[SEG:, :], preferred_element_type=F32)
    h = _layer_norm(ALPHA * x_ref[...] + mix, g_ref[...], b_ref[...])
    h_ref[...] = h
    _store_token_tiles(ht_ref, h)


def _even_mixer(x2, bn, seq, rotary_tables, w_in, w_out, lam_p, subln_g, rel_bias, lam_init, ln_g, ln_b):
    aq_t, ak, av_t, bq_t, bk, bv_t = _even_proj(x2, w_in, rotary_tables, seq)
    lp = lam_p.astype(F32)
    lam = jnp.exp(jnp.sum(lp[0] * lp[1])) - jnp.exp(jnp.sum(lp[2] * lp[3])) + lam_init
    r3 = lambda a: a.reshape(bn, seq, SEG)
    ya = _diff_attn(aq_t, r3(ak), av_t, lam, subln_g, lam_init)
    yb = _chunk_attn(bq_t, r3(bk), bv_t, rel_bias)
    return _out_call(_even_out_body, [x2, ya.reshape(-1, SEG), yb.reshape(-1, SEG)],
                     [w_out.astype(BF16), ln_g.reshape(1, -1), ln_b.reshape(1, -1)], "even_out")


def _rwkv_pre_body(*refs, has_vres, blocks_per_seq):
    if has_vres:
        (x_ref, xp_ref, vf_ref, mix_ref, wr_ref, wk_ref, wv_ref, w0_ref, w1_ref, w2_ref, a0_ref, a1_ref, a2_ref,
         g1_ref, g2_ref, kk_ref, ka_ref, v0_ref, v1_ref, v2_ref, r_o, lw_o, k_o, v_o, kkr_o, a_o, g_o) = refs
    else:
        (x_ref, xp_ref, mix_ref, wr_ref, wk_ref, wv_ref, w0_ref, w1_ref, w2_ref, a0_ref, a1_ref, a2_ref,
         g1_ref, g2_ref, kk_ref, ka_ref, r_o, lw_o, k_o, v_o, kkr_o, a_o, g_o) = refs
    x = x_ref[...]
    first_of_seq = (pl.program_id(0) % blocks_per_seq) == 0
    before = jnp.where(first_of_seq, 0.0, xp_ref[SUBLANES - 1:SUBLANES, :])
    row = lax.broadcasted_iota(I32, x.shape, 0)
    xx = jnp.where(row == 0, before, pltpu.roll(x, 1, 0)) - x
    xr, xw, xk, xv, xa, xg = [x + xx * mix_ref[i:i + 1, :] for i in range(6)]
    r = _dot(xr, wr_ref[...])
    k = _dot(xk, wk_ref[...])
    v = _dot(xv, wv_ref[...])
    z = w0_ref[...] + _dot(jnp.tanh(_dot(xw, w1_ref[...])), w2_ref[...])
    softplus = jnp.maximum(-z, 0.0) + jnp.log(1.0 + jnp.exp(-jnp.abs(z)))
    lw_o[...] = -jnp.exp(-softplus - 0.5)
    if has_vres:
        v = v + (vf_ref[...].astype(F32) - v) * _sigmoid(v0_ref[...] + _dot(_dot(xv, v1_ref[...]), v2_ref[...]))
    a = _sigmoid(a0_ref[...] + _dot(_dot(xa, a1_ref[...]), a2_ref[...]))
    g_o[...] = _dot(_sigmoid(_dot(xg, g1_ref[...])), g2_ref[...]).astype(g_o.dtype)
    r_o[...] = r.astype(r_o.dtype)
    v_o[...] = v.astype(v_o.dtype)
    a_o[...] = a.astype(a_o.dtype)
    kkr_o[...] = (k * kk_ref[...]).astype(kkr_o.dtype)
    k_o[...] = (k * (1.0 + (a - 1.0) * ka_ref[...])).astype(k_o.dtype)


def _head_sum(x, first):
    zero = jnp.zeros_like(x)
    s0 = jnp.sum(jnp.where(first, x, zero), axis=1, keepdims=True)
    s1 = jnp.sum(jnp.where(first, zero, x), axis=1, keepdims=True)
    return jnp.where(first, s0, s1)


def _wkv_body(r_ref, lw_ref, k_ref, v_ref, kk_ref, a_ref, rk_ref, lg_ref, lb_ref, y_ref, z_s, *bufs):
    step = pl.program_id(2)
    half = len(bufs) // 2

    @pl.when(step == 0)
    def _():
        z_s[...] = jnp.zeros(z_s.shape, F32)
        for buf in bufs[half:]:
            buf[...] = jnp.zeros(buf.shape, buf.dtype)

    @pl.when(step % 2 == 0)
    def _():
        _wkv_step(r_ref, lw_ref, k_ref, v_ref, kk_ref, a_ref, rk_ref, lg_ref, lb_ref, y_ref, z_s,
                  bufs[:half], bufs[half:])

    @pl.when(step % 2 == 1)
    def _():
        _wkv_step(r_ref, lw_ref, k_ref, v_ref, kk_ref, a_ref, rk_ref, lg_ref, lb_ref, y_ref, z_s,
                  bufs[half:], bufs[:half])


def _wkv_step(r_ref, lw_ref, k_ref, v_ref, kk_ref, a_ref, rk_ref, lg_ref, lb_ref, y_ref, z_s, fill, drain):
    n = CHUNK
    lhs_f, y0_f, sadd_f, decay_f, bonus_f = fill
    lhs_d, y0_d, sadd_d, decay_d, bonus_d = drain
    first = lax.broadcasted_iota(I32, (n, LANES), 1) < C_DIM
    row = lax.broadcasted_iota(I32, (2 * n, 2 * n), 0)
    col = lax.broadcasted_iota(I32, (2 * n, 2 * n), 1)
    same = (row // n) == (col // n)
    strict = same & ((col % n) < (row % n))
    incl = same & ((col % n) <= (row % n))
    eye = (row == col).astype(F32)
    tri = (lax.broadcasted_iota(I32, (n, n), 1) <= lax.broadcasted_iota(I32, (n, n), 0)).astype(BF16)

    def stack(x):
        zero = jnp.zeros_like(x)
        return jnp.concatenate([jnp.where(first, x, zero), jnp.where(first, zero, x)], axis=0)

    zero = jnp.zeros((2 * n, 2 * n), F32)
    chunks = range(WKV_ROWS // n)
    prep = []
    z = z_s[...]
    y2 = []
    for c in chunks:
        both = jnp.dot(lhs_d[c], z.astype(BF16), preferred_element_type=F32)
        y2.append(both[0:2 * n] + y0_d[c])
        z = decay_d[c] * z + both[2 * n:] + sadd_d[c]
        rows = slice(c * n, (c + 1) * n)
        r, lw, k, v, kk, a = (ref[0, rows, :].astype(F32) for ref in (r_ref, lw_ref, k_ref, v_ref, kk_ref, a_ref))
        kk = kk / jnp.maximum(jnp.sqrt(_head_sum(kk * kk, first)), 1e-12)
        hi = lw.astype(BF16)
        rem = lw - hi.astype(F32)
        mid = rem.astype(BF16)
        low = (rem - mid.astype(F32)).astype(BF16)
        cum = (jnp.dot(tri, hi, preferred_element_type=F32) + jnp.dot(tri, mid, preferred_element_type=F32)
               + jnp.dot(tri, low, preferred_element_type=F32))
        total = cum[n - 1:n, :]
        g_inv = jnp.exp(-cum)
        g_tail = jnp.exp(total - cum)
        a2 = stack(-kk * jnp.exp(cum - lw))
        r2 = stack(r * jnp.exp(cum))
        b2 = stack(kk * a * g_inv)
        k2 = stack(k * g_inv)
        m = lax.dot_general(jnp.concatenate([a2, r2], axis=0).astype(BF16),
                            jnp.concatenate([b2, k2], axis=0).astype(BF16), _NT, preferred_element_type=F32)
        prep.append(dict(
            a2=a2, r2=r2, v2=stack(v), total=total, bonus=_head_sum(r * k * rk_ref[...], first) * v,
            tails_t=jnp.concatenate([stack(kk * a * g_tail), stack(k * g_tail)], axis=0).T.astype(BF16),
            a_ab=jnp.where(strict, m[0:2 * n, 0:2 * n], zero), a_ak=jnp.where(strict, m[0:2 * n, 2 * n:], zero),
            a_r=jnp.concatenate([jnp.where(incl, m[2 * n:, 0:2 * n], zero),
                                 jnp.where(incl, m[2 * n:, 2 * n:], zero)], axis=1).astype(BF16)))
    inv = [eye + p["a_ab"] for p in prep]
    power = [p["a_ab"].astype(BF16) for p in prep]
    power = [jnp.dot(pw, pw, preferred_element_type=F32).astype(BF16) for pw in power]
    for _ in range(int(math.log2(n)) - 2):
        both = [jnp.dot(jnp.concatenate([pw, iv.astype(BF16)], axis=0), pw, preferred_element_type=F32)
                for pw, iv in zip(power, inv)]
        power = [b[0:2 * n].astype(BF16) for b in both]
        inv = [iv + b[2 * n:] for iv, b in zip(inv, both)]
    inv = [iv + jnp.dot(iv.astype(BF16), pw, preferred_element_type=F32) for iv, pw in zip(inv, power)]
    akv = [_dot(p["a_ak"], p["v2"]) for p in prep]
    wu = [_dot(iv, jnp.concatenate([p["a2"], x], axis=1)) for iv, p, x in zip(inv, prep, akv)]
    rhs = [jnp.concatenate([w, jnp.concatenate([zero, p["v2"]], axis=1)], axis=0).astype(BF16)
           for w, p in zip(wu, prep)]
    ryqs = [jnp.dot(jnp.concatenate([p["a_r"], p["tails_t"]], axis=0), x, preferred_element_type=F32)
            for p, x in zip(prep, rhs)]
    for c in chunks:
        rows = slice(c * n, (c + 1) * n)
        p, x = prep[c], ryqs[c]
        lhs_f[c] = jnp.concatenate([p["r2"] + x[0:2 * n, 0:2 * n], x[2 * n:, 0:2 * n]], axis=0).astype(BF16)
        y0_f[c] = x[0:2 * n, 2 * n:]
        sadd_f[c] = x[2 * n:, 2 * n:]
        decay_f[c] = jnp.broadcast_to(jnp.sum(eye * jnp.exp(p["total"]), axis=1, keepdims=True), (2 * n, 2 * n))
        bonus_f[rows, :] = p["bonus"]
    z_s[...] = z
    for c in chunks:
        rows = slice(c * n, (c + 1) * n)
        y = y2[c][0:n] + y2[c][n:]
        mu = _head_sum(y, first) * (1.0 / C_DIM)
        yc = y - mu
        var = _head_sum(yc * yc, first) * (1.0 / C_DIM)
        y_ref[0, rows, :] = (yc * lax.rsqrt(var + LNX_EPS) * lg_ref[...] + lb_ref[...] + bonus_d[rows, :]).astype(y_ref.dtype)


def _wkv(r, lw, k, v, kkr, a, r_k, lnx_g, lnx_b):
    bn, seq, d = r.shape
    nblk, nc, n2 = seq // WKV_ROWS, WKV_ROWS // CHUNK, 2 * CHUNK
    blk = pl.BlockSpec((1, WKV_ROWS, LANES), lambda b, h, c: (b, jnp.minimum(c, nblk - 1), h))
    out = pl.BlockSpec((1, WKV_ROWS, LANES), lambda b, h, c: (b, jnp.maximum(c - 1, 0), h))
    par = pl.BlockSpec((1, LANES), lambda b, h, c: (0, h))
    buffers = [pltpu.VMEM((nc, 2 * n2, n2), BF16), pltpu.VMEM((nc, n2, n2), F32), pltpu.VMEM((nc, n2, n2), F32),
               pltpu.VMEM((nc, n2, n2), F32), pltpu.VMEM((WKV_ROWS, LANES), F32)]
    return pl.pallas_call(
        _wkv_body, grid=(bn, d // LANES, nblk + 1),
        in_specs=[blk] * 6 + [par] * 3, out_specs=out,
        out_shape=jax.ShapeDtypeStruct((bn, seq, d), BF16),
        scratch_shapes=[pltpu.VMEM((LANES, LANES), F32)] + buffers + buffers,
        compiler_params=_params("parallel", "parallel", "arbitrary"), name="wkv7",
    )(r, lw, k, v, kkr, a, r_k.reshape(1, d), lnx_g.reshape(1, d), lnx_b.reshape(1, d))


def _odd_out_body(x_ref, y_ref, g_ref, w_ref, lg_ref, lb_ref, h_ref, ht_ref):
    mix = _dot(y_ref[...].astype(F32) * g_ref[...].astype(F32), w_ref[...])
    h = _layer_norm(ALPHA * x_ref[...] + mix, lg_ref[...], lb_ref[...])
    h_ref[...] = h
    _store_token_tiles(ht_ref, h)


def _rwkv_mixer(x2, bn, seq, v_first, mix, w_rkv, w0, w1, w2, a0, a1, a2, g1, g2, k_k, k_a, r_k, lnx_g, lnx_b,
                w_out, vres, ln_g, ln_b):
    d = x2.shape[1]
    row = lambda p: p.reshape(1, -1)
    bf = lambda p: p.astype(BF16)
    groups = ROW_TILE // SUBLANES
    rows = [x2, (x2, lambda i: jnp.maximum(i * groups - 1, 0), SUBLANES)] + ([v_first] if vres is not None else [])
    fulls = [mix, bf(w_rkv[0]), bf(w_rkv[1]), bf(w_rkv[2]), row(w0), bf(w1), bf(w2), row(a0), bf(a1), bf(a2),
             bf(g1), bf(g2), row(k_k), row(k_a)]
    if vres is not None:
        fulls += [row(vres[0]), bf(vres[1]), bf(vres[2])]
    body = functools.partial(_rwkv_pre_body, has_vres=vres is not None, blocks_per_seq=seq // ROW_TILE)
    outs = [(d, BF16), (d, F32)] + [(d, BF16)] * 5
    r, lw, k, v, kkr, a, g = _row_call(body, rows, fulls, outs, ROW_TILE, "rwkv_pre")
    if vres is None:
        v_first = v
    r3 = lambda t: t.reshape(bn, seq, d)
    y = _wkv(r3(r), r3(lw), r3(k), r3(v), r3(kkr), r3(a), r_k, lnx_g, lnx_b)
    h, ht = _out_call(_odd_out_body, [x2, y.reshape(-1, d), g], [bf(w_out), row(ln_g), row(ln_b)], "odd_out")
    return h, ht, v_first


def _router_body(h_ref, wt_ref, b_ref, upper_ref, idx_ref, gate_ref, rank_ref, cnt_ref, carry_s):
    @pl.when(pl.program_id(0) == 0)
    def _():
        carry_s[...] = jnp.zeros(carry_s.shape, F32)

    logits = lax.dot_general(wt_ref[...], h_ref[...], _NT, precision=lax.Precision.HIGHEST,
                             preferred_element_type=F32)
    scores = _sigmoid(logits)
    sel = scores + b_ref[...]
    sub = lax.broadcasted_iota(I32, sel.shape, 0)
    grp = sub // EXPERTS_PER_GROUP
    ninf = jnp.full(sel.shape, -jnp.inf, F32)
    big = jnp.full(sel.shape, N_EXPERTS, I32)

    def top2(vals):
        m1 = jnp.max(vals, axis=0, keepdims=True)
        i1 = jnp.min(jnp.where(vals == m1, sub, big), axis=0, keepdims=True)
        rest = jnp.where(sub == i1, ninf, vals)
        m2 = jnp.max(rest, axis=0, keepdims=True)
        i2 = jnp.min(jnp.where(rest == m2, sub, big), axis=0, keepdims=True)
        return m1, i1, m2, i2

    best = jnp.zeros((1, sel.shape[1]), I32)
    best_score = None
    for g in range(N_GROUPS):
        m1, _, m2, _ = top2(jnp.where(grp == g, sel, ninf))
        score = m1 + m2
        if g == 0:
            best_score = score
        else:
            better = score > best_score
            best = jnp.where(better, g, best)
            best_score = jnp.where(better, score, best_score)
    _, i1, _, i2 = top2(jnp.where(grp == best, sel, ninf))
    oh1, oh2 = sub == i1, sub == i2
    zero = jnp.zeros_like(scores)
    g1 = jnp.sum(jnp.where(oh1, scores, zero), axis=0, keepdims=True)
    g2 = jnp.sum(jnp.where(oh2, scores, zero), axis=0, keepdims=True)
    den = g1 + g2
    idx_ref[...] = jnp.concatenate([i1, i2], axis=0)
    gate_ref[...] = jnp.concatenate([g1 / den, g2 / den], axis=0)
    both = (oh1 | oh2).astype(BF16)
    before = jnp.dot(both, upper_ref[...], preferred_element_type=F32) + carry_s[...]
    rank1 = jnp.sum(jnp.where(oh1, before, zero), axis=0, keepdims=True)
    rank2 = jnp.sum(jnp.where(oh2, before, zero), axis=0, keepdims=True)
    rank_ref[...] = jnp.concatenate([rank1, rank2], axis=0).astype(I32)
    carry_s[...] = carry_s[...] + jnp.sum(both.astype(F32), axis=1, keepdims=True)
    cnt_ref[...] = carry_s[...]


def _router(h, router_w, router_b):
    n, d = h.shape
    tm = ROUTE_TILE
    upper = (jnp.arange(tm)[:, None] < jnp.arange(tm)[None, :]).astype(BF16)
    tok = pl.BlockSpec((2, tm), lambda i: (0, i))
    return pl.pallas_call(
        _router_body, grid=(n // tm,),
        in_specs=[pl.BlockSpec((tm, d), lambda i: (i, 0)), pl.BlockSpec((N_EXPERTS, d), lambda i: (0, 0)),
                  pl.BlockSpec((N_EXPERTS, 1), lambda i: (0, 0)), pl.BlockSpec((tm, tm), lambda i: (0, 0))],
        out_specs=[tok, tok, tok, pl.BlockSpec((N_EXPERTS, 1), lambda i: (0, 0))],
        out_shape=[jax.ShapeDtypeStruct((2, n), I32), jax.ShapeDtypeStruct((2, n), F32),
                   jax.ShapeDtypeStruct((2, n), I32), jax.ShapeDtypeStruct((N_EXPERTS, 1), F32)],
        scratch_shapes=[pltpu.VMEM((N_EXPERTS, 1), F32)],
        compiler_params=_params("arbitrary"), name="router",
    )(h, router_w.T.astype(F32), router_b.reshape(N_EXPERTS, 1).astype(F32), upper)


def _load_token_tiles(ref, first_row, tokens):
    chunks = [ref[pl.ds(first_row + j, tokens, stride=TOKEN_CHUNKS), :] for j in range(TOKEN_CHUNKS)]
    return jnp.concatenate(chunks, axis=1)


def _store_token_tiles(ref, val):
    tokens = val.shape[0]
    for j in range(TOKEN_CHUNKS):
        ref[pl.ds(j, tokens, stride=TOKEN_CHUNKS), :] = val[:, j * LANES:(j + 1) * LANES].astype(ref.dtype)


def _tile_copy(src, dst, src_tok, dst_tok, sem):
    src_row = pl.multiple_of(src_tok * TOKEN_CHUNKS, TOKEN_CHUNKS)
    dst_row = pl.multiple_of(dst_tok * TOKEN_CHUNKS, TOKEN_CHUNKS)
    return pltpu.make_async_copy(src.at[pl.ds(src_row, TOKEN_CHUNKS), :], dst.at[pl.ds(dst_row, TOKEN_CHUNKS), :], sem)


def _dispatch_body(pad_end_ref, padded_ref, dest_ref, ht_ref, xb_ref, zero_s, sem):
    block_rows = MOE_TILE * TOKEN_CHUNKS
    n_blocks = xb_ref.shape[0] // block_rows

    def block_copy(b):
        return pltpu.make_async_copy(zero_s, xb_ref.at[pl.ds(pl.multiple_of(b * block_rows, block_rows), block_rows), :], sem)

    @pl.when(pl.program_id(0) == 0)
    def _():
        zero_s[...] = jnp.zeros(zero_s.shape, zero_s.dtype)

        def start(e, carry):
            @pl.when(padded_ref[e] > 0)
            def _():
                block_copy(pad_end_ref[e] // MOE_TILE - 1).start()
            return carry

        def wait(e, carry):
            @pl.when(padded_ref[e] > 0)
            def _():
                block_copy(0).wait()
            return carry

        lax.fori_loop(0, N_EXPERTS, start, 0)
        lax.fori_loop(0, N_EXPERTS, wait, 0)
        first_unused = pad_end_ref[N_EXPERTS - 1] // MOE_TILE
        lax.fori_loop(first_unused, n_blocks, lambda b, c: (block_copy(b).start(), c)[1], 0)
        lax.fori_loop(first_unused, n_blocks, lambda b, c: (block_copy(0).wait(), c)[1], 0)

    def start_token(t, carry):
        _tile_copy(ht_ref, xb_ref, t, dest_ref[0, 0, t], sem).start()
        _tile_copy(ht_ref, xb_ref, t, dest_ref[0, 0, GATHER_TILE + t], sem).start()
        return carry

    def wait_row(t, carry):
        _tile_copy(ht_ref, xb_ref, 0, 0, sem).wait()
        return carry

    lax.fori_loop(0, GATHER_TILE, start_token, 0, unroll=8)
    lax.fori_loop(0, 2 * GATHER_TILE, wait_row, 0, unroll=8)


def _dispatch(ht, dest_blocks, pad_end, padded, cap):
    n = ht.shape[0] // TOKEN_CHUNKS
    grid_spec = pltpu.PrefetchScalarGridSpec(
        num_scalar_prefetch=2, grid=(n // GATHER_TILE,),
        in_specs=[pl.BlockSpec((1, 1, 2 * GATHER_TILE), lambda i, pe, pd: (i, 0, 0), memory_space=pltpu.SMEM),
                  pl.BlockSpec((GATHER_TILE * TOKEN_CHUNKS, LANES), lambda i, pe, pd: (i, 0))],
        out_specs=pl.BlockSpec(memory_space=pl.ANY),
        scratch_shapes=[pltpu.VMEM((MOE_TILE * TOKEN_CHUNKS, LANES), F32), pltpu.SemaphoreType.DMA(())])
    return pl.pallas_call(
        _dispatch_body, grid_spec=grid_spec, out_shape=jax.ShapeDtypeStruct((cap * TOKEN_CHUNKS, LANES), F32),
        compiler_params=_params("arbitrary"), name="moe_dispatch",
    )(pad_end, padded, dest_blocks, ht)


def _expert_body(blk_ref, exp_ref, used_ref, x_ref, wg_ref, wu_ref, wd_ref, y_ref, wg_s, wu_s, wd_s):
    i = pl.program_id(0)

    @pl.when((i == 0) | (exp_ref[i] != exp_ref[jnp.maximum(i - 1, 0)]))
    def _():
        wg_s[...] = wg_ref[0].astype(BF16)
        wu_s[...] = wu_ref[0].astype(BF16)
        wd_s[...] = wd_ref[0].astype(BF16)

    @pl.when(i < used_ref[0])
    def _():
        x = _load_token_tiles(x_ref, 0, MOE_TILE).astype(BF16)
        gate = jnp.dot(x, wg_s[...], preferred_element_type=F32)
        up = jnp.dot(x, wu_s[...], preferred_element_type=F32)
        u = gate * _sigmoid(gate) * up
        _store_token_tiles(y_ref, jnp.dot(u.astype(BF16), wd_s[...], preferred_element_type=F32))

    @pl.when(i >= used_ref[0])
    def _():
        y_ref[...] = jnp.zeros(y_ref.shape, y_ref.dtype)


def _experts(xb, blk, blk_expert, n_used, layer, w_gate, w_up, w_down):
    d = D_MODEL
    rows = pl.BlockSpec((MOE_TILE * TOKEN_CHUNKS, LANES), lambda i, blk, ex, used: (blk[i], 0))
    out_rows = pl.BlockSpec((MOE_TILE * TOKEN_CHUNKS, LANES), lambda i, blk, ex, used: (i, 0))
    grid_spec = pltpu.PrefetchScalarGridSpec(
        num_scalar_prefetch=3, grid=(xb.shape[0] // (MOE_TILE * TOKEN_CHUNKS),),
        in_specs=[rows,
                  pl.BlockSpec((None, 1, d, D_EXPERT), lambda i, blk, ex, used: (layer, ex[i], 0, 0)),
                  pl.BlockSpec((None, 1, d, D_EXPERT), lambda i, blk, ex, used: (layer, ex[i], 0, 0)),
                  pl.BlockSpec((None, 1, D_EXPERT, d), lambda i, blk, ex, used: (layer, ex[i], 0, 0))],
        out_specs=out_rows,
        scratch_shapes=[pltpu.VMEM((d, D_EXPERT), BF16), pltpu.VMEM((d, D_EXPERT), BF16),
                        pltpu.VMEM((D_EXPERT, d), BF16)])
    return pl.pallas_call(
        _expert_body, grid_spec=grid_spec, out_shape=jax.ShapeDtypeStruct(xb.shape, F32),
        compiler_params=_params("arbitrary"), name="moe_experts",
    )(blk, blk_expert, n_used, xb, w_gate, w_up, w_down)


def _combine_body(dest_ref, h_ref, gate_ref, lg_ref, lb_ref, yb_ref, o_ref, rows_s, sem):
    def start_row(t, carry):
        _tile_copy(yb_ref, rows_s, dest_ref[0, 0, t], t, sem).start()
        return carry

    def wait_row(t, carry):
        _tile_copy(yb_ref, rows_s, 0, 0, sem).wait()
        return carry

    lax.fori_loop(0, 2 * GATHER_TILE, start_row, 0, unroll=8)
    lax.fori_loop(0, 2 * GATHER_TILE, wait_row, 0, unroll=8)
    gate = gate_ref[...]
    ffn = (gate[:, 0:1] * _load_token_tiles(rows_s, 0, GATHER_TILE)
           + gate[:, 1:2] * _load_token_tiles(rows_s, GATHER_TILE * TOKEN_CHUNKS, GATHER_TILE))
    o_ref[...] = _layer_norm(ALPHA * h_ref[...] + ffn, lg_ref[...], lb_ref[...])


def _combine(h, gate_rows, dest_blocks, yb, ln_g, ln_b):
    n, d = h.shape
    tm = GATHER_TILE
    return pl.pallas_call(
        _combine_body, grid=(n // tm,),
        in_specs=[pl.BlockSpec((1, 1, 2 * tm), lambda i: (i, 0, 0), memory_space=pltpu.SMEM),
                  pl.BlockSpec((tm, d), lambda i: (i, 0)), pl.BlockSpec((tm, 2), lambda i: (i, 0)),
                  pl.BlockSpec((1, d), lambda i: (0, 0)), pl.BlockSpec((1, d), lambda i: (0, 0)),
                  pl.BlockSpec(memory_space=pl.ANY)],
        out_specs=pl.BlockSpec((tm, d), lambda i: (i, 0)), out_shape=jax.ShapeDtypeStruct((n, d), F32),
        scratch_shapes=[pltpu.VMEM((2 * tm * TOKEN_CHUNKS, LANES), F32), pltpu.SemaphoreType.DMA(())],
        compiler_params=_params("arbitrary"), name="moe_combine",
    )(dest_blocks, h, gate_rows, ln_g.reshape(1, d), ln_b.reshape(1, d), yb)


def _moe_layer(h, ht, router_w, router_b, layer, w_gate, w_up, w_down, ln_g, ln_b):
    n, d = h.shape
    idx, gate, rank, cnt = _router(h, router_w, router_b)
    counts = cnt[:, 0].astype(I32)
    padded = (counts + MOE_TILE - 1) // MOE_TILE * MOE_TILE
    pad_end = jnp.cumsum(padded).astype(I32)
    pad_start = pad_end - padded
    experts = jnp.arange(N_EXPERTS, dtype=I32)
    dest = jnp.sum(jnp.where(idx[..., None] == experts, pad_start, 0), axis=-1) + rank
    cap = 2 * n + N_EXPERTS * MOE_TILE
    n_blocks = cap // MOE_TILE
    n_used = pad_end[-1] // MOE_TILE
    blk = jnp.minimum(jnp.arange(n_blocks, dtype=I32), n_used - 1)
    blk_expert = jnp.minimum(jnp.sum(pad_end[None, :] <= (blk * MOE_TILE)[:, None], axis=1), N_EXPERTS - 1).astype(I32)
    tm = GATHER_TILE
    dest_blocks = dest.reshape(2, n // tm, tm).transpose(1, 0, 2).reshape(n // tm, 1, 2 * tm)
    xb = _dispatch(ht, dest_blocks, pad_end, padded, cap)
    yb = _experts(xb, blk, blk_expert, n_used.reshape(1).astype(I32), layer, w_gate, w_up, w_down)
    return _combine(h, gate.T, dest_blocks, yb, ln_g, ln_b)


def kernel(x, ln_g, ln_b, even_w_in, even_w_out, even_lambda, even_subln_g, even_rel_bias, odd_mix, odd_w_rkv, odd_w0, odd_w1, odd_w2, odd_a0, odd_a1, odd_a2, odd_g1, odd_g2, odd_k_k, odd_k_a, odd_r_k, odd_lnx_g, odd_lnx_b, odd_w_out, vres_v0, vres_v1, vres_v2, router_w, router_b, moe_w_gate, moe_w_up, moe_w_down):
    bn, seq, d = x.shape
    x2 = x.reshape(bn * seq, d)
    v_first = None
    rotary_tables = _rotary_tables(seq)
    for layer in range(DEPTH):
        if layer % 2 == 0:
            e = layer // 2
            lam_init = 0.8 - 0.6 * math.exp(-0.3 * layer)
            h, ht = _even_mixer(x2, bn, seq, rotary_tables, even_w_in[e], even_w_out[e], even_lambda[e],
                                even_subln_g[e], even_rel_bias[e], lam_init, ln_g[layer, 0], ln_b[layer, 0])
        else:
            o = layer // 2
            vres = None if o == 0 else (vres_v0[o - 1], vres_v1[o - 1], vres_v2[o - 1])
            h, ht, v_first = _rwkv_mixer(x2, bn, seq, v_first, odd_mix[o], odd_w_rkv[o], odd_w0[o], odd_w1[o], odd_w2[o],
                                     odd_a0[o], odd_a1[o], odd_a2[o], odd_g1[o], odd_g2[o], odd_k_k[o], odd_k_a[o],
                                     odd_r_k[o], odd_lnx_g[o], odd_lnx_b[o], odd_w_out[o], vres,
                                     ln_g[layer, 0], ln_b[layer, 0])
        x2 = _moe_layer(h, ht, router_w, router_b, layer, moe_w_gate, moe_w_up, moe_w_down,
                        ln_g[layer, 1], ln_b[layer, 1])
    return x2.reshape(bn, seq, d)
```

```python
import functools
import math

import jax
import jax.numpy as jnp
from jax import lax
from jax.experimental import pallas as pl
from jax.experimental.pallas import tpu as pltpu

F32 = jnp.float32
BF16 = jnp.bfloat16
I32 = jnp.int32

LANES = 128
SUBLANES = 8
VMEM_LIMIT_BYTES = 56 * 1024 * 1024

D_MODEL = 1024
DEPTH = 4
CHUNK = 64
A_HEADS = 4
A_DIM = 64
A_VDIM = 128
ROPE_DIM = 16
ROPE_THETA = 500000.0
SUBLN_EPS = 1e-5
B_HEADS = 8
B_DIM = 64
B_CHUNKS_BACK = 8
REL_CLIP = 128
BAND = (B_CHUNKS_BACK + 1) * CHUNK
SEG = 512
C_DIM = 64
LNX_EPS = 1e-5 * C_DIM
N_EXPERTS = 32
N_GROUPS = 4
EXPERTS_PER_GROUP = 8
D_EXPERT = 512
LN_EPS = 1e-5
ALPHA = (2 * DEPTH) ** 0.25
NEG_INF = -1e30

ROW_TILE = 512
ATT_TILE = 2048
ATT_KEYS = 256
V_ROWS = A_VDIM + 16
LOG2_E = math.log2(math.e)
Q_SCALE = A_DIM ** -0.5 * LOG2_E
BAND_TILE = B_CHUNKS_BACK * CHUNK
WKV_ROWS = 512
MOE_TILE = 512
ROUTE_TILE = 512
GATHER_TILE = 256
TOKEN_CHUNKS = D_MODEL // LANES

_NT = (((1,), (1,)), ((), ()))
_TN = (((0,), (0,)), ((), ()))


def _params(*sem):
    return pltpu.CompilerParams(dimension_semantics=sem, vmem_limit_bytes=VMEM_LIMIT_BYTES)


def _dot(a, b):
    return jnp.dot(a.astype(BF16), b.astype(BF16), preferred_element_type=F32)


def _layer_norm(x, g, b):
    mu = jnp.mean(x, axis=-1, keepdims=True)
    xc = x - mu
    var = jnp.mean(xc * xc, axis=-1, keepdims=True)
    return xc * lax.rsqrt(var + LN_EPS) * g + b


def _sigmoid(x):
    return 1.0 / (1.0 + jnp.exp(-x))


def _row_call(body, rows, fulls, outs, tm, name):
    m = rows[0].shape[0] if not isinstance(rows[0], tuple) else None
    in_specs, args = [], []
    for r in rows:
        if isinstance(r, tuple):
            arr, fn, block_rows = r
            in_specs.append(pl.BlockSpec((block_rows, arr.shape[1]), lambda i, fn=fn: (fn(i), 0)))
        else:
            arr = r
            m = arr.shape[0]
            in_specs.append(pl.BlockSpec((tm, arr.shape[1]), lambda i: (i, 0)))
        args.append(arr)
    for f in fulls:
        in_specs.append(pl.BlockSpec(f.shape, lambda i, nd=f.ndim: (0,) * nd))
        args.append(f)
    out_specs = [pl.BlockSpec((tm, w), lambda i: (i, 0)) for w, _ in outs]
    out_shape = [jax.ShapeDtypeStruct((m, w), dt) for w, dt in outs]
    return pl.pallas_call(
        body, grid=(m // tm,), in_specs=in_specs, out_specs=out_specs, out_shape=out_shape,
        compiler_params=_params("parallel"), name=name)(*args)


def _out_call(body, rows, fulls, name):
    n, tm = rows[0].shape[0], ROW_TILE
    in_specs = [pl.BlockSpec((tm, r.shape[1]), lambda i: (i, 0)) for r in rows]
    in_specs += [pl.BlockSpec(f.shape, lambda i, nd=f.ndim: (0,) * nd) for f in fulls]
    return pl.pallas_call(
        body, grid=(n // tm,), in_specs=in_specs,
        out_specs=[pl.BlockSpec((tm, D_MODEL), lambda i: (i, 0)), pl.BlockSpec((tm * TOKEN_CHUNKS, LANES), lambda i: (i, 0))],
        out_shape=[jax.ShapeDtypeStruct((n, D_MODEL), F32), jax.ShapeDtypeStruct((n * TOKEN_CHUNKS, LANES), F32)],
        compiler_params=_params("parallel"), name=name)(*rows, *fulls)


def _even_proj_body(x_ref, c_ref, sm_ref, sp_ref, ct_ref, smt_ref, spt_ref, w_ref, wt_ref,
                    aq_ref, ak_ref, av_ref, bq_ref, bk_ref, bv_ref):
    x = x_ref[...]
    xb = x.astype(BF16)
    half = ROPE_DIM // 2

    def seg(j):
        return jnp.dot(xb, w_ref[:, j * SEG:(j + 1) * SEG], preferred_element_type=F32)

    qv_t = jnp.dot(wt_ref[...], x.T.astype(BF16), preferred_element_type=F32)
    cos, s_minus, s_plus = ct_ref[...], smt_ref[...], spt_ref[...]
    for j in range(SEG // LANES):
        u = qv_t[j * LANES:(j + 1) * LANES, :]
        rot = u * cos + pltpu.roll(u, LANES - half, 0) * s_minus + pltpu.roll(u, half, 0) * s_plus
        aq_ref[j * LANES:(j + 1) * LANES, :] = (rot * Q_SCALE).astype(aq_ref.dtype)
    for v_ref, base in ((av_ref, SEG), (bv_ref, 3 * SEG)):
        for h in range(SEG // LANES):
            v_ref[h * V_ROWS:h * V_ROWS + LANES, :] = qv_t[base + h * LANES:base + (h + 1) * LANES, :].astype(v_ref.dtype)
            v_ref[h * V_ROWS + LANES:(h + 1) * V_ROWS, :] = jnp.ones((V_ROWS - LANES, x.shape[0]), v_ref.dtype)
    bq_ref[...] = (qv_t[2 * SEG:3 * SEG, :] * (B_DIM ** -0.5 * LOG2_E)).astype(bq_ref.dtype)
    cos, s_minus, s_plus = c_ref[...], sm_ref[...], sp_ref[...]
    k = seg(1)
    for j in range(SEG // LANES):
        u = k[:, j * LANES:(j + 1) * LANES]
        rot = u * cos + pltpu.roll(u, LANES - half, 1) * s_minus + pltpu.roll(u, half, 1) * s_plus
        ak_ref[:, j * LANES:(j + 1) * LANES] = rot.astype(ak_ref.dtype)
    bk_ref[...] = seg(4).astype(bk_ref.dtype)


def _rotary_tables(seq):
    half = ROPE_DIM // 2
    inv = ROPE_THETA ** (-jnp.arange(half, dtype=F32) / half)
    ang = jnp.arange(seq, dtype=F32)[:, None] * inv
    cos, sin = jnp.cos(ang), jnp.sin(ang)
    d = jnp.arange(LANES) % A_DIM
    first, second = d < half, (d >= half) & (d < ROPE_DIM)
    col = jnp.where(first, d, d - half) % half
    cos_t = jnp.where(first | second, cos[:, col], 1.0)
    s_minus = jnp.where(first, -sin[:, col], 0.0)
    s_plus = jnp.where(second, sin[:, col], 0.0)
    return cos_t, s_minus, s_plus


def _even_proj(x2, w_in, tabs, seq):
    n, d = x2.shape
    tm = ROW_TILE
    blocks = seq // tm
    w = w_in.astype(BF16)
    seg = lambda j: w[:, j * SEG:(j + 1) * SEG]
    w_qv_t = jnp.concatenate([seg(0), seg(2), seg(3), seg(5)], axis=1).T
    tabs_t = [t.T for t in tabs]
    rows = pl.BlockSpec((tm, SEG), lambda i: (i, 0))
    cols = lambda r: pl.BlockSpec((r, tm), lambda i: (0, i))
    full = lambda a: pl.BlockSpec(a.shape, lambda i: (0, 0))
    v_rows = SEG // LANES * V_ROWS
    shapes = [(SEG, n), (n, SEG), (v_rows, n), (SEG, n), (n, SEG), (v_rows, n)]
    return pl.pallas_call(
        _even_proj_body, grid=(n // tm,),
        in_specs=[pl.BlockSpec((tm, d), lambda i: (i, 0))]
        + [pl.BlockSpec((tm, LANES), lambda i: (i % blocks, 0))] * 3
        + [pl.BlockSpec((LANES, tm), lambda i: (0, i % blocks))] * 3 + [full(w), full(w_qv_t)],
        out_specs=[cols(SEG), rows, cols(v_rows), cols(SEG), rows, cols(v_rows)],
        out_shape=[jax.ShapeDtypeStruct(s, BF16) for s in shapes],
        compiler_params=_params("parallel"), name="even_proj")(x2, *tabs, *tabs_t, w, w_qv_t)


def _diff_attn_body(qi_ref, kj_ref, lam_ref, q_ref, k_ref, v_ref, g_ref, o_ref, q2_s, m_s, l_s, acc_s, *, lam_init):
    p = pl.program_id(2)
    qi, kj = qi_ref[p], kj_ref[p]
    groups = ATT_TILE // LANES

    @pl.when(kj == 0)
    def _():
        feat = lax.broadcasted_iota(I32, (LANES, LANES), 0)
        for j in range(groups):
            q = q_ref[:, j * LANES:(j + 1) * LANES]
            zero = jnp.zeros_like(q)
            q2_s[j, :, 0:LANES] = jnp.where(feat < A_DIM, q, zero)
            q2_s[j, :, LANES:] = jnp.where(feat >= A_DIM, q, zero)
        m_s[...] = jnp.full(m_s.shape, -jnp.inf, F32)
        l_s[...] = jnp.zeros(l_s.shape, F32)
        acc_s[...] = jnp.zeros(acc_s.shape, F32)

    def step(diagonal):
        pieces = range(ATT_TILE // ATT_KEYS)

        def live(j, i):
            return not diagonal or i * ATT_KEYS // CHUNK <= ((j + 1) * LANES - 1) // CHUNK

        def scores(j, i):
            if not live(j, i):
                return None
            s = jnp.dot(k_ref[0, i * ATT_KEYS:(i + 1) * ATT_KEYS, :], q2_s[j], preferred_element_type=F32)
            if diagonal:
                key_chunk = (lax.broadcasted_iota(I32, s.shape, 0) + i * ATT_KEYS) // CHUNK
                query_chunk = (lax.broadcasted_iota(I32, s.shape, 1) % LANES + j * LANES) // CHUNK
                s = jnp.where(key_chunk <= query_chunk, s, NEG_INF)
            return s

        def stats(j, s):
            m = m_s[j]
            m_new = m
            for piece in s:
                if piece is not None:
                    m_new = jnp.maximum(m_new, jnp.max(piece, axis=0, keepdims=True))
            m_s[j] = m_new
            return jnp.exp2(m - m_new), m_new

        def values(pr, i):
            if pr[i] is None:
                return None
            return jnp.dot(v_ref[:, i * ATT_KEYS:(i + 1) * ATT_KEYS], pr[i], preferred_element_type=F32)

        def accumulate(j, alpha, pv):
            pv = functools.reduce(lambda a, b: a + b, [x for x in pv if x is not None])
            acc_s[j] = alpha * acc_s[j] + pv[0:A_VDIM]
            l_s[j] = alpha * l_s[j] + pv[A_VDIM:A_VDIM + 1]

        s = [scores(0, i) for i in pieces]
        alpha, m_new = stats(0, s)
        pending = None
        for j in range(groups):
            s_next, pr, pv = [], [], []
            for i in pieces:
                if j + 1 < groups:
                    s_next.append(scores(j + 1, i))
                pr.append(None if s[i] is None else jnp.exp2(s[i] - m_new).astype(BF16))
                if pending is not None:
                    pv.append(values(pending[2], i))
            if pending is not None:
                accumulate(pending[0], pending[1], pv)
            pending = (j, alpha, pr)
            if j + 1 < groups:
                s = s_next
                alpha, m_new = stats(j + 1, s)
        accumulate(pending[0], pending[1], [values(pending[2], i) for i in pieces])

    @pl.when(kj != qi)
    def _():
        step(False)

    @pl.when(kj == qi)
    def _():
        step(True)
        for j in range(groups):
            o = acc_s[j] / l_s[j]
            o = o[:, 0:LANES] - lam_ref[0] * o[:, LANES:]
            od = o.T
            od = od * lax.rsqrt(jnp.mean(od * od, axis=-1, keepdims=True) + SUBLN_EPS) * g_ref[...] * (1.0 - lam_init)
            o_ref[0, j * LANES:(j + 1) * LANES, :] = od.astype(o_ref.dtype)


def _diff_attn(aq_t, ak, av_t, lam, subln_g, lam_init):
    bn, seq, _ = ak.shape
    t = ATT_TILE
    nblk = seq // t
    pairs = [(i, j) for i in range(nblk) for j in range(i + 1)]
    qi = jnp.asarray([p[0] for p in pairs], I32)
    kj = jnp.asarray([p[1] for p in pairs], I32)
    groups, width = t // LANES, 2 * LANES
    grid_spec = pltpu.PrefetchScalarGridSpec(
        num_scalar_prefetch=3, grid=(bn, A_HEADS, len(pairs)),
        in_specs=[
            pl.BlockSpec((LANES, t), lambda b, h, p, qi, kj, lam: (h, b * nblk + qi[p])),
            pl.BlockSpec((1, t, LANES), lambda b, h, p, qi, kj, lam: (b, kj[p], h)),
            pl.BlockSpec((V_ROWS, t), lambda b, h, p, qi, kj, lam: (h, b * nblk + kj[p])),
            pl.BlockSpec((1, LANES), lambda b, h, p, qi, kj, lam: (0, 0)),
        ],
        out_specs=pl.BlockSpec((1, t, LANES), lambda b, h, p, qi, kj, lam: (b, qi[p], h)),
        scratch_shapes=[pltpu.VMEM((groups, LANES, width), BF16), pltpu.VMEM((groups, 1, width), F32),
                        pltpu.VMEM((groups, 1, width), F32), pltpu.VMEM((groups, A_VDIM, width), F32)])
    return pl.pallas_call(
        functools.partial(_diff_attn_body, lam_init=lam_init), grid_spec=grid_spec,
        out_shape=jax.ShapeDtypeStruct((bn, seq, A_HEADS * A_VDIM), BF16),
        compiler_params=_params("parallel", "parallel", "arbitrary"), name="diff_attn",
    )(qi, kj, lam.reshape(1).astype(F32), aq_t, ak, av_t, subln_g.reshape(1, A_VDIM))


def _chunk_attn_body(q_ref, kp_ref, kc_ref, vp_ref, vc_ref, bias_ref, o_ref, kcat, vcat):
    i = pl.program_id(1)
    t = BAND_TILE
    kcat[0:t, :] = kp_ref[0]
    kcat[t:, :] = kc_ref[0]
    vcat[:, 0:t] = vp_ref[...]
    vcat[:, t:] = vc_ref[...]
    pair = 2 * CHUNK
    keys = BAND + CHUNK
    feat = lax.broadcasted_iota(I32, (LANES, pair), 0)
    key = lax.broadcasted_iota(I32, (keys, pair), 0)
    items = [(cp, hp, hh) for cp in range(t // pair) for hp in range(B_HEADS // 2) for hh in range(2)]

    def scores(item):
        cp, hp, hh = item
        q = q_ref[hp * LANES:(hp + 1) * LANES, cp * pair:(cp + 1) * pair]
        q = jnp.where((feat < B_DIM) if hh == 0 else (feat >= B_DIM), q, jnp.zeros_like(q))
        s = jnp.dot(kcat[cp * pair:cp * pair + keys, hp * LANES:(hp + 1) * LANES], q, preferred_element_type=F32)
        s = s + bias_ref[2 * hp + hh]
        return jnp.where((i > 0) | (key + cp * pair >= t), s, NEG_INF)

    s_next = scores(items[0])
    halves = []
    for n, (cp, hp, hh) in enumerate(items):
        s = s_next
        if n + 1 < len(items):
            s_next = scores(items[n + 1])
        pr = jnp.exp2(s - jnp.max(s, axis=0, keepdims=True)).astype(BF16)
        pv = jnp.dot(vcat[hp * V_ROWS:(hp + 1) * V_ROWS, cp * pair:cp * pair + keys], pr, preferred_element_type=F32)
        halves.append(pv[hh * B_DIM:(hh + 1) * B_DIM] / pv[LANES:LANES + 1])
        if hh == 1:
            out = jnp.concatenate(halves, axis=0).T
            o_ref[0, cp * pair:(cp + 1) * pair, hp * LANES:(hp + 1) * LANES] = out.astype(o_ref.dtype)
            halves = []


def _chunk_attn(bq_t, bk, bv_t, rel_bias):
    bn, seq, width = bk.shape
    t = BAND_TILE
    nblk = seq // t
    far = jnp.broadcast_to(rel_bias[:, 2 * REL_CLIP:], (B_HEADS, BAND - REL_CLIP))
    f = jnp.concatenate([far, rel_bias[:, REL_CLIP - CHUNK + 1:2 * REL_CLIP][:, ::-1]], axis=1) * LOG2_E
    span = f.shape[1]
    flat = jnp.broadcast_to(f[:, None, :], (B_HEADS, CHUNK, span)).reshape(B_HEADS, CHUNK * span)
    skew = flat[:, CHUNK - 1:CHUNK - 1 + CHUNK * (span - 1)].reshape(B_HEADS, CHUNK, span - 1)
    bias_t = skew[:, :, 0:BAND].transpose(0, 2, 1)
    out_of_band = jnp.full((B_HEADS, CHUNK, CHUNK), NEG_INF, F32)
    bias2 = jnp.concatenate([jnp.concatenate([bias_t, out_of_band], axis=1),
                             jnp.concatenate([out_of_band, bias_t], axis=1)], axis=2).astype(F32)
    cur = pl.BlockSpec((1, t, width), lambda b, i: (b, i, 0))
    prev = pl.BlockSpec((1, t, width), lambda b, i: (b, jnp.maximum(i - 1, 0), 0))
    v_rows = bv_t.shape[0]
    cur_t = lambda r: pl.BlockSpec((r, t), lambda b, i: (0, b * nblk + i))
    prev_t = lambda r: pl.BlockSpec((r, t), lambda b, i: (0, b * nblk + jnp.maximum(i - 1, 0)))
    return pl.pallas_call(
        _chunk_attn_body, grid=(bn, nblk),
        in_specs=[cur_t(width), prev, cur, prev_t(v_rows), cur_t(v_rows), pl.BlockSpec(bias2.shape, lambda b, i: (0, 0, 0))],
        out_specs=cur, out_shape=jax.ShapeDtypeStruct((bn, seq, width), BF16),
        scratch_shapes=[pltpu.VMEM((2 * t, width), BF16), pltpu.VMEM((v_rows, 2 * t), BF16)],
        compiler_params=_params("parallel", "parallel"), name="chunk_attn",
    )(bq_t, bk, bk, bv_t, bv_t, bias2)


def _even_out_body(x_ref, ya_ref, yb_ref, w_ref, g_ref, b_ref, h_ref, ht_ref):
    mix = jnp.dot(ya_ref[...], w_ref[0:SEG, :], preferred_element_type=F32)
    mix = mix + jnp.dot(yb_ref[...], w_ref[SEG:, :], preferred_element_type=F32)
    h = _layer_norm(ALPHA * x_ref[...] + mix, g_ref[...], b_ref[...])
    h_ref[...] = h
    _store_token_tiles(ht_ref, h)


def _even_mixer(x2, bn, seq, rotary_tables, w_in, w_out, lam_p, subln_g, rel_bias, lam_init, ln_g, ln_b):
    aq_t, ak, av_t, bq_t, bk, bv_t = _even_proj(x2, w_in, rotary_tables, seq)
    lp = lam_p.astype(F32)
    lam = jnp.exp(jnp.sum(lp[0] * lp[1])) - jnp.exp(jnp.sum(lp[2] * lp[3])) + lam_init
    r3 = lambda a: a.reshape(bn, seq, SEG)
    ya = _diff_attn(aq_t, r3(ak), av_t, lam, subln_g, lam_init)
    yb = _chunk_attn(bq_t, r3(bk), bv_t, rel_bias)
    return _out_call(_even_out_body, [x2, ya.reshape(-1, SEG), yb.reshape(-1, SEG)],
                     [w_out.astype(BF16), ln_g.reshape(1, -1), ln_b.reshape(1, -1)], "even_out")


def _rwkv_pre_body(*refs, has_vres, blocks_per_seq):
    if has_vres:
        (x_ref, xp_ref, vf_ref, mix_ref, wr_ref, wk_ref, wv_ref, w0_ref, w1_ref, w2_ref, a0_ref, a1_ref, a2_ref,
         g1_ref, g2_ref, kk_ref, ka_ref, v0_ref, v1_ref, v2_ref, r_o, lw_o, k_o, v_o, kkr_o, a_o, g_o) = refs
    else:
        (x_ref, xp_ref, mix_ref, wr_ref, wk_ref, wv_ref, w0_ref, w1_ref, w2_ref, a0_ref, a1_ref, a2_ref,
         g1_ref, g2_ref, kk_ref, ka_ref, r_o, lw_o, k_o, v_o, kkr_o, a_o, g_o) = refs
    x = x_ref[...]
    first_of_seq = (pl.program_id(0) % blocks_per_seq) == 0
    before = jnp.where(first_of_seq, 0.0, xp_ref[SUBLANES - 1:SUBLANES, :])
    row = lax.broadcasted_iota(I32, x.shape, 0)
    xx = jnp.where(row == 0, before, pltpu.roll(x, 1, 0)) - x
    xr, xw, xk, xv, xa, xg = [x + xx * mix_ref[i:i + 1, :] for i in range(6)]
    r = _dot(xr, wr_ref[...])
    k = _dot(xk, wk_ref[...])
    v = _dot(xv, wv_ref[...])
    z = w0_ref[...] + _dot(jnp.tanh(_dot(xw, w1_ref[...])), w2_ref[...])
    softplus = jnp.maximum(-z, 0.0) + jnp.log(1.0 + jnp.exp(-jnp.abs(z)))
    lw_o[...] = -jnp.exp(-softplus - 0.5)
    if has_vres:
        v = v + (vf_ref[...].astype(F32) - v) * _sigmoid(v0_ref[...] + _dot(_dot(xv, v1_ref[...]), v2_ref[...]))
    a = _sigmoid(a0_ref[...] + _dot(_dot(xa, a1_ref[...]), a2_ref[...]))
    g_o[...] = _dot(_sigmoid(_dot(xg, g1_ref[...])), g2_ref[...]).astype(g_o.dtype)
    r_o[...] = r.astype(r_o.dtype)
    v_o[...] = v.astype(v_o.dtype)
    a_o[...] = a.astype(a_o.dtype)
    kkr_o[...] = (k * kk_ref[...]).astype(kkr_o.dtype)
    k_o[...] = (k * (1.0 + (a - 1.0) * ka_ref[...])).astype(k_o.dtype)


def _head_sum(x, first):
    zero = jnp.zeros_like(x)
    s0 = jnp.sum(jnp.where(first, x, zero), axis=1, keepdims=True)
    s1 = jnp.sum(jnp.where(first, zero, x), axis=1, keepdims=True)
    return jnp.where(first, s0, s1)


def _wkv_body(r_ref, lw_ref, k_ref, v_ref, kk_ref, a_ref, rk_ref, lg_ref, lb_ref, y_ref, z_s, *bufs):
    step = pl.program_id(2)
    half = len(bufs) // 2

    @pl.when(step == 0)
    def _():
        z_s[...] = jnp.zeros(z_s.shape, F32)
        for buf in bufs[half:]:
            buf[...] = jnp.zeros(buf.shape, buf.dtype)

    @pl.when(step % 2 == 0)
    def _():
        _wkv_step(r_ref, lw_ref, k_ref, v_ref, kk_ref, a_ref, rk_ref, lg_ref, lb_ref, y_ref, z_s,
                  bufs[:half], bufs[half:])

    @pl.when(step % 2 == 1)
    def _():
        _wkv_step(r_ref, lw_ref, k_ref, v_ref, kk_ref, a_ref, rk_ref, lg_ref, lb_ref, y_ref, z_s,
                  bufs[half:], bufs[:half])


def _wkv_step(r_ref, lw_ref, k_ref, v_ref, kk_ref, a_ref, rk_ref, lg_ref, lb_ref, y_ref, z_s, fill, drain):
    n = CHUNK
    lhs_f, y0_f, sadd_f, decay_f, bonus_f = fill
    lhs_d, y0_d, sadd_d, decay_d, bonus_d = drain
    first = lax.broadcasted_iota(I32, (n, LANES), 1) < C_DIM
    row = lax.broadcasted_iota(I32, (2 * n, 2 * n), 0)
    col = lax.broadcasted_iota(I32, (2 * n, 2 * n), 1)
    same = (row // n) == (col // n)
    strict = same & ((col % n) < (row % n))
    incl = same & ((col % n) <= (row % n))
    eye = (row == col).astype(F32)
    tri = (lax.broadcasted_iota(I32, (n, n), 1) <= lax.broadcasted_iota(I32, (n, n), 0)).astype(BF16)

    def stack(x):
        zero = jnp.zeros_like(x)
        return jnp.concatenate([jnp.where(first, x, zero), jnp.where(first, zero, x)], axis=0)

    zero = jnp.zeros((2 * n, 2 * n), F32)
    chunks = range(WKV_ROWS // n)
    prep = []
    z = z_s[...]
    y2 = []
    for c in chunks:
        both = jnp.dot(lhs_d[c], z.astype(BF16), preferred_element_type=F32)
        y2.append(both[0:2 * n] + y0_d[c])
        z = decay_d[c] * z + both[2 * n:] + sadd_d[c]
        rows = slice(c * n, (c + 1) * n)
        r, lw, k, v, kk, a = (ref[0, rows, :].astype(F32) for ref in (r_ref, lw_ref, k_ref, v_ref, kk_ref, a_ref))
        kk = kk / jnp.maximum(jnp.sqrt(_head_sum(kk * kk, first)), 1e-12)
        hi = lw.astype(BF16)
        rem = lw - hi.astype(F32)
        mid = rem.astype(BF16)
        low = (rem - mid.astype(F32)).astype(BF16)
        cum = (jnp.dot(tri, hi, preferred_element_type=F32) + jnp.dot(tri, mid, preferred_element_type=F32)
               + jnp.dot(tri, low, preferred_element_type=F32))
        total = cum[n - 1:n, :]
        g_inv = jnp.exp(-cum)
        g_tail = jnp.exp(total - cum)
        a2 = stack(-kk * jnp.exp(cum - lw))
        r2 = stack(r * jnp.exp(cum))
        b2 = stack(kk * a * g_inv)
        k2 = stack(k * g_inv)
        m = lax.dot_general(jnp.concatenate([a2, r2], axis=0).astype(BF16),
                            jnp.concatenate([b2, k2], axis=0).astype(BF16), _NT, preferred_element_type=F32)
        prep.append(dict(
            a2=a2, r2=r2, v2=stack(v), total=total, bonus=_head_sum(r * k * rk_ref[...], first) * v,
            tails_t=jnp.concatenate([stack(kk * a * g_tail), stack(k * g_tail)], axis=0).T.astype(BF16),
            a_ab=jnp.where(strict, m[0:2 * n, 0:2 * n], zero), a_ak=jnp.where(strict, m[0:2 * n, 2 * n:], zero),
            a_r=jnp.concatenate([jnp.where(incl, m[2 * n:, 0:2 * n], zero),
                                 jnp.where(incl, m[2 * n:, 2 * n:], zero)], axis=1).astype(BF16)))
    inv = [eye + p["a_ab"] for p in prep]
    power = [p["a_ab"].astype(BF16) for p in prep]
    power = [jnp.dot(pw, pw, preferred_element_type=F32).astype(BF16) for pw in power]
    for _ in range(int(math.log2(n)) - 2):
        both = [jnp.dot(jnp.concatenate([pw, iv.astype(BF16)], axis=0), pw, preferred_element_type=F32)
                for pw, iv in zip(power, inv)]
        power = [b[0:2 * n].astype(BF16) for b in both]
        inv = [iv + b[2 * n:] for iv, b in zip(inv, both)]
    inv = [iv + jnp.dot(iv.astype(BF16), pw, preferred_element_type=F32) for iv, pw in zip(inv, power)]
    akv = [_dot(p["a_ak"], p["v2"]) for p in prep]
    wu = [_dot(iv, jnp.concatenate([p["a2"], x], axis=1)) for iv, p, x in zip(inv, prep, akv)]
    rhs = [jnp.concatenate([w, jnp.concatenate([zero, p["v2"]], axis=1)], axis=0).astype(BF16)
           for w, p in zip(wu, prep)]
    ryqs = [jnp.dot(jnp.concatenate([p["a_r"], p["tails_t"]], axis=0), x, preferred_element_type=F32)
            for p, x in zip(prep, rhs)]
    for c in chunks:
        rows = slice(c * n, (c + 1) * n)
        p, x = prep[c], ryqs[c]
        lhs_f[c] = jnp.concatenate([p["r2"] + x[0:2 * n, 0:2 * n], x[2 * n:, 0:2 * n]], axis=0).astype(BF16)
        y0_f[c] = x[0:2 * n, 2 * n:]
        sadd_f[c] = x[2 * n:, 2 * n:]
        decay_f[c] = jnp.broadcast_to(jnp.sum(eye * jnp.exp(p["total"]), axis=1, keepdims=True), (2 * n, 2 * n))
        bonus_f[rows, :] = p["bonus"]
    z_s[...] = z
    for c in chunks:
        rows = slice(c * n, (c + 1) * n)
        y = y2[c][0:n] + y2[c][n:]
        mu = _head_sum(y, first) * (1.0 / C_DIM)
        yc = y - mu
        var = _head_sum(yc * yc, first) * (1.0 / C_DIM)
        y_ref[0, rows, :] = (yc * lax.rsqrt(var + LNX_EPS) * lg_ref[...] + lb_ref[...] + bonus_d[rows, :]).astype(y_ref.dtype)


def _wkv(r, lw, k, v, kkr, a, r_k, lnx_g, lnx_b):
    bn, seq, d = r.shape
    nblk, nc, n2 = seq // WKV_ROWS, WKV_ROWS // CHUNK, 2 * CHUNK
    blk = pl.BlockSpec((1, WKV_ROWS, LANES), lambda b, h, c: (b, jnp.minimum(c, nblk - 1), h))
    out = pl.BlockSpec((1, WKV_ROWS, LANES), lambda b, h, c: (b, jnp.maximum(c - 1, 0), h))
    par = pl.BlockSpec((1, LANES), lambda b, h, c: (0, h))
    buffers = [pltpu.VMEM((nc, 2 * n2, n2), BF16), pltpu.VMEM((nc, n2, n2), F32), pltpu.VMEM((nc, n2, n2), F32),
               pltpu.VMEM((nc, n2, n2), F32), pltpu.VMEM((WKV_ROWS, LANES), F32)]
    return pl.pallas_call(
        _wkv_body, grid=(bn, d // LANES, nblk + 1),
        in_specs=[blk] * 6 + [par] * 3, out_specs=out,
        out_shape=jax.ShapeDtypeStruct((bn, seq, d), BF16),
        scratch_shapes=[pltpu.VMEM((LANES, LANES), F32)] + buffers + buffers,
        compiler_params=_params("parallel", "parallel", "arbitrary"), name="wkv7",
    )(r, lw, k, v, kkr, a, r_k.reshape(1, d), lnx_g.reshape(1, d), lnx_b.reshape(1, d))


def _odd_out_body(x_ref, y_ref, g_ref, w_ref, lg_ref, lb_ref, h_ref, ht_ref):
    mix = _dot(y_ref[...].astype(F32) * g_ref[...].astype(F32), w_ref[...])
    h = _layer_norm(ALPHA * x_ref[...] + mix, lg_ref[...], lb_ref[...])
    h_ref[...] = h
    _store_token_tiles(ht_ref, h)


def _rwkv_mixer(x2, bn, seq, v_first, mix, w_rkv, w0, w1, w2, a0, a1, a2, g1, g2, k_k, k_a, r_k, lnx_g, lnx_b,
                w_out, vres, ln_g, ln_b):
    d = x2.shape[1]
    row = lambda p: p.reshape(1, -1)
    bf = lambda p: p.astype(BF16)
    groups = ROW_TILE // SUBLANES
    rows = [x2, (x2, lambda i: jnp.maximum(i * groups - 1, 0), SUBLANES)] + ([v_first] if vres is not None else [])
    fulls = [mix, bf(w_rkv[0]), bf(w_rkv[1]), bf(w_rkv[2]), row(w0), bf(w1), bf(w2), row(a0), bf(a1), bf(a2),
             bf(g1), bf(g2), row(k_k), row(k_a)]
    if vres is not None:
        fulls += [row(vres[0]), bf(vres[1]), bf(vres[2])]
    body = functools.partial(_rwkv_pre_body, has_vres=vres is not None, blocks_per_seq=seq // ROW_TILE)
    outs = [(d, BF16), (d, F32)] + [(d, BF16)] * 5
    r, lw, k, v, kkr, a, g = _row_call(body, rows, fulls, outs, ROW_TILE, "rwkv_pre")
    if vres is None:
        v_first = v
    r3 = lambda t: t.reshape(bn, seq, d)
    y = _wkv(r3(r), r3(lw), r3(k), r3(v), r3(kkr), r3(a), r_k, lnx_g, lnx_b)
    h, ht = _out_call(_odd_out_body, [x2, y.reshape(-1, d), g], [bf(w_out), row(ln_g), row(ln_b)], "odd_out")
    return h, ht, v_first


def _router_body(h_ref, wt_ref, b_ref, upper_ref, idx_ref, gate_ref, rank_ref, cnt_ref, carry_s):
    @pl.when(pl.program_id(0) == 0)
    def _():
        carry_s[...] = jnp.zeros(carry_s.shape, F32)

    logits = lax.dot_general(wt_ref[...], h_ref[...], _NT, precision=lax.Precision.HIGHEST,
                             preferred_element_type=F32)
    scores = _sigmoid(logits)
    sel = scores + b_ref[...]
    sub = lax.broadcasted_iota(I32, sel.shape, 0)
    grp = sub // EXPERTS_PER_GROUP
    ninf = jnp.full(sel.shape, -jnp.inf, F32)
    big = jnp.full(sel.shape, N_EXPERTS, I32)

    def top2(vals):
        m1 = jnp.max(vals, axis=0, keepdims=True)
        i1 = jnp.min(jnp.where(vals == m1, sub, big), axis=0, keepdims=True)
        rest = jnp.where(sub == i1, ninf, vals)
        m2 = jnp.max(rest, axis=0, keepdims=True)
        i2 = jnp.min(jnp.where(rest == m2, sub, big), axis=0, keepdims=True)
        return m1, i1, m2, i2

    best = jnp.zeros((1, sel.shape[1]), I32)
    best_score = None
    for g in range(N_GROUPS):
        m1, _, m2, _ = top2(jnp.where(grp == g, sel, ninf))
        score = m1 + m2
        if g == 0:
            best_score = score
        else:
            better = score > best_score
            best = jnp.where(better, g, best)
            best_score = jnp.where(better, score, best_score)
    _, i1, _, i2 = top2(jnp.where(grp == best, sel, ninf))
    oh1, oh2 = sub == i1, sub == i2
    zero = jnp.zeros_like(scores)
    g1 = jnp.sum(jnp.where(oh1, scores, zero), axis=0, keepdims=True)
    g2 = jnp.sum(jnp.where(oh2, scores, zero), axis=0, keepdims=True)
    den = g1 + g2
    idx_ref[...] = jnp.concatenate([i1, i2], axis=0)
    gate_ref[...] = jnp.concatenate([g1 / den, g2 / den], axis=0)
    both = (oh1 | oh2).astype(BF16)
    before = jnp.dot(both, upper_ref[...], preferred_element_type=F32) + carry_s[...]
    rank1 = jnp.sum(jnp.where(oh1, before, zero), axis=0, keepdims=True)
    rank2 = jnp.sum(jnp.where(oh2, before, zero), axis=0, keepdims=True)
    rank_ref[...] = jnp.concatenate([rank1, rank2], axis=0).astype(I32)
    carry_s[...] = carry_s[...] + jnp.sum(both.astype(F32), axis=1, keepdims=True)
    cnt_ref[...] = carry_s[...]


def _router(h, router_w, router_b):
    n, d = h.shape
    tm = ROUTE_TILE
    upper = (jnp.arange(tm)[:, None] < jnp.arange(tm)[None, :]).astype(BF16)
    tok = pl.BlockSpec((2, tm), lambda i: (0, i))
    return pl.pallas_call(
        _router_body, grid=(n // tm,),
        in_specs=[pl.BlockSpec((tm, d), lambda i: (i, 0)), pl.BlockSpec((N_EXPERTS, d), lambda i: (0, 0)),
                  pl.BlockSpec((N_EXPERTS, 1), lambda i: (0, 0)), pl.BlockSpec((tm, tm), lambda i: (0, 0))],
        out_specs=[tok, tok, tok, pl.BlockSpec((N_EXPERTS, 1), lambda i: (0, 0))],
        out_shape=[jax.ShapeDtypeStruct((2, n), I32), jax.ShapeDtypeStruct((2, n), F32),
                   jax.ShapeDtypeStruct((2, n), I32), jax.ShapeDtypeStruct((N_EXPERTS, 1), F32)],
        scratch_shapes=[pltpu.VMEM((N_EXPERTS, 1), F32)],
        compiler_params=_params("arbitrary"), name="router",
    )(h, router_w.T.astype(F32), router_b.reshape(N_EXPERTS, 1).astype(F32), upper)


def _load_token_tiles(ref, first_row, tokens):
    chunks = [ref[pl.ds(first_row + j, tokens, stride=TOKEN_CHUNKS), :] for j in range(TOKEN_CHUNKS)]
    return jnp.concatenate(chunks, axis=1)


def _store_token_tiles(ref, val):
    tokens = val.shape[0]
    for j in range(TOKEN_CHUNKS):
        ref[pl.ds(j, tokens, stride=TOKEN_CHUNKS), :] = val[:, j * LANES:(j + 1) * LANES].astype(ref.dtype)


def _tile_copy(src, dst, src_tok, dst_tok, sem):
    src_row = pl.multiple_of(src_tok * TOKEN_CHUNKS, TOKEN_CHUNKS)
    dst_row = pl.multiple_of(dst_tok * TOKEN_CHUNKS, TOKEN_CHUNKS)
    return pltpu.make_async_copy(src.at[pl.ds(src_row, TOKEN_CHUNKS), :], dst.at[pl.ds(dst_row, TOKEN_CHUNKS), :], sem)


def _dispatch_body(pad_end_ref, padded_ref, dest_ref, ht_ref, xb_ref, zero_s, sem):
    block_rows = MOE_TILE * TOKEN_CHUNKS
    n_blocks = xb_ref.shape[0] // block_rows

    def block_copy(b):
        return pltpu.make_async_copy(zero_s, xb_ref.at[pl.ds(pl.multiple_of(b * block_rows, block_rows), block_rows), :], sem)

    @pl.when(pl.program_id(0) == 0)
    def _():
        zero_s[...] = jnp.zeros(zero_s.shape, zero_s.dtype)

        def start(e, carry):
            @pl.when(padded_ref[e] > 0)
            def _():
                block_copy(pad_end_ref[e] // MOE_TILE - 1).start()
            return carry

        def wait(e, carry):
            @pl.when(padded_ref[e] > 0)
            def _():
                block_copy(0).wait()
            return carry

        lax.fori_loop(0, N_EXPERTS, start, 0)
        lax.fori_loop(0, N_EXPERTS, wait, 0)
        first_unused = pad_end_ref[N_EXPERTS - 1] // MOE_TILE
        lax.fori_loop(first_unused, n_blocks, lambda b, c: (block_copy(b).start(), c)[1], 0)
        lax.fori_loop(first_unused, n_blocks, lambda b, c: (block_copy(0).wait(), c)[1], 0)

    def start_token(t, carry):
        _tile_copy(ht_ref, xb_ref, t, dest_ref[0, 0, t], sem).start()
        _tile_copy(ht_ref, xb_ref, t, dest_ref[0, 0, GATHER_TILE + t], sem).start()
        return carry

    def wait_row(t, carry):
        _tile_copy(ht_ref, xb_ref, 0, 0, sem).wait()
        return carry

    lax.fori_loop(0, GATHER_TILE, start_token, 0, unroll=8)
    lax.fori_loop(0, 2 * GATHER_TILE, wait_row, 0, unroll=8)


def _dispatch(ht, dest_blocks, pad_end, padded, cap):
    n = ht.shape[0] // TOKEN_CHUNKS
    grid_spec = pltpu.PrefetchScalarGridSpec(
        num_scalar_prefetch=2, grid=(n // GATHER_TILE,),
        in_specs=[pl.BlockSpec((1, 1, 2 * GATHER_TILE), lambda i, pe, pd: (i, 0, 0), memory_space=pltpu.SMEM),
                  pl.BlockSpec((GATHER_TILE * TOKEN_CHUNKS, LANES), lambda i, pe, pd: (i, 0))],
        out_specs=pl.BlockSpec(memory_space=pl.ANY),
        scratch_shapes=[pltpu.VMEM((MOE_TILE * TOKEN_CHUNKS, LANES), F32), pltpu.SemaphoreType.DMA(())])
    return pl.pallas_call(
        _dispatch_body, grid_spec=grid_spec, out_shape=jax.ShapeDtypeStruct((cap * TOKEN_CHUNKS, LANES), F32),
        compiler_params=_params("arbitrary"), name="moe_dispatch",
    )(pad_end, padded, dest_blocks, ht)


def _expert_body(blk_ref, exp_ref, used_ref, x_ref, wg_ref, wu_ref, wd_ref, y_ref, wg_s, wu_s, wd_s):
    i = pl.program_id(0)

    @pl.when((i == 0) | (exp_ref[i] != exp_ref[jnp.maximum(i - 1, 0)]))
    def _():
        wg_s[...] = wg_ref[0].astype(BF16)
        wu_s[...] = wu_ref[0].astype(BF16)
        wd_s[...] = wd_ref[0].astype(BF16)

    @pl.when(i < used_ref[0])
    def _():
        x = _load_token_tiles(x_ref, 0, MOE_TILE).astype(BF16)
        gate = jnp.dot(x, wg_s[...], preferred_element_type=F32)
        up = jnp.dot(x, wu_s[...], preferred_element_type=F32)
        u = gate * _sigmoid(gate) * up
        _store_token_tiles(y_ref, jnp.dot(u.astype(BF16), wd_s[...], preferred_element_type=F32))

    @pl.when(i >= used_ref[0])
    def _():
        y_ref[...] = jnp.zeros(y_ref.shape, y_ref.dtype)


def _experts(xb, blk, blk_expert, n_used, layer, w_gate, w_up, w_down):
    d = D_MODEL
    rows = pl.BlockSpec((MOE_TILE * TOKEN_CHUNKS, LANES), lambda i, blk, ex, used: (blk[i], 0))
    out_rows = pl.BlockSpec((MOE_TILE * TOKEN_CHUNKS, LANES), lambda i, blk, ex, used: (i, 0))
    grid_spec = pltpu.PrefetchScalarGridSpec(
        num_scalar_prefetch=3, grid=(xb.shape[0] // (MOE_TILE * TOKEN_CHUNKS),),
        in_specs=[rows,
                  pl.BlockSpec((None, 1, d, D_EXPERT), lambda i, blk, ex, used: (layer, ex[i], 0, 0)),
                  pl.BlockSpec((None, 1, d, D_EXPERT), lambda i, blk, ex, used: (layer, ex[i], 0, 0)),
                  pl.BlockSpec((None, 1, D_EXPERT, d), lambda i, blk, ex, used: (layer, ex[i], 0, 0))],
        out_specs=out_rows,
        scratch_shapes=[pltpu.VMEM((d, D_EXPERT), BF16), pltpu.VMEM((d, D_EXPERT), BF16),
                        pltpu.VMEM((D_EXPERT, d), BF16)])
    return pl.pallas_call(
        _expert_body, grid_spec=grid_spec, out_shape=jax.ShapeDtypeStruct(xb.shape, F32),
        compiler_params=_params("arbitrary"), name="moe_experts",
    )(blk, blk_expert, n_used, xb, w_gate, w_up, w_down)


def _combine_body(dest_ref, next_ref, h_ref, gate_ref, lg_ref, lb_ref, yb_ref, o_ref, rows_a, rows_b, sem_a, sem_b):
    step, last = pl.program_id(0), pl.num_programs(0) - 1

    def gather(idx_ref, rows, sem):
        def start_row(t, carry):
            _tile_copy(yb_ref, rows, idx_ref[0, 0, t], t, sem).start()
            return carry
        lax.fori_loop(0, 2 * GATHER_TILE, start_row, 0, unroll=8)

    def finish(rows, sem):
        def wait_row(t, carry):
            _tile_copy(yb_ref, rows, 0, 0, sem).wait()
            return carry
        lax.fori_loop(0, 2 * GATHER_TILE, wait_row, 0, unroll=8)
        gate = gate_ref[...]
        ffn = (gate[:, 0:1] * _load_token_tiles(rows, 0, GATHER_TILE)
               + gate[:, 1:2] * _load_token_tiles(rows, GATHER_TILE * TOKEN_CHUNKS, GATHER_TILE))
        o_ref[...] = _layer_norm(ALPHA * h_ref[...] + ffn, lg_ref[...], lb_ref[...])

    @pl.when(step == 0)
    def _():
        gather(dest_ref, rows_a, sem_a)

    for parity, (rows, sem, rows_next, sem_next) in enumerate(((rows_a, sem_a, rows_b, sem_b),
                                                               (rows_b, sem_b, rows_a, sem_a))):
        @pl.when(step % 2 == parity)
        def _(rows=rows, sem=sem, rows_next=rows_next, sem_next=sem_next):
            @pl.when(step < last)
            def _():
                gather(next_ref, rows_next, sem_next)
            finish(rows, sem)


def _combine(h, gate_rows, dest_blocks, yb, ln_g, ln_b):
    n, d = h.shape
    tm = GATHER_TILE
    steps = n // tm
    rows = pltpu.VMEM((2 * tm * TOKEN_CHUNKS, LANES), F32)
    return pl.pallas_call(
        _combine_body, grid=(steps,),
        in_specs=[pl.BlockSpec((1, 1, 2 * tm), lambda i: (i, 0, 0), memory_space=pltpu.SMEM),
                  pl.BlockSpec((1, 1, 2 * tm), lambda i: (jnp.minimum(i + 1, steps - 1), 0, 0), memory_space=pltpu.SMEM),
                  pl.BlockSpec((tm, d), lambda i: (i, 0)), pl.BlockSpec((tm, 2), lambda i: (i, 0)),
                  pl.BlockSpec((1, d), lambda i: (0, 0)), pl.BlockSpec((1, d), lambda i: (0, 0)),
                  pl.BlockSpec(memory_space=pl.ANY)],
        out_specs=pl.BlockSpec((tm, d), lambda i: (i, 0)), out_shape=jax.ShapeDtypeStruct((n, d), F32),
        scratch_shapes=[rows, rows, pltpu.SemaphoreType.DMA(()), pltpu.SemaphoreType.DMA(())],
        compiler_params=_params("arbitrary"), name="moe_combine",
    )(dest_blocks, dest_blocks, h, gate_rows, ln_g.reshape(1, d), ln_b.reshape(1, d), yb)


def _moe_layer(h, ht, router_w, router_b, layer, w_gate, w_up, w_down, ln_g, ln_b):
    n, d = h.shape
    idx, gate, rank, cnt = _router(h, router_w, router_b)
    counts = cnt[:, 0].astype(I32)
    padded = (counts + MOE_TILE - 1) // MOE_TILE * MOE_TILE
    pad_end = jnp.cumsum(padded).astype(I32)
    pad_start = pad_end - padded
    experts = jnp.arange(N_EXPERTS, dtype=I32)
    dest = jnp.sum(jnp.where(idx[..., None] == experts, pad_start, 0), axis=-1) + rank
    cap = 2 * n + N_EXPERTS * MOE_TILE
    n_blocks = cap // MOE_TILE
    n_used = pad_end[-1] // MOE_TILE
    blk = jnp.minimum(jnp.arange(n_blocks, dtype=I32), n_used - 1)
    blk_expert = jnp.minimum(jnp.sum(pad_end[None, :] <= (blk * MOE_TILE)[:, None], axis=1), N_EXPERTS - 1).astype(I32)
    tm = GATHER_TILE
    dest_blocks = dest.reshape(2, n // tm, tm).transpose(1, 0, 2).reshape(n // tm, 1, 2 * tm)
    xb = _dispatch(ht, dest_blocks, pad_end, padded, cap)
    yb = _experts(xb, blk, blk_expert, n_used.reshape(1).astype(I32), layer, w_gate, w_up, w_down)
    return _combine(h, gate.T, dest_blocks, yb, ln_g, ln_b)


def kernel(x, ln_g, ln_b, even_w_in, even_w_out, even_lambda, even_subln_g, even_rel_bias, odd_mix, odd_w_rkv, odd_w0, odd_w1, odd_w2, odd_a0, odd_a1, odd_a2, odd_g1, odd_g2, odd_k_k, odd_k_a, odd_r_k, odd_lnx_g, odd_lnx_b, odd_w_out, vres_v0, vres_v1, vres_v2, router_w, router_b, moe_w_gate, moe_w_up, moe_w_down):
    bn, seq, d = x.shape
    x2 = x.reshape(bn * seq, d)
    v_first = None
    rotary_tables = _rotary_tables(seq)
    for layer in range(DEPTH):
        if layer % 2 == 0:
            e = layer // 2
            lam_init = 0.8 - 0.6 * math.exp(-0.3 * layer)
            h, ht = _even_mixer(x2, bn, seq, rotary_tables, even_w_in[e], even_w_out[e], even_lambda[e],
                                even_subln_g[e], even_rel_bias[e], lam_init, ln_g[layer, 0], ln_b[layer, 0])
        else:
            o = layer // 2
            vres = None if o == 0 else (vres_v0[o - 1], vres_v1[o - 1], vres_v2[o - 1])
            h, ht, v_first = _rwkv_mixer(x2, bn, seq, v_first, odd_mix[o], odd_w_rkv[o], odd_w0[o], odd_w1[o], odd_w2[o],
                                     odd_a0[o], odd_a1[o], odd_a2[o], odd_g1[o], odd_g2[o], odd_k_k[o], odd_k_a[o],
                                     odd_r_k[o], odd_lnx_g[o], odd_lnx_b[o], odd_w_out[o], vres,
                                     ln_g[layer, 0], ln_b[layer, 0])
        x2 = _moe_layer(h, ht, router_w, router_b, layer, moe_w_gate, moe_w_up, moe_w_down,
                        ln_g[layer, 1], ln_b[layer, 1])
    return x2.reshape(bn, seq, d)
```

```python
import functools
import math

import jax
import jax.numpy as jnp
from jax import lax
from jax.experimental import pallas as pl
from jax.experimental.pallas import tpu as pltpu

F32 = jnp.float32
BF16 = jnp.bfloat16
I32 = jnp.int32

LANES = 128
SUBLANES = 8
VMEM_LIMIT_BYTES = 56 * 1024 * 1024

D_MODEL = 1024
DEPTH = 4
CHUNK = 64
A_HEADS = 4
A_DIM = 64
A_VDIM = 128
ROPE_DIM = 16
ROPE_THETA = 500000.0
SUBLN_EPS = 1e-5
B_HEADS = 8
B_DIM = 64
B_CHUNKS_BACK = 8
REL_CLIP = 128
BAND = (B_CHUNKS_BACK + 1) * CHUNK
SEG = 512
C_DIM = 64
LNX_EPS = 1e-5 * C_DIM
N_EXPERTS = 32
N_GROUPS = 4
EXPERTS_PER_GROUP = 8
D_EXPERT = 512
LN_EPS = 1e-5
ALPHA = (2 * DEPTH) ** 0.25
NEG_INF = -1e30

ROW_TILE = 512
ATT_TILE = 2048
ATT_KEYS = 256
V_ROWS = A_VDIM + 16
LOG2_E = math.log2(math.e)
Q_SCALE = A_DIM ** -0.5 * LOG2_E
BAND_TILE = B_CHUNKS_BACK * CHUNK
WKV_ROWS = 512
MOE_TILE = 512
GATHER_TILE = 256
TOKEN_CHUNKS = D_MODEL // LANES

_NT = (((1,), (1,)), ((), ()))
_TN = (((0,), (0,)), ((), ()))


def _params(*sem):
    return pltpu.CompilerParams(dimension_semantics=sem, vmem_limit_bytes=VMEM_LIMIT_BYTES)


def _dot(a, b):
    return jnp.dot(a.astype(BF16), b.astype(BF16), preferred_element_type=F32)


def _layer_norm(x, g, b):
    mu = jnp.mean(x, axis=-1, keepdims=True)
    xc = x - mu
    var = jnp.mean(xc * xc, axis=-1, keepdims=True)
    return xc * lax.rsqrt(var + LN_EPS) * g + b


def _sigmoid(x):
    return 1.0 / (1.0 + jnp.exp(-x))


def _row_call(body, rows, fulls, outs, tm, name):
    m = rows[0].shape[0] if not isinstance(rows[0], tuple) else None
    in_specs, args = [], []
    for r in rows:
        if isinstance(r, tuple):
            arr, fn, block_rows = r
            in_specs.append(pl.BlockSpec((block_rows, arr.shape[1]), lambda i, fn=fn: (fn(i), 0)))
        else:
            arr = r
            m = arr.shape[0]
            in_specs.append(pl.BlockSpec((tm, arr.shape[1]), lambda i: (i, 0)))
        args.append(arr)
    for f in fulls:
        in_specs.append(pl.BlockSpec(f.shape, lambda i, nd=f.ndim: (0,) * nd))
        args.append(f)
    out_specs = [pl.BlockSpec((tm, w), lambda i: (i, 0)) for w, _ in outs]
    out_shape = [jax.ShapeDtypeStruct((m, w), dt) for w, dt in outs]
    return pl.pallas_call(
        body, grid=(m // tm,), in_specs=in_specs, out_specs=out_specs, out_shape=out_shape,
        compiler_params=_params("parallel"), name=name)(*args)


def _out_call(body, rows, fulls, router_w, router_b, name):
    n, tm = rows[0].shape[0], ROW_TILE
    upper = (jnp.arange(tm)[:, None] < jnp.arange(tm)[None, :]).astype(BF16)
    fulls = list(fulls) + [router_w.T.astype(F32), router_b.reshape(N_EXPERTS, 1).astype(F32), upper]
    in_specs = [pl.BlockSpec((tm, r.shape[1]), lambda i: (i, 0)) for r in rows]
    in_specs += [pl.BlockSpec(f.shape, lambda i, nd=f.ndim: (0,) * nd) for f in fulls]
    tok = pl.BlockSpec((2, tm), lambda i: (0, i))
    h, ht, *routing = pl.pallas_call(
        body, grid=(n // tm,), in_specs=in_specs,
        out_specs=[pl.BlockSpec((tm, D_MODEL), lambda i: (i, 0)), pl.BlockSpec((tm * TOKEN_CHUNKS, LANES), lambda i: (i, 0)),
                   tok, tok, tok, pl.BlockSpec((N_EXPERTS, 1), lambda i: (0, 0))],
        out_shape=[jax.ShapeDtypeStruct((n, D_MODEL), F32), jax.ShapeDtypeStruct((n * TOKEN_CHUNKS, LANES), F32),
                   jax.ShapeDtypeStruct((2, n), I32), jax.ShapeDtypeStruct((2, n), F32),
                   jax.ShapeDtypeStruct((2, n), I32), jax.ShapeDtypeStruct((N_EXPERTS, 1), F32)],
        scratch_shapes=[pltpu.VMEM((N_EXPERTS, 1), F32)],
        compiler_params=_params("arbitrary"), name=name)(*rows, *fulls)
    return h, ht, routing


def _even_proj_body(x_ref, c_ref, sm_ref, sp_ref, ct_ref, smt_ref, spt_ref, w_ref, wt_ref,
                    aq_ref, ak_ref, av_ref, bq_ref, bk_ref, bv_ref):
    x = x_ref[...]
    xb = x.astype(BF16)
    half = ROPE_DIM // 2

    def seg(j):
        return jnp.dot(xb, w_ref[:, j * SEG:(j + 1) * SEG], preferred_element_type=F32)

    qv_t = jnp.dot(wt_ref[...], x.T.astype(BF16), preferred_element_type=F32)
    cos, s_minus, s_plus = ct_ref[...], smt_ref[...], spt_ref[...]
    for j in range(SEG // LANES):
        u = qv_t[j * LANES:(j + 1) * LANES, :]
        rot = u * cos + pltpu.roll(u, LANES - half, 0) * s_minus + pltpu.roll(u, half, 0) * s_plus
        aq_ref[j * LANES:(j + 1) * LANES, :] = (rot * Q_SCALE).astype(aq_ref.dtype)
    for v_ref, base in ((av_ref, SEG), (bv_ref, 3 * SEG)):
        for h in range(SEG // LANES):
            v_ref[h * V_ROWS:h * V_ROWS + LANES, :] = qv_t[base + h * LANES:base + (h + 1) * LANES, :].astype(v_ref.dtype)
            v_ref[h * V_ROWS + LANES:(h + 1) * V_ROWS, :] = jnp.ones((V_ROWS - LANES, x.shape[0]), v_ref.dtype)
    bq_ref[...] = (qv_t[2 * SEG:3 * SEG, :] * (B_DIM ** -0.5 * LOG2_E)).astype(bq_ref.dtype)
    cos, s_minus, s_plus = c_ref[...], sm_ref[...], sp_ref[...]
    k = seg(1)
    for j in range(SEG // LANES):
        u = k[:, j * LANES:(j + 1) * LANES]
        rot = u * cos + pltpu.roll(u, LANES - half, 1) * s_minus + pltpu.roll(u, half, 1) * s_plus
        ak_ref[:, j * LANES:(j + 1) * LANES] = rot.astype(ak_ref.dtype)
    bk_ref[...] = seg(4).astype(bk_ref.dtype)


def _rotary_tables(seq):
    half = ROPE_DIM // 2
    inv = ROPE_THETA ** (-jnp.arange(half, dtype=F32) / half)
    ang = jnp.arange(seq, dtype=F32)[:, None] * inv
    cos, sin = jnp.cos(ang), jnp.sin(ang)
    d = jnp.arange(LANES) % A_DIM
    first, second = d < half, (d >= half) & (d < ROPE_DIM)
    col = jnp.where(first, d, d - half) % half
    cos_t = jnp.where(first | second, cos[:, col], 1.0)
    s_minus = jnp.where(first, -sin[:, col], 0.0)
    s_plus = jnp.where(second, sin[:, col], 0.0)
    return cos_t, s_minus, s_plus


def _even_proj(x2, w_in, tabs, seq):
    n, d = x2.shape
    tm = ROW_TILE
    blocks = seq // tm
    w = w_in.astype(BF16)
    seg = lambda j: w[:, j * SEG:(j + 1) * SEG]
    w_qv_t = jnp.concatenate([seg(0), seg(2), seg(3), seg(5)], axis=1).T
    tabs_t = [t.T for t in tabs]
    rows = pl.BlockSpec((tm, SEG), lambda i: (i, 0))
    cols = lambda r: pl.BlockSpec((r, tm), lambda i: (0, i))
    full = lambda a: pl.BlockSpec(a.shape, lambda i: (0, 0))
    v_rows = SEG // LANES * V_ROWS
    shapes = [(SEG, n), (n, SEG), (v_rows, n), (SEG, n), (n, SEG), (v_rows, n)]
    return pl.pallas_call(
        _even_proj_body, grid=(n // tm,),
        in_specs=[pl.BlockSpec((tm, d), lambda i: (i, 0))]
        + [pl.BlockSpec((tm, LANES), lambda i: (i % blocks, 0))] * 3
        + [pl.BlockSpec((LANES, tm), lambda i: (0, i % blocks))] * 3 + [full(w), full(w_qv_t)],
        out_specs=[cols(SEG), rows, cols(v_rows), cols(SEG), rows, cols(v_rows)],
        out_shape=[jax.ShapeDtypeStruct(s, BF16) for s in shapes],
        compiler_params=_params("parallel"), name="even_proj")(x2, *tabs, *tabs_t, w, w_qv_t)


def _diff_attn_body(qi_ref, kj_ref, lam_ref, q_ref, k_ref, v_ref, g_ref, o_ref, q2_s, m_s, l_s, acc_s, *, lam_init):
    p = pl.program_id(2)
    qi, kj = qi_ref[p], kj_ref[p]
    groups = ATT_TILE // LANES

    @pl.when(kj == 0)
    def _():
        feat = lax.broadcasted_iota(I32, (LANES, LANES), 0)
        for j in range(groups):
            q = q_ref[:, j * LANES:(j + 1) * LANES]
            zero = jnp.zeros_like(q)
            q2_s[j, :, 0:LANES] = jnp.where(feat < A_DIM, q, zero)
            q2_s[j, :, LANES:] = jnp.where(feat >= A_DIM, q, zero)
        m_s[...] = jnp.full(m_s.shape, -jnp.inf, F32)
        l_s[...] = jnp.zeros(l_s.shape, F32)
        acc_s[...] = jnp.zeros(acc_s.shape, F32)

    def step(diagonal):
        pieces = range(ATT_TILE // ATT_KEYS)

        def live(j, i):
            return not diagonal or i * ATT_KEYS // CHUNK <= ((j + 1) * LANES - 1) // CHUNK

        def scores(j, i):
            if not live(j, i):
                return None
            s = jnp.dot(k_ref[0, i * ATT_KEYS:(i + 1) * ATT_KEYS, :], q2_s[j], preferred_element_type=F32)
            if diagonal:
                key_chunk = (lax.broadcasted_iota(I32, s.shape, 0) + i * ATT_KEYS) // CHUNK
                query_chunk = (lax.broadcasted_iota(I32, s.shape, 1) % LANES + j * LANES) // CHUNK
                s = jnp.where(key_chunk <= query_chunk, s, NEG_INF)
            return s

        def stats(j, s):
            m = m_s[j]
            m_new = m
            for piece in s:
                if piece is not None:
                    m_new = jnp.maximum(m_new, jnp.max(piece, axis=0, keepdims=True))
            m_s[j] = m_new
            return jnp.exp2(m - m_new), m_new

        def values(pr, i):
            if pr[i] is None:
                return None
            return jnp.dot(v_ref[:, i * ATT_KEYS:(i + 1) * ATT_KEYS], pr[i], preferred_element_type=F32)

        def accumulate(j, alpha, pv):
            pv = functools.reduce(lambda a, b: a + b, [x for x in pv if x is not None])
            acc_s[j] = alpha * acc_s[j] + pv[0:A_VDIM]
            l_s[j] = alpha * l_s[j] + pv[A_VDIM:A_VDIM + 1]

        s = [scores(0, i) for i in pieces]
        alpha, m_new = stats(0, s)
        pending = None
        for j in range(groups):
            s_next, pr, pv = [], [], []
            for i in pieces:
                if j + 1 < groups:
                    s_next.append(scores(j + 1, i))
                pr.append(None if s[i] is None else jnp.exp2(s[i] - m_new).astype(BF16))
                if pending is not None:
                    pv.append(values(pending[2], i))
            if pending is not None:
                accumulate(pending[0], pending[1], pv)
            pending = (j, alpha, pr)
            if j + 1 < groups:
                s = s_next
                alpha, m_new = stats(j + 1, s)
        accumulate(pending[0], pending[1], [values(pending[2], i) for i in pieces])

    @pl.when(kj != qi)
    def _():
        step(False)

    @pl.when(kj == qi)
    def _():
        step(True)
        for j in range(groups):
            o = acc_s[j] / l_s[j]
            o = o[:, 0:LANES] - lam_ref[0] * o[:, LANES:]
            od = o.T
            od = od * lax.rsqrt(jnp.mean(od * od, axis=-1, keepdims=True) + SUBLN_EPS) * g_ref[...] * (1.0 - lam_init)
            o_ref[0, j * LANES:(j + 1) * LANES, :] = od.astype(o_ref.dtype)


def _diff_attn(aq_t, ak, av_t, lam, subln_g, lam_init):
    bn, seq, _ = ak.shape
    t = ATT_TILE
    nblk = seq // t
    pairs = [(i, j) for i in range(nblk) for j in range(i + 1)]
    qi = jnp.asarray([p[0] for p in pairs], I32)
    kj = jnp.asarray([p[1] for p in pairs], I32)
    groups, width = t // LANES, 2 * LANES
    grid_spec = pltpu.PrefetchScalarGridSpec(
        num_scalar_prefetch=3, grid=(bn, A_HEADS, len(pairs)),
        in_specs=[
            pl.BlockSpec((LANES, t), lambda b, h, p, qi, kj, lam: (h, b * nblk + qi[p])),
            pl.BlockSpec((1, t, LANES), lambda b, h, p, qi, kj, lam: (b, kj[p], h)),
            pl.BlockSpec((V_ROWS, t), lambda b, h, p, qi, kj, lam: (h, b * nblk + kj[p])),
            pl.BlockSpec((1, LANES), lambda b, h, p, qi, kj, lam: (0, 0)),
        ],
        out_specs=pl.BlockSpec((1, t, LANES), lambda b, h, p, qi, kj, lam: (b, qi[p], h)),
        scratch_shapes=[pltpu.VMEM((groups, LANES, width), BF16), pltpu.VMEM((groups, 1, width), F32),
                        pltpu.VMEM((groups, 1, width), F32), pltpu.VMEM((groups, A_VDIM, width), F32)])
    return pl.pallas_call(
        functools.partial(_diff_attn_body, lam_init=lam_init), grid_spec=grid_spec,
        out_shape=jax.ShapeDtypeStruct((bn, seq, A_HEADS * A_VDIM), BF16),
        compiler_params=_params("parallel", "parallel", "arbitrary"), name="diff_attn",
    )(qi, kj, lam.reshape(1).astype(F32), aq_t, ak, av_t, subln_g.reshape(1, A_VDIM))


def _chunk_attn_body(q_ref, kp_ref, kc_ref, vp_ref, vc_ref, bias_ref, o_ref, kcat, vcat):
    i = pl.program_id(1)
    t = BAND_TILE
    kcat[0:t, :] = kp_ref[0]
    kcat[t:, :] = kc_ref[0]
    vcat[:, 0:t] = vp_ref[...]
    vcat[:, t:] = vc_ref[...]
    pair = 2 * CHUNK
    keys = BAND + CHUNK
    feat = lax.broadcasted_iota(I32, (LANES, pair), 0)
    key = lax.broadcasted_iota(I32, (keys, pair), 0)
    items = [(cp, hp, hh) for cp in range(t // pair) for hp in range(B_HEADS // 2) for hh in range(2)]

    def scores(item):
        cp, hp, hh = item
        q = q_ref[hp * LANES:(hp + 1) * LANES, cp * pair:(cp + 1) * pair]
        q = jnp.where((feat < B_DIM) if hh == 0 else (feat >= B_DIM), q, jnp.zeros_like(q))
        s = jnp.dot(kcat[cp * pair:cp * pair + keys, hp * LANES:(hp + 1) * LANES], q, preferred_element_type=F32)
        s = s + bias_ref[2 * hp + hh]
        return jnp.where((i > 0) | (key + cp * pair >= t), s, NEG_INF)

    s_next = scores(items[0])
    halves = []
    for n, (cp, hp, hh) in enumerate(items):
        s = s_next
        if n + 1 < len(items):
            s_next = scores(items[n + 1])
        pr = jnp.exp2(s - jnp.max(s, axis=0, keepdims=True)).astype(BF16)
        pv = jnp.dot(vcat[hp * V_ROWS:(hp + 1) * V_ROWS, cp * pair:cp * pair + keys], pr, preferred_element_type=F32)
        halves.append(pv[hh * B_DIM:(hh + 1) * B_DIM] / pv[LANES:LANES + 1])
        if hh == 1:
            out = jnp.concatenate(halves, axis=0).T
            o_ref[0, cp * pair:(cp + 1) * pair, hp * LANES:(hp + 1) * LANES] = out.astype(o_ref.dtype)
            halves = []


def _chunk_attn(bq_t, bk, bv_t, rel_bias):
    bn, seq, width = bk.shape
    t = BAND_TILE
    nblk = seq // t
    far = jnp.broadcast_to(rel_bias[:, 2 * REL_CLIP:], (B_HEADS, BAND - REL_CLIP))
    f = jnp.concatenate([far, rel_bias[:, REL_CLIP - CHUNK + 1:2 * REL_CLIP][:, ::-1]], axis=1) * LOG2_E
    span = f.shape[1]
    flat = jnp.broadcast_to(f[:, None, :], (B_HEADS, CHUNK, span)).reshape(B_HEADS, CHUNK * span)
    skew = flat[:, CHUNK - 1:CHUNK - 1 + CHUNK * (span - 1)].reshape(B_HEADS, CHUNK, span - 1)
    bias_t = skew[:, :, 0:BAND].transpose(0, 2, 1)
    out_of_band = jnp.full((B_HEADS, CHUNK, CHUNK), NEG_INF, F32)
    bias2 = jnp.concatenate([jnp.concatenate([bias_t, out_of_band], axis=1),
                             jnp.concatenate([out_of_band, bias_t], axis=1)], axis=2).astype(F32)
    cur = pl.BlockSpec((1, t, width), lambda b, i: (b, i, 0))
    prev = pl.BlockSpec((1, t, width), lambda b, i: (b, jnp.maximum(i - 1, 0), 0))
    v_rows = bv_t.shape[0]
    cur_t = lambda r: pl.BlockSpec((r, t), lambda b, i: (0, b * nblk + i))
    prev_t = lambda r: pl.BlockSpec((r, t), lambda b, i: (0, b * nblk + jnp.maximum(i - 1, 0)))
    return pl.pallas_call(
        _chunk_attn_body, grid=(bn, nblk),
        in_specs=[cur_t(width), prev, cur, prev_t(v_rows), cur_t(v_rows), pl.BlockSpec(bias2.shape, lambda b, i: (0, 0, 0))],
        out_specs=cur, out_shape=jax.ShapeDtypeStruct((bn, seq, width), BF16),
        scratch_shapes=[pltpu.VMEM((2 * t, width), BF16), pltpu.VMEM((v_rows, 2 * t), BF16)],
        compiler_params=_params("parallel", "parallel"), name="chunk_attn",
    )(bq_t, bk, bk, bv_t, bv_t, bias2)


def _even_out_body(x_ref, ya_ref, yb_ref, w_ref, g_ref, b_ref, wt_ref, rb_ref, upper_ref, h_ref, ht_ref, *route_refs):
    mix = jnp.dot(ya_ref[...], w_ref[0:SEG, :], preferred_element_type=F32)
    mix = mix + jnp.dot(yb_ref[...], w_ref[SEG:, :], preferred_element_type=F32)
    h = _layer_norm(ALPHA * x_ref[...] + mix, g_ref[...], b_ref[...])
    h_ref[...] = h
    _store_token_tiles(ht_ref, h)
    _route(h, wt_ref, rb_ref, upper_ref, *route_refs)


def _even_mixer(x2, bn, seq, rotary_tables, w_in, w_out, lam_p, subln_g, rel_bias, lam_init, ln_g, ln_b,
                router_w, router_b):
    aq_t, ak, av_t, bq_t, bk, bv_t = _even_proj(x2, w_in, rotary_tables, seq)
    lp = lam_p.astype(F32)
    lam = jnp.exp(jnp.sum(lp[0] * lp[1])) - jnp.exp(jnp.sum(lp[2] * lp[3])) + lam_init
    r3 = lambda a: a.reshape(bn, seq, SEG)
    ya = _diff_attn(aq_t, r3(ak), av_t, lam, subln_g, lam_init)
    yb = _chunk_attn(bq_t, r3(bk), bv_t, rel_bias)
    return _out_call(_even_out_body, [x2, ya.reshape(-1, SEG), yb.reshape(-1, SEG)],
                     [w_out.astype(BF16), ln_g.reshape(1, -1), ln_b.reshape(1, -1)], router_w, router_b, "even_out")


def _rwkv_pre_body(*refs, has_vres, blocks_per_seq):
    if has_vres:
        (x_ref, xp_ref, vf_ref, mix_ref, wr_ref, wk_ref, wv_ref, w0_ref, w1_ref, w2_ref, a0_ref, a1_ref, a2_ref,
         g1_ref, g2_ref, kk_ref, ka_ref, v0_ref, v1_ref, v2_ref, r_o, lw_o, k_o, v_o, kkr_o, a_o, g_o) = refs
    else:
        (x_ref, xp_ref, mix_ref, wr_ref, wk_ref, wv_ref, w0_ref, w1_ref, w2_ref, a0_ref, a1_ref, a2_ref,
         g1_ref, g2_ref, kk_ref, ka_ref, r_o, lw_o, k_o, v_o, kkr_o, a_o, g_o) = refs
    x = x_ref[...]
    first_of_seq = (pl.program_id(0) % blocks_per_seq) == 0
    before = jnp.where(first_of_seq, 0.0, xp_ref[SUBLANES - 1:SUBLANES, :])
    row = lax.broadcasted_iota(I32, x.shape, 0)
    xx = jnp.where(row == 0, before, pltpu.roll(x, 1, 0)) - x
    xr, xw, xk, xv, xa, xg = [x + xx * mix_ref[i:i + 1, :] for i in range(6)]
    r = _dot(xr, wr_ref[...])
    k = _dot(xk, wk_ref[...])
    v = _dot(xv, wv_ref[...])
    z = w0_ref[...] + _dot(jnp.tanh(_dot(xw, w1_ref[...])), w2_ref[...])
    softplus = jnp.maximum(-z, 0.0) + jnp.log(1.0 + jnp.exp(-jnp.abs(z)))
    lw_o[...] = -jnp.exp(-softplus - 0.5)
    if has_vres:
        v = v + (vf_ref[...].astype(F32) - v) * _sigmoid(v0_ref[...] + _dot(_dot(xv, v1_ref[...]), v2_ref[...]))
    a = _sigmoid(a0_ref[...] + _dot(_dot(xa, a1_ref[...]), a2_ref[...]))
    g_o[...] = _dot(_sigmoid(_dot(xg, g1_ref[...])), g2_ref[...]).astype(g_o.dtype)
    r_o[...] = r.astype(r_o.dtype)
    v_o[...] = v.astype(v_o.dtype)
    a_o[...] = a.astype(a_o.dtype)
    kkr_o[...] = (k * kk_ref[...]).astype(kkr_o.dtype)
    k_o[...] = (k * (1.0 + (a - 1.0) * ka_ref[...])).astype(k_o.dtype)


def _head_sum(x, first):
    zero = jnp.zeros_like(x)
    s0 = jnp.sum(jnp.where(first, x, zero), axis=1, keepdims=True)
    s1 = jnp.sum(jnp.where(first, zero, x), axis=1, keepdims=True)
    return jnp.where(first, s0, s1)


def _wkv_body(r_ref, lw_ref, k_ref, v_ref, kk_ref, a_ref, rk_ref, lg_ref, lb_ref, y_ref, z_s, *bufs):
    step = pl.program_id(2)
    half = len(bufs) // 2

    @pl.when(step == 0)
    def _():
        z_s[...] = jnp.zeros(z_s.shape, F32)
        for buf in bufs[half:]:
            buf[...] = jnp.zeros(buf.shape, buf.dtype)

    @pl.when(step % 2 == 0)
    def _():
        _wkv_step(r_ref, lw_ref, k_ref, v_ref, kk_ref, a_ref, rk_ref, lg_ref, lb_ref, y_ref, z_s,
                  bufs[:half], bufs[half:])

    @pl.when(step % 2 == 1)
    def _():
        _wkv_step(r_ref, lw_ref, k_ref, v_ref, kk_ref, a_ref, rk_ref, lg_ref, lb_ref, y_ref, z_s,
                  bufs[half:], bufs[:half])


def _wkv_step(r_ref, lw_ref, k_ref, v_ref, kk_ref, a_ref, rk_ref, lg_ref, lb_ref, y_ref, z_s, fill, drain):
    n = CHUNK
    lhs_f, y0_f, sadd_f, decay_f, bonus_f = fill
    lhs_d, y0_d, sadd_d, decay_d, bonus_d = drain
    first = lax.broadcasted_iota(I32, (n, LANES), 1) < C_DIM
    row = lax.broadcasted_iota(I32, (2 * n, 2 * n), 0)
    col = lax.broadcasted_iota(I32, (2 * n, 2 * n), 1)
    same = (row // n) == (col // n)
    strict = same & ((col % n) < (row % n))
    incl = same & ((col % n) <= (row % n))
    eye = (row == col).astype(F32)
    tri = (lax.broadcasted_iota(I32, (n, n), 1) <= lax.broadcasted_iota(I32, (n, n), 0)).astype(BF16)

    def stack(x):
        zero = jnp.zeros_like(x)
        return jnp.concatenate([jnp.where(first, x, zero), jnp.where(first, zero, x)], axis=0)

    zero = jnp.zeros((2 * n, 2 * n), F32)
    chunks = range(WKV_ROWS // n)
    state = {"z": z_s[...]}
    y2 = []

    def pipeline(c):
        z = state["z"]
        both = jnp.dot(lhs_d[c], z.astype(BF16), preferred_element_type=F32)
        y2.append(both[0:2 * n] + y0_d[c])
        state["z"] = decay_d[c] * z + both[2 * n:] + sadd_d[c]
        rows = slice(c * n, (c + 1) * n)
        r, lw, k, v, kk, a = (ref[0, rows, :].astype(F32) for ref in (r_ref, lw_ref, k_ref, v_ref, kk_ref, a_ref))
        kk = kk / jnp.maximum(jnp.sqrt(_head_sum(kk * kk, first)), 1e-12)
        hi = lw.astype(BF16)
        rem = lw - hi.astype(F32)
        mid = rem.astype(BF16)
        low = (rem - mid.astype(F32)).astype(BF16)
        cum = (jnp.dot(tri, hi, preferred_element_type=F32) + jnp.dot(tri, mid, preferred_element_type=F32)
               + jnp.dot(tri, low, preferred_element_type=F32))
        total = cum[n - 1:n, :]
        g_inv = jnp.exp(-cum)
        g_tail = jnp.exp(total - cum)
        a2 = stack(-kk * jnp.exp(cum - lw))
        r2 = stack(r * jnp.exp(cum))
        b2 = stack(kk * a * g_inv)
        k2 = stack(k * g_inv)
        m = lax.dot_general(jnp.concatenate([a2, r2], axis=0).astype(BF16),
                            jnp.concatenate([b2, k2], axis=0).astype(BF16), _NT, preferred_element_type=F32)
        v2 = stack(v)
        bonus_f[rows, :] = _head_sum(r * k * rk_ref[...], first) * v
        decay_f[c] = jnp.broadcast_to(jnp.sum(eye * jnp.exp(total), axis=1, keepdims=True), (2 * n, 2 * n))
        tails_t = jnp.concatenate([stack(kk * a * g_tail), stack(k * g_tail)], axis=0).T.astype(BF16)
        a_ab = jnp.where(strict, m[0:2 * n, 0:2 * n], zero)
        a_ak = jnp.where(strict, m[0:2 * n, 2 * n:], zero)
        a_r = jnp.concatenate([jnp.where(incl, m[2 * n:, 0:2 * n], zero),
                               jnp.where(incl, m[2 * n:, 2 * n:], zero)], axis=1).astype(BF16)
        yield
        inv = eye + a_ab
        power = a_ab.astype(BF16)
        power = jnp.dot(power, power, preferred_element_type=F32).astype(BF16)
        akv = _dot(a_ak, v2)
        yield
        for _ in range(int(math.log2(n)) - 2):
            both = jnp.dot(jnp.concatenate([power, inv.astype(BF16)], axis=0), power, preferred_element_type=F32)
            power = both[0:2 * n].astype(BF16)
            inv = inv + both[2 * n:]
            yield
        inv = inv + jnp.dot(inv.astype(BF16), power, preferred_element_type=F32)
        yield
        wu = _dot(inv, jnp.concatenate([a2, akv], axis=1))
        yield
        rhs = jnp.concatenate([wu, jnp.concatenate([zero, v2], axis=1)], axis=0).astype(BF16)
        x = jnp.dot(jnp.concatenate([a_r, tails_t], axis=0), rhs, preferred_element_type=F32)
        yield
        lhs_f[c] = jnp.concatenate([r2 + x[0:2 * n, 0:2 * n], x[2 * n:, 0:2 * n]], axis=0).astype(BF16)
        y0_f[c] = x[0:2 * n, 2 * n:]
        sadd_f[c] = x[2 * n:, 2 * n:]

    running = [pipeline(c) for c in chunks]
    for _ in range(16):
        running = [g for g in running if next(g, "done") != "done"]
    assert not running
    z_s[...] = state["z"]
    for c in chunks:
        rows = slice(c * n, (c + 1) * n)
        y = y2[c][0:n] + y2[c][n:]
        mu = _head_sum(y, first) * (1.0 / C_DIM)
        yc = y - mu
        var = _head_sum(yc * yc, first) * (1.0 / C_DIM)
        y_ref[0, rows, :] = (yc * lax.rsqrt(var + LNX_EPS) * lg_ref[...] + lb_ref[...] + bonus_d[rows, :]).astype(y_ref.dtype)


def _wkv(r, lw, k, v, kkr, a, r_k, lnx_g, lnx_b):
    bn, seq, d = r.shape
    nblk, nc, n2 = seq // WKV_ROWS, WKV_ROWS // CHUNK, 2 * CHUNK
    blk = pl.BlockSpec((1, WKV_ROWS, LANES), lambda b, h, c: (b, jnp.minimum(c, nblk - 1), h))
    out = pl.BlockSpec((1, WKV_ROWS, LANES), lambda b, h, c: (b, jnp.maximum(c - 1, 0), h))
    par = pl.BlockSpec((1, LANES), lambda b, h, c: (0, h))
    buffers = [pltpu.VMEM((nc, 2 * n2, n2), BF16), pltpu.VMEM((nc, n2, n2), F32), pltpu.VMEM((nc, n2, n2), F32),
               pltpu.VMEM((nc, n2, n2), F32), pltpu.VMEM((WKV_ROWS, LANES), F32)]
    return pl.pallas_call(
        _wkv_body, grid=(bn, d // LANES, nblk + 1),
        in_specs=[blk] * 6 + [par] * 3, out_specs=out,
        out_shape=jax.ShapeDtypeStruct((bn, seq, d), BF16),
        scratch_shapes=[pltpu.VMEM((LANES, LANES), F32)] + buffers + buffers,
        compiler_params=_params("parallel", "parallel", "arbitrary"), name="wkv7",
    )(r, lw, k, v, kkr, a, r_k.reshape(1, d), lnx_g.reshape(1, d), lnx_b.reshape(1, d))


def _odd_out_body(x_ref, y_ref, g_ref, w_ref, lg_ref, lb_ref, wt_ref, rb_ref, upper_ref, h_ref, ht_ref, *route_refs):
    mix = _dot(y_ref[...].astype(F32) * g_ref[...].astype(F32), w_ref[...])
    h = _layer_norm(ALPHA * x_ref[...] + mix, lg_ref[...], lb_ref[...])
    h_ref[...] = h
    _store_token_tiles(ht_ref, h)
    _route(h, wt_ref, rb_ref, upper_ref, *route_refs)


def _rwkv_mixer(x2, bn, seq, v_first, mix, w_rkv, w0, w1, w2, a0, a1, a2, g1, g2, k_k, k_a, r_k, lnx_g, lnx_b,
                w_out, vres, ln_g, ln_b, router_w, router_b):
    d = x2.shape[1]
    row = lambda p: p.reshape(1, -1)
    bf = lambda p: p.astype(BF16)
    groups = ROW_TILE // SUBLANES
    rows = [x2, (x2, lambda i: jnp.maximum(i * groups - 1, 0), SUBLANES)] + ([v_first] if vres is not None else [])
    fulls = [mix, bf(w_rkv[0]), bf(w_rkv[1]), bf(w_rkv[2]), row(w0), bf(w1), bf(w2), row(a0), bf(a1), bf(a2),
             bf(g1), bf(g2), row(k_k), row(k_a)]
    if vres is not None:
        fulls += [row(vres[0]), bf(vres[1]), bf(vres[2])]
    body = functools.partial(_rwkv_pre_body, has_vres=vres is not None, blocks_per_seq=seq // ROW_TILE)
    outs = [(d, BF16), (d, F32)] + [(d, BF16)] * 5
    r, lw, k, v, kkr, a, g = _row_call(body, rows, fulls, outs, ROW_TILE, "rwkv_pre")
    if vres is None:
        v_first = v
    r3 = lambda t: t.reshape(bn, seq, d)
    y = _wkv(r3(r), r3(lw), r3(k), r3(v), r3(kkr), r3(a), r_k, lnx_g, lnx_b)
    h, ht, routing = _out_call(_odd_out_body, [x2, y.reshape(-1, d), g], [bf(w_out), row(ln_g), row(ln_b)],
                               router_w, router_b, "odd_out")
    return h, ht, routing, v_first


def _route(h, wt_ref, b_ref, upper_ref, idx_ref, gate_ref, rank_ref, cnt_ref, carry_s):
    @pl.when(pl.program_id(0) == 0)
    def _():
        carry_s[...] = jnp.zeros(carry_s.shape, F32)

    logits = lax.dot_general(wt_ref[...], h, _NT, precision=lax.Precision.HIGHEST,
                             preferred_element_type=F32)
    scores = _sigmoid(logits)
    sel = scores + b_ref[...]
    sub = lax.broadcasted_iota(I32, sel.shape, 0)
    grp = sub // EXPERTS_PER_GROUP
    ninf = jnp.full(sel.shape, -jnp.inf, F32)
    big = jnp.full(sel.shape, N_EXPERTS, I32)

    def top2(vals):
        m1 = jnp.max(vals, axis=0, keepdims=True)
        i1 = jnp.min(jnp.where(vals == m1, sub, big), axis=0, keepdims=True)
        rest = jnp.where(sub == i1, ninf, vals)
        m2 = jnp.max(rest, axis=0, keepdims=True)
        i2 = jnp.min(jnp.where(rest == m2, sub, big), axis=0, keepdims=True)
        return m1, i1, m2, i2

    best = jnp.zeros((1, sel.shape[1]), I32)
    best_score = None
    for g in range(N_GROUPS):
        m1, _, m2, _ = top2(jnp.where(grp == g, sel, ninf))
        score = m1 + m2
        if g == 0:
            best_score = score
        else:
            better = score > best_score
            best = jnp.where(better, g, best)
            best_score = jnp.where(better, score, best_score)
    _, i1, _, i2 = top2(jnp.where(grp == best, sel, ninf))
    oh1, oh2 = sub == i1, sub == i2
    zero = jnp.zeros_like(scores)
    g1 = jnp.sum(jnp.where(oh1, scores, zero), axis=0, keepdims=True)
    g2 = jnp.sum(jnp.where(oh2, scores, zero), axis=0, keepdims=True)
    den = g1 + g2
    idx_ref[...] = jnp.concatenate([i1, i2], axis=0)
    gate_ref[...] = jnp.concatenate([g1 / den, g2 / den], axis=0)
    both = (oh1 | oh2).astype(BF16)
    before = jnp.dot(both, upper_ref[...], preferred_element_type=F32) + carry_s[...]
    rank1 = jnp.sum(jnp.where(oh1, before, zero), axis=0, keepdims=True)
    rank2 = jnp.sum(jnp.where(oh2, before, zero), axis=0, keepdims=True)
    rank_ref[...] = jnp.concatenate([rank1, rank2], axis=0).astype(I32)
    carry_s[...] = carry_s[...] + jnp.sum(both.astype(F32), axis=1, keepdims=True)
    cnt_ref[...] = carry_s[...]


def _load_token_tiles(ref, first_row, tokens):
    chunks = [ref[pl.ds(first_row + j, tokens, stride=TOKEN_CHUNKS), :] for j in range(TOKEN_CHUNKS)]
    return jnp.concatenate(chunks, axis=1)


def _store_token_tiles(ref, val):
    tokens = val.shape[0]
    for j in range(TOKEN_CHUNKS):
        ref[pl.ds(j, tokens, stride=TOKEN_CHUNKS), :] = val[:, j * LANES:(j + 1) * LANES].astype(ref.dtype)


def _tile_copy(src, dst, src_tok, dst_tok, sem):
    src_row = pl.multiple_of(src_tok * TOKEN_CHUNKS, TOKEN_CHUNKS)
    dst_row = pl.multiple_of(dst_tok * TOKEN_CHUNKS, TOKEN_CHUNKS)
    return pltpu.make_async_copy(src.at[pl.ds(src_row, TOKEN_CHUNKS), :], dst.at[pl.ds(dst_row, TOKEN_CHUNKS), :], sem)


def _dispatch_body(pad_end_ref, padded_ref, dest_ref, ht_ref, xb_ref, zero_s, sem):
    block_rows = MOE_TILE * TOKEN_CHUNKS
    n_blocks = xb_ref.shape[0] // block_rows

    def block_copy(b):
        return pltpu.make_async_copy(zero_s, xb_ref.at[pl.ds(pl.multiple_of(b * block_rows, block_rows), block_rows), :], sem)

    @pl.when(pl.program_id(0) == 0)
    def _():
        zero_s[...] = jnp.zeros(zero_s.shape, zero_s.dtype)

        def start(e, carry):
            @pl.when(padded_ref[e] > 0)
            def _():
                block_copy(pad_end_ref[e] // MOE_TILE - 1).start()
            return carry

        def wait(e, carry):
            @pl.when(padded_ref[e] > 0)
            def _():
                block_copy(0).wait()
            return carry

        lax.fori_loop(0, N_EXPERTS, start, 0)
        lax.fori_loop(0, N_EXPERTS, wait, 0)
        first_unused = pad_end_ref[N_EXPERTS - 1] // MOE_TILE
        lax.fori_loop(first_unused, n_blocks, lambda b, c: (block_copy(b).start(), c)[1], 0)
        lax.fori_loop(first_unused, n_blocks, lambda b, c: (block_copy(0).wait(), c)[1], 0)

    def start_token(t, carry):
        _tile_copy(ht_ref, xb_ref, t, dest_ref[0, 0, t], sem).start()
        _tile_copy(ht_ref, xb_ref, t, dest_ref[0, 0, GATHER_TILE + t], sem).start()
        return carry

    def wait_row(t, carry):
        _tile_copy(ht_ref, xb_ref, 0, 0, sem).wait()
        return carry

    lax.fori_loop(0, GATHER_TILE, start_token, 0, unroll=8)
    lax.fori_loop(0, 2 * GATHER_TILE, wait_row, 0, unroll=8)


def _dispatch(ht, dest_blocks, pad_end, padded, cap):
    n = ht.shape[0] // TOKEN_CHUNKS
    grid_spec = pltpu.PrefetchScalarGridSpec(
        num_scalar_prefetch=2, grid=(n // GATHER_TILE,),
        in_specs=[pl.BlockSpec((1, 1, 2 * GATHER_TILE), lambda i, pe, pd: (i, 0, 0), memory_space=pltpu.SMEM),
                  pl.BlockSpec((GATHER_TILE * TOKEN_CHUNKS, LANES), lambda i, pe, pd: (i, 0))],
        out_specs=pl.BlockSpec(memory_space=pl.ANY),
        scratch_shapes=[pltpu.VMEM((MOE_TILE * TOKEN_CHUNKS, LANES), F32), pltpu.SemaphoreType.DMA(())])
    return pl.pallas_call(
        _dispatch_body, grid_spec=grid_spec, out_shape=jax.ShapeDtypeStruct((cap * TOKEN_CHUNKS, LANES), F32),
        compiler_params=_params("arbitrary"), name="moe_dispatch",
    )(pad_end, padded, dest_blocks, ht)


def _expert_body(blk_ref, exp_ref, used_ref, x_ref, wg_ref, wu_ref, wd_ref, y_ref, wg_s, wu_s, wd_s):
    i = pl.program_id(0)

    @pl.when((i == 0) | (exp_ref[i] != exp_ref[jnp.maximum(i - 1, 0)]))
    def _():
        wg_s[...] = wg_ref[0].astype(BF16)
        wu_s[...] = wu_ref[0].astype(BF16)
        wd_s[...] = wd_ref[0].astype(BF16)

    @pl.when(i < used_ref[0])
    def _():
        half = MOE_TILE // 2
        hidden = []
        for part in range(2):
            x = _load_token_tiles(x_ref, part * half * TOKEN_CHUNKS, half).astype(BF16)
            gate = jnp.dot(x, wg_s[...], preferred_element_type=F32)
            up = jnp.dot(x, wu_s[...], preferred_element_type=F32)
            hidden.append((gate, up))
        for part, (gate, up) in enumerate(hidden):
            u = gate * _sigmoid(gate) * up
            y = jnp.dot(u.astype(BF16), wd_s[...], preferred_element_type=F32)
            rows = y_ref.at[pl.ds(part * half * TOKEN_CHUNKS, half * TOKEN_CHUNKS), :]
            _store_token_tiles(rows, y)

    @pl.when(i >= used_ref[0])
    def _():
        y_ref[...] = jnp.zeros(y_ref.shape, y_ref.dtype)


def _experts(xb, blk, blk_expert, n_used, layer, w_gate, w_up, w_down):
    d = D_MODEL
    rows = pl.BlockSpec((MOE_TILE * TOKEN_CHUNKS, LANES), lambda i, blk, ex, used: (blk[i], 0))
    out_rows = pl.BlockSpec((MOE_TILE * TOKEN_CHUNKS, LANES), lambda i, blk, ex, used: (i, 0))
    grid_spec = pltpu.PrefetchScalarGridSpec(
        num_scalar_prefetch=3, grid=(xb.shape[0] // (MOE_TILE * TOKEN_CHUNKS),),
        in_specs=[rows,
                  pl.BlockSpec((None, 1, d, D_EXPERT), lambda i, blk, ex, used: (layer, ex[i], 0, 0)),
                  pl.BlockSpec((None, 1, d, D_EXPERT), lambda i, blk, ex, used: (layer, ex[i], 0, 0)),
                  pl.BlockSpec((None, 1, D_EXPERT, d), lambda i, blk, ex, used: (layer, ex[i], 0, 0))],
        out_specs=out_rows,
        scratch_shapes=[pltpu.VMEM((d, D_EXPERT), BF16), pltpu.VMEM((d, D_EXPERT), BF16),
                        pltpu.VMEM((D_EXPERT, d), BF16)])
    return pl.pallas_call(
        _expert_body, grid_spec=grid_spec, out_shape=jax.ShapeDtypeStruct(xb.shape, F32),
        compiler_params=_params("arbitrary"), name="moe_experts",
    )(blk, blk_expert, n_used, xb, w_gate, w_up, w_down)


def _combine_body(dest_ref, next_ref, h_ref, gate_ref, lg_ref, lb_ref, yb_ref, o_ref, rows_a, rows_b, sem_a, sem_b):
    step, last = pl.program_id(0), pl.num_programs(0) - 1

    def gather(idx_ref, rows, sem):
        def start_row(t, carry):
            _tile_copy(yb_ref, rows, idx_ref[0, 0, t], t, sem).start()
            return carry
        lax.fori_loop(0, 2 * GATHER_TILE, start_row, 0, unroll=8)

    def finish(rows, sem):
        def wait_row(t, carry):
            _tile_copy(yb_ref, rows, 0, 0, sem).wait()
            return carry
        lax.fori_loop(0, 2 * GATHER_TILE, wait_row, 0, unroll=8)
        gate = gate_ref[...]
        ffn = (gate[:, 0:1] * _load_token_tiles(rows, 0, GATHER_TILE)
               + gate[:, 1:2] * _load_token_tiles(rows, GATHER_TILE * TOKEN_CHUNKS, GATHER_TILE))
        o_ref[...] = _layer_norm(ALPHA * h_ref[...] + ffn, lg_ref[...], lb_ref[...])

    @pl.when(step == 0)
    def _():
        gather(dest_ref, rows_a, sem_a)

    for parity, (rows, sem, rows_next, sem_next) in enumerate(((rows_a, sem_a, rows_b, sem_b),
                                                               (rows_b, sem_b, rows_a, sem_a))):
        @pl.when(step % 2 == parity)
        def _(rows=rows, sem=sem, rows_next=rows_next, sem_next=sem_next):
            @pl.when(step < last)
            def _():
                gather(next_ref, rows_next, sem_next)
            finish(rows, sem)


def _combine(h, gate_rows, dest_blocks, yb, ln_g, ln_b):
    n, d = h.shape
    tm = GATHER_TILE
    steps = n // tm
    rows = pltpu.VMEM((2 * tm * TOKEN_CHUNKS, LANES), F32)
    return pl.pallas_call(
        _combine_body, grid=(steps,),
        in_specs=[pl.BlockSpec((1, 1, 2 * tm), lambda i: (i, 0, 0), memory_space=pltpu.SMEM),
                  pl.BlockSpec((1, 1, 2 * tm), lambda i: (jnp.minimum(i + 1, steps - 1), 0, 0), memory_space=pltpu.SMEM),
                  pl.BlockSpec((tm, d), lambda i: (i, 0)), pl.BlockSpec((tm, 2), lambda i: (i, 0)),
                  pl.BlockSpec((1, d), lambda i: (0, 0)), pl.BlockSpec((1, d), lambda i: (0, 0)),
                  pl.BlockSpec(memory_space=pl.ANY)],
        out_specs=pl.BlockSpec((tm, d), lambda i: (i, 0)), out_shape=jax.ShapeDtypeStruct((n, d), F32),
        scratch_shapes=[rows, rows, pltpu.SemaphoreType.DMA(()), pltpu.SemaphoreType.DMA(())],
        compiler_params=_params("arbitrary"), name="moe_combine",
    )(dest_blocks, dest_blocks, h, gate_rows, ln_g.reshape(1, d), ln_b.reshape(1, d), yb)


def _moe_layer(h, ht, routing, layer, w_gate, w_up, w_down, ln_g, ln_b):
    n, d = h.shape
    idx, gate, rank, cnt = routing
    counts = cnt[:, 0].astype(I32)
    padded = (counts + MOE_TILE - 1) // MOE_TILE * MOE_TILE
    pad_end = jnp.cumsum(padded).astype(I32)
    pad_start = pad_end - padded
    experts = jnp.arange(N_EXPERTS, dtype=I32)
    dest = jnp.sum(jnp.where(idx[..., None] == experts, pad_start, 0), axis=-1) + rank
    cap = 2 * n + N_EXPERTS * MOE_TILE
    n_blocks = cap // MOE_TILE
    n_used = pad_end[-1] // MOE_TILE
    blk = jnp.minimum(jnp.arange(n_blocks, dtype=I32), n_used - 1)
    blk_expert = jnp.minimum(jnp.sum(pad_end[None, :] <= (blk * MOE_TILE)[:, None], axis=1), N_EXPERTS - 1).astype(I32)
    tm = GATHER_TILE
    dest_blocks = dest.reshape(2, n // tm, tm).transpose(1, 0, 2).reshape(n // tm, 1, 2 * tm)
    xb = _dispatch(ht, dest_blocks, pad_end, padded, cap)
    yb = _experts(xb, blk, blk_expert, n_used.reshape(1).astype(I32), layer, w_gate, w_up, w_down)
    return _combine(h, gate.T, dest_blocks, yb, ln_g, ln_b)


def kernel(x, ln_g, ln_b, even_w_in, even_w_out, even_lambda, even_subln_g, even_rel_bias, odd_mix, odd_w_rkv, odd_w0, odd_w1, odd_w2, odd_a0, odd_a1, odd_a2, odd_g1, odd_g2, odd_k_k, odd_k_a, odd_r_k, odd_lnx_g, odd_lnx_b, odd_w_out, vres_v0, vres_v1, vres_v2, router_w, router_b, moe_w_gate, moe_w_up, moe_w_down):
    bn, seq, d = x.shape
    x2 = x.reshape(bn * seq, d)
    v_first = None
    rotary_tables = _rotary_tables(seq)
    for layer in range(DEPTH):
        if layer % 2 == 0:
            e = layer // 2
            lam_init = 0.8 - 0.6 * math.exp(-0.3 * layer)
            h, ht, routing = _even_mixer(x2, bn, seq, rotary_tables, even_w_in[e], even_w_out[e], even_lambda[e],
                                         even_subln_g[e], even_rel_bias[e], lam_init, ln_g[layer, 0], ln_b[layer, 0],
                                         router_w, router_b)
        else:
            o = layer // 2
            vres = None if o == 0 else (vres_v0[o - 1], vres_v1[o - 1], vres_v2[o - 1])
            h, ht, routing, v_first = _rwkv_mixer(
                x2, bn, seq, v_first, odd_mix[o], odd_w_rkv[o], odd_w0[o], odd_w1[o], odd_w2[o], odd_a0[o], odd_a1[o],
                odd_a2[o], odd_g1[o], odd_g2[o], odd_k_k[o], odd_k_a[o], odd_r_k[o], odd_lnx_g[o], odd_lnx_b[o],
                odd_w_out[o], vres, ln_g[layer, 0], ln_b[layer, 0], router_w, router_b)
        x2 = _moe_layer(h, ht, routing, layer, moe_w_gate, moe_w_up, moe_w_down, ln_g[layer, 1], ln_b[layer, 1])
    return x2.reshape(bn, seq, d)
```

```python
import functools
import math

import jax
import jax.numpy as jnp
from jax import lax
from jax.experimental import pallas as pl
from jax.experimental.pallas import tpu as pltpu

F32 = jnp.float32
BF16 = jnp.bfloat16
I32 = jnp.int32

LANES = 128
SUBLANES = 8
VMEM_LIMIT_BYTES = 56 * 1024 * 1024

D_MODEL = 1024
DEPTH = 4
CHUNK = 64
A_HEADS = 4
A_DIM = 64
A_VDIM = 128
ROPE_DIM = 16
ROPE_THETA = 500000.0
SUBLN_EPS = 1e-5
B_HEADS = 8
B_DIM = 64
B_CHUNKS_BACK = 8
REL_CLIP = 128
BAND = (B_CHUNKS_BACK + 1) * CHUNK
SEG = 512
C_DIM = 64
LNX_EPS = 1e-5 * C_DIM
N_EXPERTS = 32
N_GROUPS = 4
EXPERTS_PER_GROUP = 8
D_EXPERT = 512
LN_EPS = 1e-5
ALPHA = (2 * DEPTH) ** 0.25
NEG_INF = -1e30

ROW_TILE = 512
ATT_TILE = 2048
ATT_KEYS = 256
V_ROWS = A_VDIM + 16
LOG2_E = math.log2(math.e)
Q_SCALE = A_DIM ** -0.5 * LOG2_E
BAND_TILE = B_CHUNKS_BACK * CHUNK
WKV_ROWS = 512
MOE_TILE = 512
GATHER_TILE = 512
TOKEN_CHUNKS = D_MODEL // LANES

_NT = (((1,), (1,)), ((), ()))
_TN = (((0,), (0,)), ((), ()))


def _params(*sem):
    return pltpu.CompilerParams(dimension_semantics=sem, vmem_limit_bytes=VMEM_LIMIT_BYTES)


def _dot(a, b):
    return jnp.dot(a.astype(BF16), b.astype(BF16), preferred_element_type=F32)


def _layer_norm(x, g, b):
    mu = jnp.mean(x, axis=-1, keepdims=True)
    xc = x - mu
    var = jnp.mean(xc * xc, axis=-1, keepdims=True)
    return xc * lax.rsqrt(var + LN_EPS) * g + b


def _sigmoid(x):
    return 1.0 / (1.0 + jnp.exp(-x))


def _row_call(body, rows, fulls, outs, tm, name):
    m = rows[0].shape[0] if not isinstance(rows[0], tuple) else None
    in_specs, args = [], []
    for r in rows:
        if isinstance(r, tuple):
            arr, fn, block_rows = r
            in_specs.append(pl.BlockSpec((block_rows, arr.shape[1]), lambda i, fn=fn: (fn(i), 0)))
        else:
            arr = r
            m = arr.shape[0]
            in_specs.append(pl.BlockSpec((tm, arr.shape[1]), lambda i: (i, 0)))
        args.append(arr)
    for f in fulls:
        in_specs.append(pl.BlockSpec(f.shape, lambda i, nd=f.ndim: (0,) * nd))
        args.append(f)
    out_specs = [pl.BlockSpec((tm, w), lambda i: (i, 0)) for w, _ in outs]
    out_shape = [jax.ShapeDtypeStruct((m, w), dt) for w, dt in outs]
    return pl.pallas_call(
        body, grid=(m // tm,), in_specs=in_specs, out_specs=out_specs, out_shape=out_shape,
        compiler_params=_params("parallel"), name=name)(*args)


def _out_call(body, rows, fulls, router_w, router_b, name):
    n, tm = rows[0].shape[0], ROW_TILE
    upper = (jnp.arange(tm)[:, None] < jnp.arange(tm)[None, :]).astype(BF16)
    fulls = list(fulls) + [router_w.T.astype(F32), router_b.reshape(N_EXPERTS, 1).astype(F32), upper]
    in_specs = [pl.BlockSpec((tm, r.shape[1]), lambda i: (i, 0)) for r in rows]
    in_specs += [pl.BlockSpec(f.shape, lambda i, nd=f.ndim: (0,) * nd) for f in fulls]
    tok = pl.BlockSpec((2, tm), lambda i: (0, i))
    h, ht, *routing = pl.pallas_call(
        body, grid=(n // tm,), in_specs=in_specs,
        out_specs=[pl.BlockSpec((tm, D_MODEL), lambda i: (i, 0)), pl.BlockSpec((tm * TOKEN_CHUNKS, LANES), lambda i: (i, 0)),
                   tok, tok, tok, pl.BlockSpec((N_EXPERTS, 1), lambda i: (0, 0))],
        out_shape=[jax.ShapeDtypeStruct((n, D_MODEL), F32), jax.ShapeDtypeStruct((n * TOKEN_CHUNKS, LANES), F32),
                   jax.ShapeDtypeStruct((2, n), I32), jax.ShapeDtypeStruct((2, n), F32),
                   jax.ShapeDtypeStruct((2, n), I32), jax.ShapeDtypeStruct((N_EXPERTS, 1), F32)],
        scratch_shapes=[pltpu.VMEM((N_EXPERTS, 1), F32)],
        compiler_params=_params("arbitrary"), name=name)(*rows, *fulls)
    return h, ht, routing


def _even_proj_body(x_ref, c_ref, sm_ref, sp_ref, ct_ref, smt_ref, spt_ref, w_ref, wt_ref,
                    aq_ref, ak_ref, av_ref, bq_ref, bk_ref, bv_ref):
    x = x_ref[...]
    xb = x.astype(BF16)
    half = ROPE_DIM // 2

    def seg(j):
        return jnp.dot(xb, w_ref[:, j * SEG:(j + 1) * SEG], preferred_element_type=F32)

    qv_t = jnp.dot(wt_ref[...], x.T.astype(BF16), preferred_element_type=F32)
    cos, s_minus, s_plus = ct_ref[...], smt_ref[...], spt_ref[...]
    for j in range(SEG // LANES):
        u = qv_t[j * LANES:(j + 1) * LANES, :]
        rot = u * cos + pltpu.roll(u, LANES - half, 0) * s_minus + pltpu.roll(u, half, 0) * s_plus
        aq_ref[j * LANES:(j + 1) * LANES, :] = (rot * Q_SCALE).astype(aq_ref.dtype)
    for v_ref, base in ((av_ref, SEG), (bv_ref, 3 * SEG)):
        for h in range(SEG // LANES):
            v_ref[h * V_ROWS:h * V_ROWS + LANES, :] = qv_t[base + h * LANES:base + (h + 1) * LANES, :].astype(v_ref.dtype)
            v_ref[h * V_ROWS + LANES:(h + 1) * V_ROWS, :] = jnp.ones((V_ROWS - LANES, x.shape[0]), v_ref.dtype)
    bq_ref[...] = (qv_t[2 * SEG:3 * SEG, :] * (B_DIM ** -0.5 * LOG2_E)).astype(bq_ref.dtype)
    cos, s_minus, s_plus = c_ref[...], sm_ref[...], sp_ref[...]
    k = seg(1)
    for j in range(SEG // LANES):
        u = k[:, j * LANES:(j + 1) * LANES]
        rot = u * cos + pltpu.roll(u, LANES - half, 1) * s_minus + pltpu.roll(u, half, 1) * s_plus
        ak_ref[:, j * LANES:(j + 1) * LANES] = rot.astype(ak_ref.dtype)
    bk_ref[...] = seg(4).astype(bk_ref.dtype)


def _rotary_tables(seq):
    half = ROPE_DIM // 2
    inv = ROPE_THETA ** (-jnp.arange(half, dtype=F32) / half)
    ang = jnp.arange(seq, dtype=F32)[:, None] * inv
    cos, sin = jnp.cos(ang), jnp.sin(ang)
    d = jnp.arange(LANES) % A_DIM
    first, second = d < half, (d >= half) & (d < ROPE_DIM)
    col = jnp.where(first, d, d - half) % half
    cos_t = jnp.where(first | second, cos[:, col], 1.0)
    s_minus = jnp.where(first, -sin[:, col], 0.0)
    s_plus = jnp.where(second, sin[:, col], 0.0)
    return cos_t, s_minus, s_plus


def _even_proj(x2, w_in, tabs, seq):
    n, d = x2.shape
    tm = ROW_TILE
    blocks = seq // tm
    w = w_in.astype(BF16)
    seg = lambda j: w[:, j * SEG:(j + 1) * SEG]
    w_qv_t = jnp.concatenate([seg(0), seg(2), seg(3), seg(5)], axis=1).T
    tabs_t = [t.T for t in tabs]
    rows = pl.BlockSpec((tm, SEG), lambda i: (i, 0))
    cols = lambda r: pl.BlockSpec((r, tm), lambda i: (0, i))
    full = lambda a: pl.BlockSpec(a.shape, lambda i: (0, 0))
    v_rows = SEG // LANES * V_ROWS
    shapes = [(SEG, n), (n, SEG), (v_rows, n), (SEG, n), (n, SEG), (v_rows, n)]
    return pl.pallas_call(
        _even_proj_body, grid=(n // tm,),
        in_specs=[pl.BlockSpec((tm, d), lambda i: (i, 0))]
        + [pl.BlockSpec((tm, LANES), lambda i: (i % blocks, 0))] * 3
        + [pl.BlockSpec((LANES, tm), lambda i: (0, i % blocks))] * 3 + [full(w), full(w_qv_t)],
        out_specs=[cols(SEG), rows, cols(v_rows), cols(SEG), rows, cols(v_rows)],
        out_shape=[jax.ShapeDtypeStruct(s, BF16) for s in shapes],
        compiler_params=_params("parallel"), name="even_proj")(x2, *tabs, *tabs_t, w, w_qv_t)


def _diff_attn_body(qi_ref, kj_ref, lam_ref, q_ref, k_ref, v_ref, g_ref, o_ref, q2_s, m_s, l_s, acc_s, *, lam_init):
    p = pl.program_id(2)
    qi, kj = qi_ref[p], kj_ref[p]
    groups = ATT_TILE // LANES

    @pl.when(kj == 0)
    def _():
        feat = lax.broadcasted_iota(I32, (LANES, LANES), 0)
        for j in range(groups):
            q = q_ref[:, j * LANES:(j + 1) * LANES]
            zero = jnp.zeros_like(q)
            q2_s[j, :, 0:LANES] = jnp.where(feat < A_DIM, q, zero)
            q2_s[j, :, LANES:] = jnp.where(feat >= A_DIM, q, zero)
        m_s[...] = jnp.full(m_s.shape, -jnp.inf, F32)
        l_s[...] = jnp.zeros(l_s.shape, F32)
        acc_s[...] = jnp.zeros(acc_s.shape, F32)

    def step(diagonal):
        pieces = range(ATT_TILE // ATT_KEYS)

        def live(j, i):
            return not diagonal or i * ATT_KEYS // CHUNK <= ((j + 1) * LANES - 1) // CHUNK

        def scores(j, i):
            if not live(j, i):
                return None
            s = jnp.dot(k_ref[0, i * ATT_KEYS:(i + 1) * ATT_KEYS, :], q2_s[j], preferred_element_type=F32)
            if diagonal:
                key_chunk = (lax.broadcasted_iota(I32, s.shape, 0) + i * ATT_KEYS) // CHUNK
                query_chunk = (lax.broadcasted_iota(I32, s.shape, 1) % LANES + j * LANES) // CHUNK
                s = jnp.where(key_chunk <= query_chunk, s, NEG_INF)
            return s

        def stats(j, s):
            m = m_s[j]
            m_new = m
            for piece in s:
                if piece is not None:
                    m_new = jnp.maximum(m_new, jnp.max(piece, axis=0, keepdims=True))
            m_s[j] = m_new
            return jnp.exp2(m - m_new), m_new

        def values(pr, i):
            if pr[i] is None:
                return None
            return jnp.dot(v_ref[:, i * ATT_KEYS:(i + 1) * ATT_KEYS], pr[i], preferred_element_type=F32)

        def accumulate(j, alpha, pv):
            pv = functools.reduce(lambda a, b: a + b, [x for x in pv if x is not None])
            acc_s[j] = alpha * acc_s[j] + pv[0:A_VDIM]
            l_s[j] = alpha * l_s[j] + pv[A_VDIM:A_VDIM + 1]

        s = [scores(0, i) for i in pieces]
        alpha, m_new = stats(0, s)
        pending = None
        for j in range(groups):
            s_next, pr, pv = [], [], []
            for i in pieces:
                if j + 1 < groups:
                    s_next.append(scores(j + 1, i))
                pr.append(None if s[i] is None else jnp.exp2(s[i] - m_new).astype(BF16))
                if pending is not None:
                    pv.append(values(pending[2], i))
            if pending is not None:
                accumulate(pending[0], pending[1], pv)
            pending = (j, alpha, pr)
            if j + 1 < groups:
                s = s_next
                alpha, m_new = stats(j + 1, s)
        accumulate(pending[0], pending[1], [values(pending[2], i) for i in pieces])

    @pl.when(kj != qi)
    def _():
        step(False)

    @pl.when(kj == qi)
    def _():
        step(True)
        for j in range(groups):
            o = acc_s[j] / l_s[j]
            o = o[:, 0:LANES] - lam_ref[0] * o[:, LANES:]
            od = o.T
            od = od * lax.rsqrt(jnp.mean(od * od, axis=-1, keepdims=True) + SUBLN_EPS) * g_ref[...] * (1.0 - lam_init)
            o_ref[0, j * LANES:(j + 1) * LANES, :] = od.astype(o_ref.dtype)


def _diff_attn(aq_t, ak, av_t, lam, subln_g, lam_init):
    bn, seq, _ = ak.shape
    t = ATT_TILE
    nblk = seq // t
    pairs = [(i, j) for i in range(nblk) for j in range(i + 1)]
    qi = jnp.asarray([p[0] for p in pairs], I32)
    kj = jnp.asarray([p[1] for p in pairs], I32)
    groups, width = t // LANES, 2 * LANES
    grid_spec = pltpu.PrefetchScalarGridSpec(
        num_scalar_prefetch=3, grid=(bn, A_HEADS, len(pairs)),
        in_specs=[
            pl.BlockSpec((LANES, t), lambda b, h, p, qi, kj, lam: (h, b * nblk + qi[p])),
            pl.BlockSpec((1, t, LANES), lambda b, h, p, qi, kj, lam: (b, kj[p], h)),
            pl.BlockSpec((V_ROWS, t), lambda b, h, p, qi, kj, lam: (h, b * nblk + kj[p])),
            pl.BlockSpec((1, LANES), lambda b, h, p, qi, kj, lam: (0, 0)),
        ],
        out_specs=pl.BlockSpec((1, t, LANES), lambda b, h, p, qi, kj, lam: (b, qi[p], h)),
        scratch_shapes=[pltpu.VMEM((groups, LANES, width), BF16), pltpu.VMEM((groups, 1, width), F32),
                        pltpu.VMEM((groups, 1, width), F32), pltpu.VMEM((groups, A_VDIM, width), F32)])
    return pl.pallas_call(
        functools.partial(_diff_attn_body, lam_init=lam_init), grid_spec=grid_spec,
        out_shape=jax.ShapeDtypeStruct((bn, seq, A_HEADS * A_VDIM), BF16),
        compiler_params=_params("parallel", "parallel", "arbitrary"), name="diff_attn",
    )(qi, kj, lam.reshape(1).astype(F32), aq_t, ak, av_t, subln_g.reshape(1, A_VDIM))


def _chunk_attn_body(q_ref, kp_ref, kc_ref, vp_ref, vc_ref, bias_ref, o_ref, kcat, vcat):
    i = pl.program_id(1)
    t = BAND_TILE
    kcat[0:t, :] = kp_ref[0]
    kcat[t:, :] = kc_ref[0]
    vcat[:, 0:t] = vp_ref[...]
    vcat[:, t:] = vc_ref[...]
    pair = 2 * CHUNK
    keys = BAND + CHUNK
    feat = lax.broadcasted_iota(I32, (LANES, pair), 0)
    key = lax.broadcasted_iota(I32, (keys, pair), 0)
    items = [(cp, hp, hh) for cp in range(t // pair) for hp in range(B_HEADS // 2) for hh in range(2)]

    def scores(item):
        cp, hp, hh = item
        q = q_ref[hp * LANES:(hp + 1) * LANES, cp * pair:(cp + 1) * pair]
        q = jnp.where((feat < B_DIM) if hh == 0 else (feat >= B_DIM), q, jnp.zeros_like(q))
        s = jnp.dot(kcat[cp * pair:cp * pair + keys, hp * LANES:(hp + 1) * LANES], q, preferred_element_type=F32)
        s = s + bias_ref[2 * hp + hh]
        return jnp.where((i > 0) | (key + cp * pair >= t), s, NEG_INF)

    s_next = scores(items[0])
    halves = []
    for n, (cp, hp, hh) in enumerate(items):
        s = s_next
        if n + 1 < len(items):
            s_next = scores(items[n + 1])
        pr = jnp.exp2(s - jnp.max(s, axis=0, keepdims=True)).astype(BF16)
        pv = jnp.dot(vcat[hp * V_ROWS:(hp + 1) * V_ROWS, cp * pair:cp * pair + keys], pr, preferred_element_type=F32)
        halves.append(pv[hh * B_DIM:(hh + 1) * B_DIM] / pv[LANES:LANES + 1])
        if hh == 1:
            out = jnp.concatenate(halves, axis=0).T
            o_ref[0, cp * pair:(cp + 1) * pair, hp * LANES:(hp + 1) * LANES] = out.astype(o_ref.dtype)
            halves = []


def _chunk_attn(bq_t, bk, bv_t, rel_bias):
    bn, seq, width = bk.shape
    t = BAND_TILE
    nblk = seq // t
    far = jnp.broadcast_to(rel_bias[:, 2 * REL_CLIP:], (B_HEADS, BAND - REL_CLIP))
    f = jnp.concatenate([far, rel_bias[:, REL_CLIP - CHUNK + 1:2 * REL_CLIP][:, ::-1]], axis=1) * LOG2_E
    span = f.shape[1]
    flat = jnp.broadcast_to(f[:, None, :], (B_HEADS, CHUNK, span)).reshape(B_HEADS, CHUNK * span)
    skew = flat[:, CHUNK - 1:CHUNK - 1 + CHUNK * (span - 1)].reshape(B_HEADS, CHUNK, span - 1)
    bias_t = skew[:, :, 0:BAND].transpose(0, 2, 1)
    out_of_band = jnp.full((B_HEADS, CHUNK, CHUNK), NEG_INF, F32)
    bias2 = jnp.concatenate([jnp.concatenate([bias_t, out_of_band], axis=1),
                             jnp.concatenate([out_of_band, bias_t], axis=1)], axis=2).astype(F32)
    cur = pl.BlockSpec((1, t, width), lambda b, i: (b, i, 0))
    prev = pl.BlockSpec((1, t, width), lambda b, i: (b, jnp.maximum(i - 1, 0), 0))
    v_rows = bv_t.shape[0]
    cur_t = lambda r: pl.BlockSpec((r, t), lambda b, i: (0, b * nblk + i))
    prev_t = lambda r: pl.BlockSpec((r, t), lambda b, i: (0, b * nblk + jnp.maximum(i - 1, 0)))
    return pl.pallas_call(
        _chunk_attn_body, grid=(bn, nblk),
        in_specs=[cur_t(width), prev, cur, prev_t(v_rows), cur_t(v_rows), pl.BlockSpec(bias2.shape, lambda b, i: (0, 0, 0))],
        out_specs=cur, out_shape=jax.ShapeDtypeStruct((bn, seq, width), BF16),
        scratch_shapes=[pltpu.VMEM((2 * t, width), BF16), pltpu.VMEM((v_rows, 2 * t), BF16)],
        compiler_params=_params("parallel", "parallel"), name="chunk_attn",
    )(bq_t, bk, bk, bv_t, bv_t, bias2)


def _even_out_body(x_ref, ya_ref, yb_ref, w_ref, g_ref, b_ref, wt_ref, rb_ref, upper_ref, h_ref, ht_ref, *route_refs):
    mix = jnp.dot(ya_ref[...], w_ref[0:SEG, :], preferred_element_type=F32)
    mix = mix + jnp.dot(yb_ref[...], w_ref[SEG:, :], preferred_element_type=F32)
    h = _layer_norm(ALPHA * x_ref[...] + mix, g_ref[...], b_ref[...])
    h_ref[...] = h
    _store_token_tiles(ht_ref, h)
    _route(h, wt_ref, rb_ref, upper_ref, *route_refs)


def _even_mixer(x2, bn, seq, rotary_tables, w_in, w_out, lam_p, subln_g, rel_bias, lam_init, ln_g, ln_b,
                router_w, router_b):
    aq_t, ak, av_t, bq_t, bk, bv_t = _even_proj(x2, w_in, rotary_tables, seq)
    lp = lam_p.astype(F32)
    lam = jnp.exp(jnp.sum(lp[0] * lp[1])) - jnp.exp(jnp.sum(lp[2] * lp[3])) + lam_init
    r3 = lambda a: a.reshape(bn, seq, SEG)
    ya = _diff_attn(aq_t, r3(ak), av_t, lam, subln_g, lam_init)
    yb = _chunk_attn(bq_t, r3(bk), bv_t, rel_bias)
    return _out_call(_even_out_body, [x2, ya.reshape(-1, SEG), yb.reshape(-1, SEG)],
                     [w_out.astype(BF16), ln_g.reshape(1, -1), ln_b.reshape(1, -1)], router_w, router_b, "even_out")


def _rwkv_pre_body(*refs, has_vres, blocks_per_seq):
    if has_vres:
        (x_ref, xp_ref, vf_ref, mix_ref, wr_ref, wk_ref, wv_ref, w0_ref, w1_ref, w2_ref, a0_ref, a1_ref, a2_ref,
         g1_ref, g2_ref, kk_ref, ka_ref, v0_ref, v1_ref, v2_ref, r_o, lw_o, k_o, v_o, kkr_o, a_o, g_o) = refs
    else:
        (x_ref, xp_ref, mix_ref, wr_ref, wk_ref, wv_ref, w0_ref, w1_ref, w2_ref, a0_ref, a1_ref, a2_ref,
         g1_ref, g2_ref, kk_ref, ka_ref, r_o, lw_o, k_o, v_o, kkr_o, a_o, g_o) = refs
    x = x_ref[...]
    first_of_seq = (pl.program_id(0) % blocks_per_seq) == 0
    before = jnp.where(first_of_seq, 0.0, xp_ref[SUBLANES - 1:SUBLANES, :])
    row = lax.broadcasted_iota(I32, x.shape, 0)
    xx = jnp.where(row == 0, before, pltpu.roll(x, 1, 0)) - x
    mixed = lambda i: (x + xx * mix_ref[i:i + 1, :]).astype(BF16)
    r = _dot(mixed(0), wr_ref[...])
    k = _dot(mixed(2), wk_ref[...])
    xv = mixed(3)
    v = _dot(xv, wv_ref[...])
    z = w0_ref[...] + _dot(jnp.tanh(_dot(mixed(1), w1_ref[...])), w2_ref[...])
    softplus = jnp.maximum(-z, 0.0) + jnp.log(1.0 + jnp.exp(-jnp.abs(z)))
    lw_o[...] = -jnp.exp(-softplus - 0.5)
    if has_vres:
        v = v + (vf_ref[...].astype(F32) - v) * _sigmoid(v0_ref[...] + _dot(_dot(xv, v1_ref[...]), v2_ref[...]))
    a = _sigmoid(a0_ref[...] + _dot(_dot(mixed(4), a1_ref[...]), a2_ref[...]))
    g_o[...] = _dot(_sigmoid(_dot(mixed(5), g1_ref[...])), g2_ref[...]).astype(g_o.dtype)
    r_o[...] = r.astype(r_o.dtype)
    v_o[...] = v.astype(v_o.dtype)
    a_o[...] = a.astype(a_o.dtype)
    kkr_o[...] = (k * kk_ref[...]).astype(kkr_o.dtype)
    k_o[...] = (k * (1.0 + (a - 1.0) * ka_ref[...])).astype(k_o.dtype)


def _head_sum(x, first):
    zero = jnp.zeros_like(x)
    s0 = jnp.sum(jnp.where(first, x, zero), axis=1, keepdims=True)
    s1 = jnp.sum(jnp.where(first, zero, x), axis=1, keepdims=True)
    return jnp.where(first, s0, s1)


def _wkv_body(r_ref, lw_ref, k_ref, v_ref, kk_ref, a_ref, rk_ref, lg_ref, lb_ref, y_ref, z_s, *bufs):
    step = pl.program_id(2)
    half = len(bufs) // 2

    @pl.when(step == 0)
    def _():
        z_s[...] = jnp.zeros(z_s.shape, F32)
        for buf in bufs[half:]:
            buf[...] = jnp.zeros(buf.shape, buf.dtype)

    @pl.when(step % 2 == 0)
    def _():
        _wkv_step(r_ref, lw_ref, k_ref, v_ref, kk_ref, a_ref, rk_ref, lg_ref, lb_ref, y_ref, z_s,
                  bufs[:half], bufs[half:])

    @pl.when(step % 2 == 1)
    def _():
        _wkv_step(r_ref, lw_ref, k_ref, v_ref, kk_ref, a_ref, rk_ref, lg_ref, lb_ref, y_ref, z_s,
                  bufs[half:], bufs[:half])


def _wkv_step(r_ref, lw_ref, k_ref, v_ref, kk_ref, a_ref, rk_ref, lg_ref, lb_ref, y_ref, z_s, fill, drain):
    n = CHUNK
    lhs_f, y0_f, sadd_f, decay_f, bonus_f = fill
    lhs_d, y0_d, sadd_d, decay_d, bonus_d = drain
    first = lax.broadcasted_iota(I32, (n, LANES), 1) < C_DIM
    row = lax.broadcasted_iota(I32, (2 * n, 2 * n), 0)
    col = lax.broadcasted_iota(I32, (2 * n, 2 * n), 1)
    same = (row // n) == (col // n)
    strict = same & ((col % n) < (row % n))
    incl = same & ((col % n) <= (row % n))
    eye = (row == col).astype(F32)
    tri = (lax.broadcasted_iota(I32, (n, n), 1) <= lax.broadcasted_iota(I32, (n, n), 0)).astype(BF16)

    def stack(x):
        zero = jnp.zeros_like(x)
        return jnp.concatenate([jnp.where(first, x, zero), jnp.where(first, zero, x)], axis=0)

    zero = jnp.zeros((2 * n, 2 * n), F32)
    chunks = range(WKV_ROWS // n)
    state = {"z": z_s[...]}
    y2 = []

    def pipeline(c):
        z = state["z"]
        both = jnp.dot(lhs_d[c], z.astype(BF16), preferred_element_type=F32)
        y2.append(both[0:2 * n] + y0_d[c])
        state["z"] = decay_d[c] * z + both[2 * n:] + sadd_d[c]
        rows = slice(c * n, (c + 1) * n)
        r, lw, k, v, kk, a = (ref[0, rows, :].astype(F32) for ref in (r_ref, lw_ref, k_ref, v_ref, kk_ref, a_ref))
        kk = kk / jnp.maximum(jnp.sqrt(_head_sum(kk * kk, first)), 1e-12)
        hi = lw.astype(BF16)
        rem = lw - hi.astype(F32)
        mid = rem.astype(BF16)
        low = (rem - mid.astype(F32)).astype(BF16)
        cum = (jnp.dot(tri, hi, preferred_element_type=F32) + jnp.dot(tri, mid, preferred_element_type=F32)
               + jnp.dot(tri, low, preferred_element_type=F32))
        total = cum[n - 1:n, :]
        g_inv = jnp.exp(-cum)
        g_tail = jnp.exp(total - cum)
        a2 = stack(-kk * jnp.exp(cum - lw))
        r2 = stack(r * jnp.exp(cum))
        b2 = stack(kk * a * g_inv)
        k2 = stack(k * g_inv)
        m = lax.dot_general(jnp.concatenate([a2, r2], axis=0).astype(BF16),
                            jnp.concatenate([b2, k2], axis=0).astype(BF16), _NT, preferred_element_type=F32)
        v2 = stack(v)
        bonus_f[rows, :] = _head_sum(r * k * rk_ref[...], first) * v
        decay_f[c] = jnp.broadcast_to(jnp.sum(eye * jnp.exp(total), axis=1, keepdims=True), (2 * n, 2 * n))
        tails_t = jnp.concatenate([stack(kk * a * g_tail), stack(k * g_tail)], axis=0).T.astype(BF16)
        a_ab = jnp.where(strict, m[0:2 * n, 0:2 * n], zero)
        a_ak = jnp.where(strict, m[0:2 * n, 2 * n:], zero)
        a_r = jnp.concatenate([jnp.where(incl, m[2 * n:, 0:2 * n], zero),
                               jnp.where(incl, m[2 * n:, 2 * n:], zero)], axis=1).astype(BF16)
        yield
        inv = eye + a_ab
        power = a_ab.astype(BF16)
        power = jnp.dot(power, power, preferred_element_type=F32).astype(BF16)
        akv = _dot(a_ak, v2)
        yield
        for _ in range(int(math.log2(n)) - 2):
            both = jnp.dot(jnp.concatenate([power, inv.astype(BF16)], axis=0), power, preferred_element_type=F32)
            power = both[0:2 * n].astype(BF16)
            inv = inv + both[2 * n:]
            yield
        inv = inv + jnp.dot(inv.astype(BF16), power, preferred_element_type=F32)
        yield
        wu = _dot(inv, jnp.concatenate([a2, akv], axis=1))
        yield
        rhs = jnp.concatenate([wu, jnp.concatenate([zero, v2], axis=1)], axis=0).astype(BF16)
        x = jnp.dot(jnp.concatenate([a_r, tails_t], axis=0), rhs, preferred_element_type=F32)
        yield
        lhs_f[c] = jnp.concatenate([r2 + x[0:2 * n, 0:2 * n], x[2 * n:, 0:2 * n]], axis=0).astype(BF16)
        y0_f[c] = x[0:2 * n, 2 * n:]
        sadd_f[c] = x[2 * n:, 2 * n:]

    running = [pipeline(c) for c in chunks]
    for _ in range(16):
        running = [g for g in running if next(g, "done") != "done"]
    assert not running
    z_s[...] = state["z"]
    for c in chunks:
        rows = slice(c * n, (c + 1) * n)
        y = y2[c][0:n] + y2[c][n:]
        mu = _head_sum(y, first) * (1.0 / C_DIM)
        yc = y - mu
        var = _head_sum(yc * yc, first) * (1.0 / C_DIM)
        y_ref[0, rows, :] = (yc * lax.rsqrt(var + LNX_EPS) * lg_ref[...] + lb_ref[...] + bonus_d[rows, :]).astype(y_ref.dtype)


def _wkv(r, lw, k, v, kkr, a, r_k, lnx_g, lnx_b):
    bn, seq, d = r.shape
    nblk, nc, n2 = seq // WKV_ROWS, WKV_ROWS // CHUNK, 2 * CHUNK
    blk = pl.BlockSpec((1, WKV_ROWS, LANES), lambda b, h, c: (b, jnp.minimum(c, nblk - 1), h))
    out = pl.BlockSpec((1, WKV_ROWS, LANES), lambda b, h, c: (b, jnp.maximum(c - 1, 0), h))
    par = pl.BlockSpec((1, LANES), lambda b, h, c: (0, h))
    buffers = [pltpu.VMEM((nc, 2 * n2, n2), BF16), pltpu.VMEM((nc, n2, n2), F32), pltpu.VMEM((nc, n2, n2), F32),
               pltpu.VMEM((nc, n2, n2), F32), pltpu.VMEM((WKV_ROWS, LANES), F32)]
    return pl.pallas_call(
        _wkv_body, grid=(bn, d // LANES, nblk + 1),
        in_specs=[blk] * 6 + [par] * 3, out_specs=out,
        out_shape=jax.ShapeDtypeStruct((bn, seq, d), BF16),
        scratch_shapes=[pltpu.VMEM((LANES, LANES), F32)] + buffers + buffers,
        compiler_params=_params("parallel", "parallel", "arbitrary"), name="wkv7",
    )(r, lw, k, v, kkr, a, r_k.reshape(1, d), lnx_g.reshape(1, d), lnx_b.reshape(1, d))


def _odd_out_body(x_ref, y_ref, g_ref, w_ref, lg_ref, lb_ref, wt_ref, rb_ref, upper_ref, h_ref, ht_ref, *route_refs):
    mix = _dot(y_ref[...].astype(F32) * g_ref[...].astype(F32), w_ref[...])
    h = _layer_norm(ALPHA * x_ref[...] + mix, lg_ref[...], lb_ref[...])
    h_ref[...] = h
    _store_token_tiles(ht_ref, h)
    _route(h, wt_ref, rb_ref, upper_ref, *route_refs)


def _rwkv_mixer(x2, bn, seq, v_first, mix, w_rkv, w0, w1, w2, a0, a1, a2, g1, g2, k_k, k_a, r_k, lnx_g, lnx_b,
                w_out, vres, ln_g, ln_b, router_w, router_b):
    d = x2.shape[1]
    row = lambda p: p.reshape(1, -1)
    bf = lambda p: p.astype(BF16)
    groups = ROW_TILE // SUBLANES
    rows = [x2, (x2, lambda i: jnp.maximum(i * groups - 1, 0), SUBLANES)] + ([v_first] if vres is not None else [])
    fulls = [mix, bf(w_rkv[0]), bf(w_rkv[1]), bf(w_rkv[2]), row(w0), bf(w1), bf(w2), row(a0), bf(a1), bf(a2),
             bf(g1), bf(g2), row(k_k), row(k_a)]
    if vres is not None:
        fulls += [row(vres[0]), bf(vres[1]), bf(vres[2])]
    body = functools.partial(_rwkv_pre_body, has_vres=vres is not None, blocks_per_seq=seq // ROW_TILE)
    outs = [(d, BF16), (d, F32)] + [(d, BF16)] * 5
    r, lw, k, v, kkr, a, g = _row_call(body, rows, fulls, outs, ROW_TILE, "rwkv_pre")
    if vres is None:
        v_first = v
    r3 = lambda t: t.reshape(bn, seq, d)
    y = _wkv(r3(r), r3(lw), r3(k), r3(v), r3(kkr), r3(a), r_k, lnx_g, lnx_b)
    h, ht, routing = _out_call(_odd_out_body, [x2, y.reshape(-1, d), g], [bf(w_out), row(ln_g), row(ln_b)],
                               router_w, router_b, "odd_out")
    return h, ht, routing, v_first


def _route(h, wt_ref, b_ref, upper_ref, idx_ref, gate_ref, rank_ref, cnt_ref, carry_s):
    @pl.when(pl.program_id(0) == 0)
    def _():
        carry_s[...] = jnp.zeros(carry_s.shape, F32)

    logits = lax.dot_general(wt_ref[...], h, _NT, precision=lax.Precision.HIGHEST,
                             preferred_element_type=F32)
    scores = _sigmoid(logits)
    sel = scores + b_ref[...]
    sub = lax.broadcasted_iota(I32, sel.shape, 0)
    grp = sub // EXPERTS_PER_GROUP
    ninf = jnp.full(sel.shape, -jnp.inf, F32)
    big = jnp.full(sel.shape, N_EXPERTS, I32)

    def top2(vals):
        m1 = jnp.max(vals, axis=0, keepdims=True)
        i1 = jnp.min(jnp.where(vals == m1, sub, big), axis=0, keepdims=True)
        rest = jnp.where(sub == i1, ninf, vals)
        m2 = jnp.max(rest, axis=0, keepdims=True)
        i2 = jnp.min(jnp.where(rest == m2, sub, big), axis=0, keepdims=True)
        return m1, i1, m2, i2

    best = jnp.zeros((1, sel.shape[1]), I32)
    best_score = None
    for g in range(N_GROUPS):
        m1, _, m2, _ = top2(jnp.where(grp == g, sel, ninf))
        score = m1 + m2
        if g == 0:
            best_score = score
        else:
            better = score > best_score
            best = jnp.where(better, g, best)
            best_score = jnp.where(better, score, best_score)
    _, i1, _, i2 = top2(jnp.where(grp == best, sel, ninf))
    oh1, oh2 = sub == i1, sub == i2
    zero = jnp.zeros_like(scores)
    g1 = jnp.sum(jnp.where(oh1, scores, zero), axis=0, keepdims=True)
    g2 = jnp.sum(jnp.where(oh2, scores, zero), axis=0, keepdims=True)
    den = g1 + g2
    idx_ref[...] = jnp.concatenate([i1, i2], axis=0)
    gate_ref[...] = jnp.concatenate([g1 / den, g2 / den], axis=0)
    both = (oh1 | oh2).astype(BF16)
    before = jnp.dot(both, upper_ref[...], preferred_element_type=F32) + carry_s[...]
    rank1 = jnp.sum(jnp.where(oh1, before, zero), axis=0, keepdims=True)
    rank2 = jnp.sum(jnp.where(oh2, before, zero), axis=0, keepdims=True)
    rank_ref[...] = jnp.concatenate([rank1, rank2], axis=0).astype(I32)
    carry_s[...] = carry_s[...] + jnp.sum(both.astype(F32), axis=1, keepdims=True)
    cnt_ref[...] = carry_s[...]


def _load_token_tiles(ref, first_row, tokens):
    chunks = [ref[pl.ds(first_row + j, tokens, stride=TOKEN_CHUNKS), :] for j in range(TOKEN_CHUNKS)]
    return jnp.concatenate(chunks, axis=1)


def _store_token_tiles(ref, val):
    tokens = val.shape[0]
    for j in range(TOKEN_CHUNKS):
        ref[pl.ds(j, tokens, stride=TOKEN_CHUNKS), :] = val[:, j * LANES:(j + 1) * LANES].astype(ref.dtype)


def _tile_copy(src, dst, src_tok, dst_tok, sem):
    src_row = pl.multiple_of(src_tok * TOKEN_CHUNKS, TOKEN_CHUNKS)
    dst_row = pl.multiple_of(dst_tok * TOKEN_CHUNKS, TOKEN_CHUNKS)
    return pltpu.make_async_copy(src.at[pl.ds(src_row, TOKEN_CHUNKS), :], dst.at[pl.ds(dst_row, TOKEN_CHUNKS), :], sem)


def _dispatch_body(pad_end_ref, padded_ref, dest_ref, ht_ref, xb_ref, zero_s, sem):
    block_rows = MOE_TILE * TOKEN_CHUNKS
    n_blocks = xb_ref.shape[0] // block_rows

    def block_copy(b):
        return pltpu.make_async_copy(zero_s, xb_ref.at[pl.ds(pl.multiple_of(b * block_rows, block_rows), block_rows), :], sem)

    @pl.when(pl.program_id(0) == 0)
    def _():
        zero_s[...] = jnp.zeros(zero_s.shape, zero_s.dtype)

        def start(e, carry):
            @pl.when(padded_ref[e] > 0)
            def _():
                block_copy(pad_end_ref[e] // MOE_TILE - 1).start()
            return carry

        def wait(e, carry):
            @pl.when(padded_ref[e] > 0)
            def _():
                block_copy(0).wait()
            return carry

        lax.fori_loop(0, N_EXPERTS, start, 0)
        lax.fori_loop(0, N_EXPERTS, wait, 0)
        first_unused = pad_end_ref[N_EXPERTS - 1] // MOE_TILE
        lax.fori_loop(first_unused, n_blocks, lambda b, c: (block_copy(b).start(), c)[1], 0)
        lax.fori_loop(first_unused, n_blocks, lambda b, c: (block_copy(0).wait(), c)[1], 0)

    def start_token(t, carry):
        _tile_copy(ht_ref, xb_ref, t, dest_ref[0, 0, t], sem).start()
        _tile_copy(ht_ref, xb_ref, t, dest_ref[0, 0, GATHER_TILE + t], sem).start()
        return carry

    def wait_row(t, carry):
        _tile_copy(ht_ref, xb_ref, 0, 0, sem).wait()
        return carry

    lax.fori_loop(0, GATHER_TILE, start_token, 0, unroll=8)
    lax.fori_loop(0, 2 * GATHER_TILE, wait_row, 0, unroll=8)


def _dispatch(ht, dest_blocks, pad_end, padded, cap):
    n = ht.shape[0] // TOKEN_CHUNKS
    grid_spec = pltpu.PrefetchScalarGridSpec(
        num_scalar_prefetch=2, grid=(n // GATHER_TILE,),
        in_specs=[pl.BlockSpec((1, 1, 2 * GATHER_TILE), lambda i, pe, pd: (i, 0, 0), memory_space=pltpu.SMEM),
                  pl.BlockSpec((GATHER_TILE * TOKEN_CHUNKS, LANES), lambda i, pe, pd: (i, 0))],
        out_specs=pl.BlockSpec(memory_space=pl.ANY),
        scratch_shapes=[pltpu.VMEM((MOE_TILE * TOKEN_CHUNKS, LANES), F32), pltpu.SemaphoreType.DMA(())])
    return pl.pallas_call(
        _dispatch_body, grid_spec=grid_spec, out_shape=jax.ShapeDtypeStruct((cap * TOKEN_CHUNKS, LANES), F32),
        compiler_params=_params("arbitrary"), name="moe_dispatch",
    )(pad_end, padded, dest_blocks, ht)


def _expert_body(blk_ref, exp_ref, used_ref, x_ref, wg_ref, wu_ref, wd_ref, y_ref, wg_s, wu_s, wd_s):
    i = pl.program_id(0)

    @pl.when((i == 0) | (exp_ref[i] != exp_ref[jnp.maximum(i - 1, 0)]))
    def _():
        wg_s[...] = wg_ref[0].astype(BF16)
        wu_s[...] = wu_ref[0].astype(BF16)
        wd_s[...] = wd_ref[0].astype(BF16)

    @pl.when(i < used_ref[0])
    def _():
        half = MOE_TILE // 2
        hidden = []
        for part in range(2):
            x = _load_token_tiles(x_ref, part * half * TOKEN_CHUNKS, half).astype(BF16)
            gate = jnp.dot(x, wg_s[...], preferred_element_type=F32)
            up = jnp.dot(x, wu_s[...], preferred_element_type=F32)
            hidden.append((gate, up))
        for part, (gate, up) in enumerate(hidden):
            u = gate * _sigmoid(gate) * up
            y = jnp.dot(u.astype(BF16), wd_s[...], preferred_element_type=F32)
            rows = y_ref.at[pl.ds(part * half * TOKEN_CHUNKS, half * TOKEN_CHUNKS), :]
            _store_token_tiles(rows, y)

    @pl.when(i >= used_ref[0])
    def _():
        y_ref[...] = jnp.zeros(y_ref.shape, y_ref.dtype)


def _experts(xb, blk, blk_expert, n_used, layer, w_gate, w_up, w_down):
    d = D_MODEL
    rows = pl.BlockSpec((MOE_TILE * TOKEN_CHUNKS, LANES), lambda i, blk, ex, used: (blk[i], 0))
    out_rows = pl.BlockSpec((MOE_TILE * TOKEN_CHUNKS, LANES), lambda i, blk, ex, used: (i, 0))
    grid_spec = pltpu.PrefetchScalarGridSpec(
        num_scalar_prefetch=3, grid=(xb.shape[0] // (MOE_TILE * TOKEN_CHUNKS),),
        in_specs=[rows,
                  pl.BlockSpec((None, 1, d, D_EXPERT), lambda i, blk, ex, used: (layer, ex[i], 0, 0)),
                  pl.BlockSpec((None, 1, d, D_EXPERT), lambda i, blk, ex, used: (layer, ex[i], 0, 0)),
                  pl.BlockSpec((None, 1, D_EXPERT, d), lambda i, blk, ex, used: (layer, ex[i], 0, 0))],
        out_specs=out_rows,
        scratch_shapes=[pltpu.VMEM((d, D_EXPERT), BF16), pltpu.VMEM((d, D_EXPERT), BF16),
                        pltpu.VMEM((D_EXPERT, d), BF16)])
    return pl.pallas_call(
        _expert_body, grid_spec=grid_spec, out_shape=jax.ShapeDtypeStruct(xb.shape, F32),
        compiler_params=_params("arbitrary"), name="moe_experts",
    )(blk, blk_expert, n_used, xb, w_gate, w_up, w_down)


def _combine_body(dest_ref, next_ref, h_ref, gate_ref, lg_ref, lb_ref, yb_ref, o_ref, rows_a, rows_b, sem_a, sem_b):
    step, last = pl.program_id(0), pl.num_programs(0) - 1

    def gather(idx_ref, rows, sem):
        def start_row(t, carry):
            _tile_copy(yb_ref, rows, idx_ref[0, 0, t], t, sem).start()
            return carry
        lax.fori_loop(0, 2 * GATHER_TILE, start_row, 0, unroll=8)

    def finish(rows, sem):
        def wait_row(t, carry):
            _tile_copy(yb_ref, rows, 0, 0, sem).wait()
            return carry
        lax.fori_loop(0, 2 * GATHER_TILE, wait_row, 0, unroll=8)
        gate = gate_ref[...]
        ffn = (gate[:, 0:1] * _load_token_tiles(rows, 0, GATHER_TILE)
               + gate[:, 1:2] * _load_token_tiles(rows, GATHER_TILE * TOKEN_CHUNKS, GATHER_TILE))
        o_ref[...] = _layer_norm(ALPHA * h_ref[...] + ffn, lg_ref[...], lb_ref[...])

    @pl.when(step == 0)
    def _():
        gather(dest_ref, rows_a, sem_a)

    for parity, (rows, sem, rows_next, sem_next) in enumerate(((rows_a, sem_a, rows_b, sem_b),
                                                               (rows_b, sem_b, rows_a, sem_a))):
        @pl.when(step % 2 == parity)
        def _(rows=rows, sem=sem, rows_next=rows_next, sem_next=sem_next):
            @pl.when(step < last)
            def _():
                gather(next_ref, rows_next, sem_next)
            finish(rows, sem)


def _combine(h, gate_rows, dest_blocks, yb, ln_g, ln_b):
    n, d = h.shape
    tm = GATHER_TILE
    steps = n // tm
    rows = pltpu.VMEM((2 * tm * TOKEN_CHUNKS, LANES), F32)
    return pl.pallas_call(
        _combine_body, grid=(steps,),
        in_specs=[pl.BlockSpec((1, 1, 2 * tm), lambda i: (i, 0, 0), memory_space=pltpu.SMEM),
                  pl.BlockSpec((1, 1, 2 * tm), lambda i: (jnp.minimum(i + 1, steps - 1), 0, 0), memory_space=pltpu.SMEM),
                  pl.BlockSpec((tm, d), lambda i: (i, 0)), pl.BlockSpec((tm, 2), lambda i: (i, 0)),
                  pl.BlockSpec((1, d), lambda i: (0, 0)), pl.BlockSpec((1, d), lambda i: (0, 0)),
                  pl.BlockSpec(memory_space=pl.ANY)],
        out_specs=pl.BlockSpec((tm, d), lambda i: (i, 0)), out_shape=jax.ShapeDtypeStruct((n, d), F32),
        scratch_shapes=[rows, rows, pltpu.SemaphoreType.DMA(()), pltpu.SemaphoreType.DMA(())],
        compiler_params=_params("arbitrary"), name="moe_combine",
    )(dest_blocks, dest_blocks, h, gate_rows, ln_g.reshape(1, d), ln_b.reshape(1, d), yb)


def _moe_layer(h, ht, routing, layer, w_gate, w_up, w_down, ln_g, ln_b):
    n, d = h.shape
    idx, gate, rank, cnt = routing
    counts = cnt[:, 0].astype(I32)
    padded = (counts + MOE_TILE - 1) // MOE_TILE * MOE_TILE
    pad_end = jnp.cumsum(padded).astype(I32)
    pad_start = pad_end - padded
    experts = jnp.arange(N_EXPERTS, dtype=I32)
    dest = jnp.sum(jnp.where(idx[..., None] == experts, pad_start, 0), axis=-1) + rank
    cap = 2 * n + N_EXPERTS * MOE_TILE
    n_blocks = cap // MOE_TILE
    n_used = pad_end[-1] // MOE_TILE
    blk = jnp.minimum(jnp.arange(n_blocks, dtype=I32), n_used - 1)
    blk_expert = jnp.minimum(jnp.sum(pad_end[None, :] <= (blk * MOE_TILE)[:, None], axis=1), N_EXPERTS - 1).astype(I32)
    tm = GATHER_TILE
    dest_blocks = dest.reshape(2, n // tm, tm).transpose(1, 0, 2).reshape(n // tm, 1, 2 * tm)
    xb = _dispatch(ht, dest_blocks, pad_end, padded, cap)
    yb = _experts(xb, blk, blk_expert, n_used.reshape(1).astype(I32), layer, w_gate, w_up, w_down)
    return _combine(h, gate.T, dest_blocks, yb, ln_g, ln_b)


def kernel(x, ln_g, ln_b, even_w_in, even_w_out, even_lambda, even_subln_g, even_rel_bias, odd_mix, odd_w_rkv, odd_w0, odd_w1, odd_w2, odd_a0, odd_a1, odd_a2, odd_g1, odd_g2, odd_k_k, odd_k_a, odd_r_k, odd_lnx_g, odd_lnx_b, odd_w_out, vres_v0, vres_v1, vres_v2, router_w, router_b, moe_w_gate, moe_w_up, moe_w_down):
    bn, seq, d = x.shape
    x2 = x.reshape(bn * seq, d)
    v_first = None
    rotary_tables = _rotary_tables(seq)
    for layer in range(DEPTH):
        if layer % 2 == 0:
            e = layer // 2
            lam_init = 0.8 - 0.6 * math.exp(-0.3 * layer)
            h, ht, routing = _even_mixer(x2, bn, seq, rotary_tables, even_w_in[e], even_w_out[e], even_lambda[e],
                                         even_subln_g[e], even_rel_bias[e], lam_init, ln_g[layer, 0], ln_b[layer, 0],
                                         router_w, router_b)
        else:
            o = layer // 2
            vres = None if o == 0 else (vres_v0[o - 1], vres_v1[o - 1], vres_v2[o - 1])
            h, ht, routing, v_first = _rwkv_mixer(
                x2, bn, seq, v_first, odd_mix[o], odd_w_rkv[o], odd_w0[o], odd_w1[o], odd_w2[o], odd_a0[o], odd_a1[o],
                odd_a2[o], odd_g1[o], odd_g2[o], odd_k_k[o], odd_k_a[o], odd_r_k[o], odd_lnx_g[o], odd_lnx_b[o],
                odd_w_out[o], vres, ln_g[layer, 0], ln_b[layer, 0], router_w, router_b)
        x2 = _moe_layer(h, ht, routing, layer, moe_w_gate, moe_w_up, moe_w_down, ln_g[layer, 1], ln_b[layer, 1])
    return x2.reshape(bn, seq, d)
```

```python
import functools
import math

import jax
import jax.numpy as jnp
from jax import lax
from jax.experimental import pallas as pl
from jax.experimental.pallas import tpu as pltpu

F32 = jnp.float32
BF16 = jnp.bfloat16
I32 = jnp.int32

LANES = 128
SUBLANES = 8
VMEM_LIMIT_BYTES = 56 * 1024 * 1024

D_MODEL = 1024
DEPTH = 4
CHUNK = 64
A_HEADS = 4
A_DIM = 64
A_VDIM = 128
ROPE_DIM = 16
ROPE_THETA = 500000.0
SUBLN_EPS = 1e-5
B_HEADS = 8
B_DIM = 64
B_CHUNKS_BACK = 8
REL_CLIP = 128
BAND = (B_CHUNKS_BACK + 1) * CHUNK
SEG = 512
C_DIM = 64
LNX_EPS = 1e-5 * C_DIM
N_EXPERTS = 32
N_GROUPS = 4
EXPERTS_PER_GROUP = 8
D_EXPERT = 512
LN_EPS = 1e-5
ALPHA = (2 * DEPTH) ** 0.25
NEG_INF = -1e30

ROW_TILE = 512
ATT_TILE = 2048
ATT_KEYS = 256
V_ROWS = A_VDIM + 16
LOG2_E = math.log2(math.e)
Q_SCALE = A_DIM ** -0.5 * LOG2_E
BAND_TILE = B_CHUNKS_BACK * CHUNK
WKV_ROWS = 512
MOE_TILE = 512
GATHER_TILE = 512
TOKEN_CHUNKS = D_MODEL // LANES

_NT = (((1,), (1,)), ((), ()))
_TN = (((0,), (0,)), ((), ()))


def _params(*sem):
    return pltpu.CompilerParams(dimension_semantics=sem, vmem_limit_bytes=VMEM_LIMIT_BYTES)


def _dot(a, b):
    return jnp.dot(a.astype(BF16), b.astype(BF16), preferred_element_type=F32)


def _layer_norm(x, g, b):
    mu = jnp.mean(x, axis=-1, keepdims=True)
    xc = x - mu
    var = jnp.mean(xc * xc, axis=-1, keepdims=True)
    return xc * lax.rsqrt(var + LN_EPS) * g + b


def _sigmoid(x):
    return 1.0 / (1.0 + jnp.exp(-x))


def _row_call(body, rows, fulls, outs, tm, name):
    m = rows[0].shape[0] if not isinstance(rows[0], tuple) else None
    in_specs, args = [], []
    for r in rows:
        if isinstance(r, tuple):
            arr, fn, block_rows = r
            in_specs.append(pl.BlockSpec((block_rows, arr.shape[1]), lambda i, fn=fn: (fn(i), 0)))
        else:
            arr = r
            m = arr.shape[0]
            in_specs.append(pl.BlockSpec((tm, arr.shape[1]), lambda i: (i, 0)))
        args.append(arr)
    for f in fulls:
        in_specs.append(pl.BlockSpec(f.shape, lambda i, nd=f.ndim: (0,) * nd))
        args.append(f)
    out_specs = [pl.BlockSpec((tm, w), lambda i: (i, 0)) for w, _ in outs]
    out_shape = [jax.ShapeDtypeStruct((m, w), dt) for w, dt in outs]
    return pl.pallas_call(
        body, grid=(m // tm,), in_specs=in_specs, out_specs=out_specs, out_shape=out_shape,
        compiler_params=_params("parallel"), name=name)(*args)


def _out_call(body, rows, fulls, router_w, router_b, name):
    n, tm = rows[0].shape[0], ROW_TILE
    upper = (jnp.arange(tm)[:, None] < jnp.arange(tm)[None, :]).astype(BF16)
    fulls = list(fulls) + [router_w.T.astype(F32), router_b.reshape(N_EXPERTS, 1).astype(F32), upper]
    in_specs = [pl.BlockSpec((tm, r.shape[1]), lambda i: (i, 0)) for r in rows]
    in_specs += [pl.BlockSpec(f.shape, lambda i, nd=f.ndim: (0,) * nd) for f in fulls]
    tok = pl.BlockSpec((2, tm), lambda i: (0, i))
    h, ht, *routing = pl.pallas_call(
        body, grid=(n // tm,), in_specs=in_specs,
        out_specs=[pl.BlockSpec((tm, D_MODEL), lambda i: (i, 0)), pl.BlockSpec((tm * TOKEN_CHUNKS, LANES), lambda i: (i, 0)),
                   tok, tok, tok, pl.BlockSpec((N_EXPERTS, 1), lambda i: (0, 0))],
        out_shape=[jax.ShapeDtypeStruct((n, D_MODEL), F32), jax.ShapeDtypeStruct((n * TOKEN_CHUNKS, LANES), F32),
                   jax.ShapeDtypeStruct((2, n), I32), jax.ShapeDtypeStruct((2, n), F32),
                   jax.ShapeDtypeStruct((2, n), I32), jax.ShapeDtypeStruct((N_EXPERTS, 1), F32)],
        scratch_shapes=[pltpu.VMEM((N_EXPERTS, 1), F32)],
        compiler_params=_params("arbitrary"), name=name)(*rows, *fulls)
    return h, ht, routing


def _even_proj_body(x_ref, c_ref, sm_ref, sp_ref, ct_ref, smt_ref, spt_ref, w_ref, wt_ref,
                    aq_ref, ak_ref, av_ref, bq_ref, bk_ref, bv_ref):
    x = x_ref[...]
    xb = x.astype(BF16)
    half = ROPE_DIM // 2

    def seg(j):
        return jnp.dot(xb, w_ref[:, j * SEG:(j + 1) * SEG], preferred_element_type=F32)

    qv_t = jnp.dot(wt_ref[...], x.T.astype(BF16), preferred_element_type=F32)
    cos, s_minus, s_plus = ct_ref[...], smt_ref[...], spt_ref[...]
    for j in range(SEG // LANES):
        u = qv_t[j * LANES:(j + 1) * LANES, :]
        rot = u * cos + pltpu.roll(u, LANES - half, 0) * s_minus + pltpu.roll(u, half, 0) * s_plus
        aq_ref[j * LANES:(j + 1) * LANES, :] = (rot * Q_SCALE).astype(aq_ref.dtype)
    for v_ref, base in ((av_ref, SEG), (bv_ref, 3 * SEG)):
        for h in range(SEG // LANES):
            v_ref[h * V_ROWS:h * V_ROWS + LANES, :] = qv_t[base + h * LANES:base + (h + 1) * LANES, :].astype(v_ref.dtype)
            v_ref[h * V_ROWS + LANES:(h + 1) * V_ROWS, :] = jnp.ones((V_ROWS - LANES, x.shape[0]), v_ref.dtype)
    bq_ref[...] = (qv_t[2 * SEG:3 * SEG, :] * (B_DIM ** -0.5 * LOG2_E)).astype(bq_ref.dtype)
    cos, s_minus, s_plus = c_ref[...], sm_ref[...], sp_ref[...]
    k = seg(1)
    for j in range(SEG // LANES):
        u = k[:, j * LANES:(j + 1) * LANES]
        rot = u * cos + pltpu.roll(u, LANES - half, 1) * s_minus + pltpu.roll(u, half, 1) * s_plus
        ak_ref[:, j * LANES:(j + 1) * LANES] = rot.astype(ak_ref.dtype)
    bk_ref[...] = seg(4).astype(bk_ref.dtype)


def _rotary_tables(seq):
    half = ROPE_DIM // 2
    inv = ROPE_THETA ** (-jnp.arange(half, dtype=F32) / half)
    ang = jnp.arange(seq, dtype=F32)[:, None] * inv
    cos, sin = jnp.cos(ang), jnp.sin(ang)
    d = jnp.arange(LANES) % A_DIM
    first, second = d < half, (d >= half) & (d < ROPE_DIM)
    col = jnp.where(first, d, d - half) % half
    cos_t = jnp.where(first | second, cos[:, col], 1.0)
    s_minus = jnp.where(first, -sin[:, col], 0.0)
    s_plus = jnp.where(second, sin[:, col], 0.0)
    return cos_t, s_minus, s_plus


def _even_proj(x2, w_in, tabs, seq):
    n, d = x2.shape
    tm = ROW_TILE
    blocks = seq // tm
    w = w_in.astype(BF16)
    seg = lambda j: w[:, j * SEG:(j + 1) * SEG]
    w_qv_t = jnp.concatenate([seg(0), seg(2), seg(3), seg(5)], axis=1).T
    tabs_t = [t.T for t in tabs]
    rows = pl.BlockSpec((tm, SEG), lambda i: (i, 0))
    cols = lambda r: pl.BlockSpec((r, tm), lambda i: (0, i))
    full = lambda a: pl.BlockSpec(a.shape, lambda i: (0, 0))
    v_rows = SEG // LANES * V_ROWS
    shapes = [(SEG, n), (n, SEG), (v_rows, n), (SEG, n), (n, SEG), (v_rows, n)]
    return pl.pallas_call(
        _even_proj_body, grid=(n // tm,),
        in_specs=[pl.BlockSpec((tm, d), lambda i: (i, 0))]
        + [pl.BlockSpec((tm, LANES), lambda i: (i % blocks, 0))] * 3
        + [pl.BlockSpec((LANES, tm), lambda i: (0, i % blocks))] * 3 + [full(w), full(w_qv_t)],
        out_specs=[cols(SEG), rows, cols(v_rows), cols(SEG), rows, cols(v_rows)],
        out_shape=[jax.ShapeDtypeStruct(s, BF16) for s in shapes],
        compiler_params=_params("parallel"), name="even_proj")(x2, *tabs, *tabs_t, w, w_qv_t)


def _diff_attn_body(qi_ref, kj_ref, lam_ref, q_ref, k_ref, v_ref, g_ref, o_ref, q2_s, m_s, l_s, acc_s, *, lam_init):
    p = pl.program_id(2)
    qi, kj = qi_ref[p], kj_ref[p]
    groups = ATT_TILE // LANES

    @pl.when(kj == 0)
    def _():
        feat = lax.broadcasted_iota(I32, (LANES, LANES), 0)
        for j in range(groups):
            q = q_ref[:, j * LANES:(j + 1) * LANES]
            zero = jnp.zeros_like(q)
            q2_s[j, :, 0:LANES] = jnp.where(feat < A_DIM, q, zero)
            q2_s[j, :, LANES:] = jnp.where(feat >= A_DIM, q, zero)
        m_s[...] = jnp.full(m_s.shape, -jnp.inf, F32)
        l_s[...] = jnp.zeros(l_s.shape, F32)
        acc_s[...] = jnp.zeros(acc_s.shape, F32)

    def step(diagonal):
        pieces = range(ATT_TILE // ATT_KEYS)

        def live(j, i):
            return not diagonal or i * ATT_KEYS // CHUNK <= ((j + 1) * LANES - 1) // CHUNK

        def scores(j, i):
            if not live(j, i):
                return None
            s = jnp.dot(k_ref[0, i * ATT_KEYS:(i + 1) * ATT_KEYS, :], q2_s[j], preferred_element_type=F32)
            if diagonal:
                key_chunk = (lax.broadcasted_iota(I32, s.shape, 0) + i * ATT_KEYS) // CHUNK
                query_chunk = (lax.broadcasted_iota(I32, s.shape, 1) % LANES + j * LANES) // CHUNK
                s = jnp.where(key_chunk <= query_chunk, s, NEG_INF)
            return s

        def stats(j, s):
            m = m_s[j]
            m_new = m
            for piece in s:
                if piece is not None:
                    m_new = jnp.maximum(m_new, jnp.max(piece, axis=0, keepdims=True))
            m_s[j] = m_new
            return jnp.exp2(m - m_new), m_new

        def values(pr, i):
            if pr[i] is None:
                return None
            return jnp.dot(v_ref[:, i * ATT_KEYS:(i + 1) * ATT_KEYS], pr[i], preferred_element_type=F32)

        def accumulate(j, alpha, pv):
            pv = functools.reduce(lambda a, b: a + b, [x for x in pv if x is not None])
            acc_s[j] = alpha * acc_s[j] + pv[0:A_VDIM]
            l_s[j] = alpha * l_s[j] + pv[A_VDIM:A_VDIM + 1]

        s = [scores(0, i) for i in pieces]
        alpha, m_new = stats(0, s)
        pending = None
        for j in range(groups):
            s_next, pr, pv = [], [], []
            for i in pieces:
                if j + 1 < groups:
                    s_next.append(scores(j + 1, i))
                pr.append(None if s[i] is None else jnp.exp2(s[i] - m_new).astype(BF16))
                if pending is not None:
                    pv.append(values(pending[2], i))
            if pending is not None:
                accumulate(pending[0], pending[1], pv)
            pending = (j, alpha, pr)
            if j + 1 < groups:
                s = s_next
                alpha, m_new = stats(j + 1, s)
        accumulate(pending[0], pending[1], [values(pending[2], i) for i in pieces])

    @pl.when(kj != qi)
    def _():
        step(False)

    @pl.when(kj == qi)
    def _():
        step(True)
        for j in range(groups):
            o = acc_s[j] / l_s[j]
            o = o[:, 0:LANES] - lam_ref[0] * o[:, LANES:]
            od = o.T
            od = od * lax.rsqrt(jnp.mean(od * od, axis=-1, keepdims=True) + SUBLN_EPS) * g_ref[...] * (1.0 - lam_init)
            o_ref[0, j * LANES:(j + 1) * LANES, :] = od.astype(o_ref.dtype)


def _diff_attn(aq_t, ak, av_t, lam, subln_g, lam_init):
    bn, seq, _ = ak.shape
    t = ATT_TILE
    nblk = seq // t
    pairs = [(i, j) for i in range(nblk) for j in range(i + 1)]
    qi = jnp.asarray([p[0] for p in pairs], I32)
    kj = jnp.asarray([p[1] for p in pairs], I32)
    groups, width = t // LANES, 2 * LANES
    grid_spec = pltpu.PrefetchScalarGridSpec(
        num_scalar_prefetch=3, grid=(bn, A_HEADS, len(pairs)),
        in_specs=[
            pl.BlockSpec((LANES, t), lambda b, h, p, qi, kj, lam: (h, b * nblk + qi[p])),
            pl.BlockSpec((1, t, LANES), lambda b, h, p, qi, kj, lam: (b, kj[p], h)),
            pl.BlockSpec((V_ROWS, t), lambda b, h, p, qi, kj, lam: (h, b * nblk + kj[p])),
            pl.BlockSpec((1, LANES), lambda b, h, p, qi, kj, lam: (0, 0)),
        ],
        out_specs=pl.BlockSpec((1, t, LANES), lambda b, h, p, qi, kj, lam: (b, qi[p], h)),
        scratch_shapes=[pltpu.VMEM((groups, LANES, width), BF16), pltpu.VMEM((groups, 1, width), F32),
                        pltpu.VMEM((groups, 1, width), F32), pltpu.VMEM((groups, A_VDIM, width), F32)])
    return pl.pallas_call(
        functools.partial(_diff_attn_body, lam_init=lam_init), grid_spec=grid_spec,
        out_shape=jax.ShapeDtypeStruct((bn, seq, A_HEADS * A_VDIM), BF16),
        compiler_params=_params("parallel", "parallel", "arbitrary"), name="diff_attn",
    )(qi, kj, lam.reshape(1).astype(F32), aq_t, ak, av_t, subln_g.reshape(1, A_VDIM))


def _chunk_attn_body(q_ref, kp_ref, kc_ref, vp_ref, vc_ref, bias_ref, o_ref, kcat, vcat):
    i = pl.program_id(1)
    t = BAND_TILE
    kcat[0:t, :] = kp_ref[0]
    kcat[t:, :] = kc_ref[0]
    vcat[:, 0:t] = vp_ref[...]
    vcat[:, t:] = vc_ref[...]
    pair = 2 * CHUNK
    keys = BAND + CHUNK
    feat = lax.broadcasted_iota(I32, (LANES, pair), 0)
    key = lax.broadcasted_iota(I32, (keys, pair), 0)
    items = [(cp, hp, hh) for cp in range(t // pair) for hp in range(B_HEADS // 2) for hh in range(2)]

    def scores(item):
        cp, hp, hh = item
        q = q_ref[hp * LANES:(hp + 1) * LANES, cp * pair:(cp + 1) * pair]
        q = jnp.where((feat < B_DIM) if hh == 0 else (feat >= B_DIM), q, jnp.zeros_like(q))
        s = jnp.dot(kcat[cp * pair:cp * pair + keys, hp * LANES:(hp + 1) * LANES], q, preferred_element_type=F32)
        s = s + bias_ref[2 * hp + hh]
        return jnp.where((i > 0) | (key + cp * pair >= t), s, NEG_INF)

    s_next = scores(items[0])
    halves = []
    for n, (cp, hp, hh) in enumerate(items):
        s = s_next
        if n + 1 < len(items):
            s_next = scores(items[n + 1])
        pr = jnp.exp2(s - jnp.max(s, axis=0, keepdims=True)).astype(BF16)
        pv = jnp.dot(vcat[hp * V_ROWS:(hp + 1) * V_ROWS, cp * pair:cp * pair + keys], pr, preferred_element_type=F32)
        halves.append(pv[hh * B_DIM:(hh + 1) * B_DIM] / pv[LANES:LANES + 1])
        if hh == 1:
            out = jnp.concatenate(halves, axis=0).T
            o_ref[0, cp * pair:(cp + 1) * pair, hp * LANES:(hp + 1) * LANES] = out.astype(o_ref.dtype)
            halves = []


def _chunk_attn(bq_t, bk, bv_t, rel_bias):
    bn, seq, width = bk.shape
    t = BAND_TILE
    nblk = seq // t
    far = jnp.broadcast_to(rel_bias[:, 2 * REL_CLIP:], (B_HEADS, BAND - REL_CLIP))
    f = jnp.concatenate([far, rel_bias[:, REL_CLIP - CHUNK + 1:2 * REL_CLIP][:, ::-1]], axis=1) * LOG2_E
    span = f.shape[1]
    flat = jnp.broadcast_to(f[:, None, :], (B_HEADS, CHUNK, span)).reshape(B_HEADS, CHUNK * span)
    skew = flat[:, CHUNK - 1:CHUNK - 1 + CHUNK * (span - 1)].reshape(B_HEADS, CHUNK, span - 1)
    bias_t = skew[:, :, 0:BAND].transpose(0, 2, 1)
    out_of_band = jnp.full((B_HEADS, CHUNK, CHUNK), NEG_INF, F32)
    bias2 = jnp.concatenate([jnp.concatenate([bias_t, out_of_band], axis=1),
                             jnp.concatenate([out_of_band, bias_t], axis=1)], axis=2).astype(F32)
    cur = pl.BlockSpec((1, t, width), lambda b, i: (b, i, 0))
    prev = pl.BlockSpec((1, t, width), lambda b, i: (b, jnp.maximum(i - 1, 0), 0))
    v_rows = bv_t.shape[0]
    cur_t = lambda r: pl.BlockSpec((r, t), lambda b, i: (0, b * nblk + i))
    prev_t = lambda r: pl.BlockSpec((r, t), lambda b, i: (0, b * nblk + jnp.maximum(i - 1, 0)))
    return pl.pallas_call(
        _chunk_attn_body, grid=(bn, nblk),
        in_specs=[cur_t(width), prev, cur, prev_t(v_rows), cur_t(v_rows), pl.BlockSpec(bias2.shape, lambda b, i: (0, 0, 0))],
        out_specs=cur, out_shape=jax.ShapeDtypeStruct((bn, seq, width), BF16),
        scratch_shapes=[pltpu.VMEM((2 * t, width), BF16), pltpu.VMEM((v_rows, 2 * t), BF16)],
        compiler_params=_params("parallel", "parallel"), name="chunk_attn",
    )(bq_t, bk, bk, bv_t, bv_t, bias2)


def _even_out_body(x_ref, ya_ref, yb_ref, w_ref, g_ref, b_ref, wt_ref, rb_ref, upper_ref, h_ref, ht_ref, *route_refs):
    mix = jnp.dot(ya_ref[...], w_ref[0:SEG, :], preferred_element_type=F32)
    mix = mix + jnp.dot(yb_ref[...], w_ref[SEG:, :], preferred_element_type=F32)
    h = _layer_norm(ALPHA * x_ref[...] + mix, g_ref[...], b_ref[...])
    h_ref[...] = h
    _store_token_tiles(ht_ref, h)
    _route(h, wt_ref, rb_ref, upper_ref, *route_refs)


def _even_mixer(x2, bn, seq, rotary_tables, w_in, w_out, lam_p, subln_g, rel_bias, lam_init, ln_g, ln_b,
                router_w, router_b):
    aq_t, ak, av_t, bq_t, bk, bv_t = _even_proj(x2, w_in, rotary_tables, seq)
    lp = lam_p.astype(F32)
    lam = jnp.exp(jnp.sum(lp[0] * lp[1])) - jnp.exp(jnp.sum(lp[2] * lp[3])) + lam_init
    r3 = lambda a: a.reshape(bn, seq, SEG)
    ya = _diff_attn(aq_t, r3(ak), av_t, lam, subln_g, lam_init)
    yb = _chunk_attn(bq_t, r3(bk), bv_t, rel_bias)
    return _out_call(_even_out_body, [x2, ya.reshape(-1, SEG), yb.reshape(-1, SEG)],
                     [w_out.astype(BF16), ln_g.reshape(1, -1), ln_b.reshape(1, -1)], router_w, router_b, "even_out")


def _rwkv_pre_body(*refs, has_vres, blocks_per_seq):
    if has_vres:
        (x_ref, xp_ref, vf_ref, mix_ref, wr_ref, wk_ref, wv_ref, w0_ref, w1_ref, w2_ref, a0_ref, a1_ref, a2_ref,
         g1_ref, g2_ref, kk_ref, ka_ref, v0_ref, v1_ref, v2_ref, r_o, lw_o, k_o, v_o, kkr_o, a_o, g_o) = refs
    else:
        (x_ref, xp_ref, mix_ref, wr_ref, wk_ref, wv_ref, w0_ref, w1_ref, w2_ref, a0_ref, a1_ref, a2_ref,
         g1_ref, g2_ref, kk_ref, ka_ref, r_o, lw_o, k_o, v_o, kkr_o, a_o, g_o) = refs
    x = x_ref[...]
    first_of_seq = (pl.program_id(0) % blocks_per_seq) == 0
    before = jnp.where(first_of_seq, 0.0, xp_ref[SUBLANES - 1:SUBLANES, :])
    row = lax.broadcasted_iota(I32, x.shape, 0)
    xx = jnp.where(row == 0, before, pltpu.roll(x, 1, 0)) - x
    mixed = lambda i: (x + xx * mix_ref[i:i + 1, :]).astype(BF16)
    r = _dot(mixed(0), wr_ref[...])
    k = _dot(mixed(2), wk_ref[...])
    xv = mixed(3)
    v = _dot(xv, wv_ref[...])
    z = w0_ref[...] + _dot(jnp.tanh(_dot(mixed(1), w1_ref[...])), w2_ref[...])
    softplus = jnp.maximum(-z, 0.0) + jnp.log(1.0 + jnp.exp(-jnp.abs(z)))
    lw_o[...] = -jnp.exp(-softplus - 0.5)
    if has_vres:
        v = v + (vf_ref[...].astype(F32) - v) * _sigmoid(v0_ref[...] + _dot(_dot(xv, v1_ref[...]), v2_ref[...]))
    a = _sigmoid(a0_ref[...] + _dot(_dot(mixed(4), a1_ref[...]), a2_ref[...]))
    g_o[...] = _dot(_sigmoid(_dot(mixed(5), g1_ref[...])), g2_ref[...]).astype(g_o.dtype)
    r_o[...] = r.astype(r_o.dtype)
    v_o[...] = v.astype(v_o.dtype)
    a_o[...] = a.astype(a_o.dtype)
    kkr_o[...] = (k * kk_ref[...]).astype(kkr_o.dtype)
    k_o[...] = (k * (1.0 + (a - 1.0) * ka_ref[...])).astype(k_o.dtype)


def _head_sum(x, first):
    zero = jnp.zeros_like(x)
    s0 = jnp.sum(jnp.where(first, x, zero), axis=1, keepdims=True)
    s1 = jnp.sum(jnp.where(first, zero, x), axis=1, keepdims=True)
    return jnp.where(first, s0, s1)


def _wkv_body(r_ref, lw_ref, k_ref, v_ref, kk_ref, a_ref, rk_ref, lg_ref, lb_ref, y_ref, z_s, *bufs):
    step = pl.program_id(2)
    half = len(bufs) // 2

    @pl.when(step == 0)
    def _():
        z_s[...] = jnp.zeros(z_s.shape, F32)
        for buf in bufs[half:]:
            buf[...] = jnp.zeros(buf.shape, buf.dtype)

    @pl.when(step % 2 == 0)
    def _():
        _wkv_step(r_ref, lw_ref, k_ref, v_ref, kk_ref, a_ref, rk_ref, lg_ref, lb_ref, y_ref, z_s,
                  bufs[:half], bufs[half:])

    @pl.when(step % 2 == 1)
    def _():
        _wkv_step(r_ref, lw_ref, k_ref, v_ref, kk_ref, a_ref, rk_ref, lg_ref, lb_ref, y_ref, z_s,
                  bufs[half:], bufs[:half])


def _wkv_step(r_ref, lw_ref, k_ref, v_ref, kk_ref, a_ref, rk_ref, lg_ref, lb_ref, y_ref, z_s, fill, drain):
    n = CHUNK
    lhs_f, y0_f, sadd_f, decay_f, bonus_f = fill
    lhs_d, y0_d, sadd_d, decay_d, bonus_d = drain
    first = lax.broadcasted_iota(I32, (n, LANES), 1) < C_DIM
    row = lax.broadcasted_iota(I32, (2 * n, 2 * n), 0)
    col = lax.broadcasted_iota(I32, (2 * n, 2 * n), 1)
    same = (row // n) == (col // n)
    strict = same & ((col % n) < (row % n))
    incl = same & ((col % n) <= (row % n))
    eye = (row == col).astype(F32)
    tri = (lax.broadcasted_iota(I32, (n, n), 1) <= lax.broadcasted_iota(I32, (n, n), 0)).astype(BF16)

    def stack(x):
        zero = jnp.zeros_like(x)
        return jnp.concatenate([jnp.where(first, x, zero), jnp.where(first, zero, x)], axis=0)

    zero = jnp.zeros((2 * n, 2 * n), F32)
    chunks = range(WKV_ROWS // n)
    state = {"z": z_s[...]}
    y2 = []

    def pipeline(c):
        z = state["z"]
        both = jnp.dot(lhs_d[c], z.astype(BF16), preferred_element_type=F32)
        y2.append(both[0:2 * n] + y0_d[c])
        state["z"] = decay_d[c] * z + both[2 * n:] + sadd_d[c]
        rows = slice(c * n, (c + 1) * n)
        r, lw, k, v, kk, a = (ref[0, rows, :].astype(F32) for ref in (r_ref, lw_ref, k_ref, v_ref, kk_ref, a_ref))
        kk = kk / jnp.maximum(jnp.sqrt(_head_sum(kk * kk, first)), 1e-12)
        hi = lw.astype(BF16)
        rem = lw - hi.astype(F32)
        mid = rem.astype(BF16)
        low = (rem - mid.astype(F32)).astype(BF16)
        cum = (jnp.dot(tri, hi, preferred_element_type=F32) + jnp.dot(tri, mid, preferred_element_type=F32)
               + jnp.dot(tri, low, preferred_element_type=F32))
        total = cum[n - 1:n, :]
        g_inv = jnp.exp(-cum)
        g_tail = jnp.exp(total - cum)
        a2 = stack(-kk * jnp.exp(cum - lw))
        r2 = stack(r * jnp.exp(cum))
        b2 = stack(kk * a * g_inv)
        k2 = stack(k * g_inv)
        m = lax.dot_general(jnp.concatenate([a2, r2], axis=0).astype(BF16),
                            jnp.concatenate([b2, k2], axis=0).astype(BF16), _NT, preferred_element_type=F32)
        v2 = stack(v)
        bonus_f[rows, :] = _head_sum(r * k * rk_ref[...], first) * v
        decay_f[c] = jnp.broadcast_to(jnp.sum(eye * jnp.exp(total), axis=1, keepdims=True), (2 * n, 2 * n))
        tails_t = jnp.concatenate([stack(kk * a * g_tail), stack(k * g_tail)], axis=0).T.astype(BF16)
        a_ab = jnp.where(strict, m[0:2 * n, 0:2 * n], zero)
        a_ak = jnp.where(strict, m[0:2 * n, 2 * n:], zero)
        a_r = jnp.concatenate([jnp.where(incl, m[2 * n:, 0:2 * n], zero),
                               jnp.where(incl, m[2 * n:, 2 * n:], zero)], axis=1).astype(BF16)
        yield
        inv = eye + a_ab
        power = a_ab.astype(BF16)
        power = jnp.dot(power, power, preferred_element_type=F32).astype(BF16)
        akv = _dot(a_ak, v2)
        yield
        for _ in range(int(math.log2(n)) - 2):
            both = jnp.dot(jnp.concatenate([power, inv.astype(BF16)], axis=0), power, preferred_element_type=F32)
            power = both[0:2 * n].astype(BF16)
            inv = inv + both[2 * n:]
            yield
        inv = inv + jnp.dot(inv.astype(BF16), power, preferred_element_type=F32)
        yield
        wu = _dot(inv, jnp.concatenate([a2, akv], axis=1))
        yield
        rhs = jnp.concatenate([wu, jnp.concatenate([zero, v2], axis=1)], axis=0).astype(BF16)
        x = jnp.dot(jnp.concatenate([a_r, tails_t], axis=0), rhs, preferred_element_type=F32)
        yield
        lhs_f[c] = jnp.concatenate([r2 + x[0:2 * n, 0:2 * n], x[2 * n:, 0:2 * n]], axis=0).astype(BF16)
        y0_f[c] = x[0:2 * n, 2 * n:]
        sadd_f[c] = x[2 * n:, 2 * n:]

    running = [pipeline(c) for c in chunks]
    for _ in range(16):
        running = [g for g in running if next(g, "done") != "done"]
    assert not running
    z_s[...] = state["z"]
    for c in chunks:
        rows = slice(c * n, (c + 1) * n)
        y = y2[c][0:n] + y2[c][n:]
        mu = _head_sum(y, first) * (1.0 / C_DIM)
        yc = y - mu
        var = _head_sum(yc * yc, first) * (1.0 / C_DIM)
        y_ref[0, rows, :] = (yc * lax.rsqrt(var + LNX_EPS) * lg_ref[...] + lb_ref[...] + bonus_d[rows, :]).astype(y_ref.dtype)


def _wkv(r, lw, k, v, kkr, a, r_k, lnx_g, lnx_b):
    bn, seq, d = r.shape
    nblk, nc, n2 = seq // WKV_ROWS, WKV_ROWS // CHUNK, 2 * CHUNK
    blk = pl.BlockSpec((1, WKV_ROWS, LANES), lambda b, h, c: (b, jnp.minimum(c, nblk - 1), h))
    out = pl.BlockSpec((1, WKV_ROWS, LANES), lambda b, h, c: (b, jnp.maximum(c - 1, 0), h))
    par = pl.BlockSpec((1, LANES), lambda b, h, c: (0, h))
    buffers = [pltpu.VMEM((nc, 2 * n2, n2), BF16), pltpu.VMEM((nc, n2, n2), F32), pltpu.VMEM((nc, n2, n2), F32),
               pltpu.VMEM((nc, n2, n2), F32), pltpu.VMEM((WKV_ROWS, LANES), F32)]
    return pl.pallas_call(
        _wkv_body, grid=(bn, d // LANES, nblk + 1),
        in_specs=[blk] * 6 + [par] * 3, out_specs=out,
        out_shape=jax.ShapeDtypeStruct((bn, seq, d), BF16),
        scratch_shapes=[pltpu.VMEM((LANES, LANES), F32)] + buffers + buffers,
        compiler_params=_params("parallel", "parallel", "arbitrary"), name="wkv7",
    )(r, lw, k, v, kkr, a, r_k.reshape(1, d), lnx_g.reshape(1, d), lnx_b.reshape(1, d))


def _odd_out_body(x_ref, y_ref, g_ref, w_ref, lg_ref, lb_ref, wt_ref, rb_ref, upper_ref, h_ref, ht_ref, *route_refs):
    mix = _dot(y_ref[...].astype(F32) * g_ref[...].astype(F32), w_ref[...])
    h = _layer_norm(ALPHA * x_ref[...] + mix, lg_ref[...], lb_ref[...])
    h_ref[...] = h
    _store_token_tiles(ht_ref, h)
    _route(h, wt_ref, rb_ref, upper_ref, *route_refs)


def _rwkv_mixer(x2, bn, seq, v_first, mix, w_rkv, w0, w1, w2, a0, a1, a2, g1, g2, k_k, k_a, r_k, lnx_g, lnx_b,
                w_out, vres, ln_g, ln_b, router_w, router_b):
    d = x2.shape[1]
    row = lambda p: p.reshape(1, -1)
    bf = lambda p: p.astype(BF16)
    groups = ROW_TILE // SUBLANES
    rows = [x2, (x2, lambda i: jnp.maximum(i * groups - 1, 0), SUBLANES)] + ([v_first] if vres is not None else [])
    fulls = [mix, bf(w_rkv[0]), bf(w_rkv[1]), bf(w_rkv[2]), row(w0), bf(w1), bf(w2), row(a0), bf(a1), bf(a2),
             bf(g1), bf(g2), row(k_k), row(k_a)]
    if vres is not None:
        fulls += [row(vres[0]), bf(vres[1]), bf(vres[2])]
    body = functools.partial(_rwkv_pre_body, has_vres=vres is not None, blocks_per_seq=seq // ROW_TILE)
    outs = [(d, BF16), (d, F32)] + [(d, BF16)] * 5
    r, lw, k, v, kkr, a, g = _row_call(body, rows, fulls, outs, ROW_TILE, "rwkv_pre")
    if vres is None:
        v_first = v
    r3 = lambda t: t.reshape(bn, seq, d)
    y = _wkv(r3(r), r3(lw), r3(k), r3(v), r3(kkr), r3(a), r_k, lnx_g, lnx_b)
    h, ht, routing = _out_call(_odd_out_body, [x2, y.reshape(-1, d), g], [bf(w_out), row(ln_g), row(ln_b)],
                               router_w, router_b, "odd_out")
    return h, ht, routing, v_first


def _route(h, wt_ref, b_ref, upper_ref, idx_ref, gate_ref, rank_ref, cnt_ref, carry_s):
    @pl.when(pl.program_id(0) == 0)
    def _():
        carry_s[...] = jnp.zeros(carry_s.shape, F32)

    logits = lax.dot_general(wt_ref[...], h, _NT, precision=lax.Precision.HIGHEST,
                             preferred_element_type=F32)
    scores = _sigmoid(logits)
    sel = scores + b_ref[...]
    sub = lax.broadcasted_iota(I32, sel.shape, 0)
    grp = sub // EXPERTS_PER_GROUP
    ninf = jnp.full(sel.shape, -jnp.inf, F32)
    big = jnp.full(sel.shape, N_EXPERTS, I32)

    def top2(vals):
        m1 = jnp.max(vals, axis=0, keepdims=True)
        i1 = jnp.min(jnp.where(vals == m1, sub, big), axis=0, keepdims=True)
        rest = jnp.where(sub == i1, ninf, vals)
        m2 = jnp.max(rest, axis=0, keepdims=True)
        i2 = jnp.min(jnp.where(rest == m2, sub, big), axis=0, keepdims=True)
        return m1, i1, m2, i2

    best = jnp.zeros((1, sel.shape[1]), I32)
    best_score = None
    for g in range(N_GROUPS):
        m1, _, m2, _ = top2(jnp.where(grp == g, sel, ninf))
        score = m1 + m2
        if g == 0:
            best_score = score
        else:
            better = score > best_score
            best = jnp.where(better, g, best)
            best_score = jnp.where(better, score, best_score)
    _, i1, _, i2 = top2(jnp.where(grp == best, sel, ninf))
    oh1, oh2 = sub == i1, sub == i2
    zero = jnp.zeros_like(scores)
    g1 = jnp.sum(jnp.where(oh1, scores, zero), axis=0, keepdims=True)
    g2 = jnp.sum(jnp.where(oh2, scores, zero), axis=0, keepdims=True)
    den = g1 + g2
    idx_ref[...] = jnp.concatenate([i1, i2], axis=0)
    gate_ref[...] = jnp.concatenate([g1 / den, g2 / den], axis=0)
    both = (oh1 | oh2).astype(BF16)
    before = jnp.dot(both, upper_ref[...], preferred_element_type=F32) + carry_s[...]
    rank1 = jnp.sum(jnp.where(oh1, before, zero), axis=0, keepdims=True)
    rank2 = jnp.sum(jnp.where(oh2, before, zero), axis=0, keepdims=True)
    rank_ref[...] = jnp.concatenate([rank1, rank2], axis=0).astype(I32)
    carry_s[...] = carry_s[...] + jnp.sum(both.astype(F32), axis=1, keepdims=True)
    cnt_ref[...] = carry_s[...]


def _load_token_tiles(ref, first_row, tokens):
    chunks = [ref[pl.ds(first_row + j, tokens, stride=TOKEN_CHUNKS), :] for j in range(TOKEN_CHUNKS)]
    return jnp.concatenate(chunks, axis=1)


def _store_token_tiles(ref, val):
    tokens = val.shape[0]
    for j in range(TOKEN_CHUNKS):
        ref[pl.ds(j, tokens, stride=TOKEN_CHUNKS), :] = val[:, j * LANES:(j + 1) * LANES].astype(ref.dtype)


def _tile_copy(src, dst, src_tok, dst_tok, sem):
    src_row = pl.multiple_of(src_tok * TOKEN_CHUNKS, TOKEN_CHUNKS)
    dst_row = pl.multiple_of(dst_tok * TOKEN_CHUNKS, TOKEN_CHUNKS)
    return pltpu.make_async_copy(src.at[pl.ds(src_row, TOKEN_CHUNKS), :], dst.at[pl.ds(dst_row, TOKEN_CHUNKS), :], sem)


def _dispatch_body(pad_end_ref, padded_ref, dest_ref, ht_ref, xb_ref, zero_s, sem):
    block_rows = MOE_TILE * TOKEN_CHUNKS
    n_blocks = xb_ref.shape[0] // block_rows

    def block_copy(b):
        return pltpu.make_async_copy(zero_s, xb_ref.at[pl.ds(pl.multiple_of(b * block_rows, block_rows), block_rows), :], sem)

    @pl.when(pl.program_id(0) == 0)
    def _():
        zero_s[...] = jnp.zeros(zero_s.shape, zero_s.dtype)

        def start(e, carry):
            @pl.when(padded_ref[e] > 0)
            def _():
                block_copy(pad_end_ref[e] // MOE_TILE - 1).start()
            return carry

        def wait(e, carry):
            @pl.when(padded_ref[e] > 0)
            def _():
                block_copy(0).wait()
            return carry

        lax.fori_loop(0, N_EXPERTS, start, 0)
        lax.fori_loop(0, N_EXPERTS, wait, 0)
        first_unused = pad_end_ref[N_EXPERTS - 1] // MOE_TILE
        lax.fori_loop(first_unused, n_blocks, lambda b, c: (block_copy(b).start(), c)[1], 0)
        lax.fori_loop(first_unused, n_blocks, lambda b, c: (block_copy(0).wait(), c)[1], 0)

    def start_token(t, carry):
        _tile_copy(ht_ref, xb_ref, t, dest_ref[0, 0, t], sem).start(priority=0)
        _tile_copy(ht_ref, xb_ref, t, dest_ref[0, 0, GATHER_TILE + t], sem).start(priority=1)
        return carry

    def wait_row(t, carry):
        _tile_copy(ht_ref, xb_ref, 0, 0, sem).wait()
        return carry

    lax.fori_loop(0, GATHER_TILE, start_token, 0, unroll=8)
    lax.fori_loop(0, 2 * GATHER_TILE, wait_row, 0, unroll=8)


def _dispatch(ht, dest_blocks, pad_end, padded, cap):
    n = ht.shape[0] // TOKEN_CHUNKS
    grid_spec = pltpu.PrefetchScalarGridSpec(
        num_scalar_prefetch=2, grid=(n // GATHER_TILE,),
        in_specs=[pl.BlockSpec((1, 1, 2 * GATHER_TILE), lambda i, pe, pd: (i, 0, 0), memory_space=pltpu.SMEM),
                  pl.BlockSpec((GATHER_TILE * TOKEN_CHUNKS, LANES), lambda i, pe, pd: (i, 0))],
        out_specs=pl.BlockSpec(memory_space=pl.ANY),
        scratch_shapes=[pltpu.VMEM((MOE_TILE * TOKEN_CHUNKS, LANES), F32), pltpu.SemaphoreType.DMA(())])
    return pl.pallas_call(
        _dispatch_body, grid_spec=grid_spec, out_shape=jax.ShapeDtypeStruct((cap * TOKEN_CHUNKS, LANES), F32),
        compiler_params=_params("arbitrary"), name="moe_dispatch",
    )(pad_end, padded, dest_blocks, ht)


def _expert_body(blk_ref, exp_ref, used_ref, x_ref, wg_ref, wu_ref, wd_ref, y_ref, wg_s, wu_s, wd_s):
    i = pl.program_id(0)

    @pl.when((i == 0) | (exp_ref[i] != exp_ref[jnp.maximum(i - 1, 0)]))
    def _():
        wg_s[...] = wg_ref[0].astype(BF16)
        wu_s[...] = wu_ref[0].astype(BF16)
        wd_s[...] = wd_ref[0].astype(BF16)

    @pl.when(i < used_ref[0])
    def _():
        half = MOE_TILE // 2
        hidden = []
        for part in range(2):
            x = _load_token_tiles(x_ref, part * half * TOKEN_CHUNKS, half).astype(BF16)
            gate = jnp.dot(x, wg_s[...], preferred_element_type=F32)
            up = jnp.dot(x, wu_s[...], preferred_element_type=F32)
            hidden.append((gate, up))
        for part, (gate, up) in enumerate(hidden):
            u = gate * _sigmoid(gate) * up
            y = jnp.dot(u.astype(BF16), wd_s[...], preferred_element_type=F32)
            rows = y_ref.at[pl.ds(part * half * TOKEN_CHUNKS, half * TOKEN_CHUNKS), :]
            _store_token_tiles(rows, y)

    @pl.when(i >= used_ref[0])
    def _():
        y_ref[...] = jnp.zeros(y_ref.shape, y_ref.dtype)


def _experts(xb, blk, blk_expert, n_used, layer, w_gate, w_up, w_down):
    d = D_MODEL
    rows = pl.BlockSpec((MOE_TILE * TOKEN_CHUNKS, LANES), lambda i, blk, ex, used: (blk[i], 0))
    out_rows = pl.BlockSpec((MOE_TILE * TOKEN_CHUNKS, LANES), lambda i, blk, ex, used: (i, 0))
    grid_spec = pltpu.PrefetchScalarGridSpec(
        num_scalar_prefetch=3, grid=(xb.shape[0] // (MOE_TILE * TOKEN_CHUNKS),),
        in_specs=[rows,
                  pl.BlockSpec((None, 1, d, D_EXPERT), lambda i, blk, ex, used: (layer, ex[i], 0, 0)),
                  pl.BlockSpec((None, 1, d, D_EXPERT), lambda i, blk, ex, used: (layer, ex[i], 0, 0)),
                  pl.BlockSpec((None, 1, D_EXPERT, d), lambda i, blk, ex, used: (layer, ex[i], 0, 0))],
        out_specs=out_rows,
        scratch_shapes=[pltpu.VMEM((d, D_EXPERT), BF16), pltpu.VMEM((d, D_EXPERT), BF16),
                        pltpu.VMEM((D_EXPERT, d), BF16)])
    return pl.pallas_call(
        _expert_body, grid_spec=grid_spec, out_shape=jax.ShapeDtypeStruct(xb.shape, F32),
        compiler_params=_params("arbitrary"), name="moe_experts",
    )(blk, blk_expert, n_used, xb, w_gate, w_up, w_down)


def _combine_body(dest_ref, next_ref, h_ref, gate_ref, lg_ref, lb_ref, yb_ref, o_ref, rows_a, rows_b, sem_a, sem_b):
    step, last = pl.program_id(0), pl.num_programs(0) - 1

    def gather(idx_ref, rows, sem):
        def start_token(t, carry):
            _tile_copy(yb_ref, rows, idx_ref[0, 0, t], t, sem).start(priority=0)
            _tile_copy(yb_ref, rows, idx_ref[0, 0, GATHER_TILE + t], GATHER_TILE + t, sem).start(priority=1)
            return carry
        lax.fori_loop(0, GATHER_TILE, start_token, 0, unroll=8)

    def finish(rows, sem):
        def wait_row(t, carry):
            _tile_copy(yb_ref, rows, 0, 0, sem).wait()
            return carry
        lax.fori_loop(0, 2 * GATHER_TILE, wait_row, 0, unroll=8)
        gate = gate_ref[...]
        ffn = (gate[:, 0:1] * _load_token_tiles(rows, 0, GATHER_TILE)
               + gate[:, 1:2] * _load_token_tiles(rows, GATHER_TILE * TOKEN_CHUNKS, GATHER_TILE))
        o_ref[...] = _layer_norm(ALPHA * h_ref[...] + ffn, lg_ref[...], lb_ref[...])

    @pl.when(step == 0)
    def _():
        gather(dest_ref, rows_a, sem_a)

    for parity, (rows, sem, rows_next, sem_next) in enumerate(((rows_a, sem_a, rows_b, sem_b),
                                                               (rows_b, sem_b, rows_a, sem_a))):
        @pl.when(step % 2 == parity)
        def _(rows=rows, sem=sem, rows_next=rows_next, sem_next=sem_next):
            @pl.when(step < last)
            def _():
                gather(next_ref, rows_next, sem_next)
            finish(rows, sem)


def _combine(h, gate_rows, dest_blocks, yb, ln_g, ln_b):
    n, d = h.shape
    tm = GATHER_TILE
    steps = n // tm
    rows = pltpu.VMEM((2 * tm * TOKEN_CHUNKS, LANES), F32)
    return pl.pallas_call(
        _combine_body, grid=(steps,),
        in_specs=[pl.BlockSpec((1, 1, 2 * tm), lambda i: (i, 0, 0), memory_space=pltpu.SMEM),
                  pl.BlockSpec((1, 1, 2 * tm), lambda i: (jnp.minimum(i + 1, steps - 1), 0, 0), memory_space=pltpu.SMEM),
                  pl.BlockSpec((tm, d), lambda i: (i, 0)), pl.BlockSpec((tm, 2), lambda i: (i, 0)),
                  pl.BlockSpec((1, d), lambda i: (0, 0)), pl.BlockSpec((1, d), lambda i: (0, 0)),
                  pl.BlockSpec(memory_space=pl.ANY)],
        out_specs=pl.BlockSpec((tm, d), lambda i: (i, 0)), out_shape=jax.ShapeDtypeStruct((n, d), F32),
        scratch_shapes=[rows, rows, pltpu.SemaphoreType.DMA(()), pltpu.SemaphoreType.DMA(())],
        compiler_params=_params("arbitrary"), name="moe_combine",
    )(dest_blocks, dest_blocks, h, gate_rows, ln_g.reshape(1, d), ln_b.reshape(1, d), yb)


def _moe_layer(h, ht, routing, layer, w_gate, w_up, w_down, ln_g, ln_b):
    n, d = h.shape
    idx, gate, rank, cnt = routing
    counts = cnt[:, 0].astype(I32)
    padded = (counts + MOE_TILE - 1) // MOE_TILE * MOE_TILE
    pad_end = jnp.cumsum(padded).astype(I32)
    pad_start = pad_end - padded
    experts = jnp.arange(N_EXPERTS, dtype=I32)
    dest = jnp.sum(jnp.where(idx[..., None] == experts, pad_start, 0), axis=-1) + rank
    cap = 2 * n + N_EXPERTS * MOE_TILE
    n_blocks = cap // MOE_TILE
    n_used = pad_end[-1] // MOE_TILE
    blk = jnp.minimum(jnp.arange(n_blocks, dtype=I32), n_used - 1)
    blk_expert = jnp.minimum(jnp.sum(pad_end[None, :] <= (blk * MOE_TILE)[:, None], axis=1), N_EXPERTS - 1).astype(I32)
    tm = GATHER_TILE
    dest_blocks = dest.reshape(2, n // tm, tm).transpose(1, 0, 2).reshape(n // tm, 1, 2 * tm)
    xb = _dispatch(ht, dest_blocks, pad_end, padded, cap)
    yb = _experts(xb, blk, blk_expert, n_used.reshape(1).astype(I32), layer, w_gate, w_up, w_down)
    return _combine(h, gate.T, dest_blocks, yb, ln_g, ln_b)


def kernel(x, ln_g, ln_b, even_w_in, even_w_out, even_lambda, even_subln_g, even_rel_bias, odd_mix, odd_w_rkv, odd_w0, odd_w1, odd_w2, odd_a0, odd_a1, odd_a2, odd_g1, odd_g2, odd_k_k, odd_k_a, odd_r_k, odd_lnx_g, odd_lnx_b, odd_w_out, vres_v0, vres_v1, vres_v2, router_w, router_b, moe_w_gate, moe_w_up, moe_w_down):
    bn, seq, d = x.shape
    x2 = x.reshape(bn * seq, d)
    v_first = None
    rotary_tables = _rotary_tables(seq)
    for layer in range(DEPTH):
        if layer % 2 == 0:
            e = layer // 2
            lam_init = 0.8 - 0.6 * math.exp(-0.3 * layer)
            h, ht, routing = _even_mixer(x2, bn, seq, rotary_tables, even_w_in[e], even_w_out[e], even_lambda[e],
                                         even_subln_g[e], even_rel_bias[e], lam_init, ln_g[layer, 0], ln_b[layer, 0],
                                         router_w, router_b)
        else:
            o = layer // 2
            vres = None if o == 0 else (vres_v0[o - 1], vres_v1[o - 1], vres_v2[o - 1])
            h, ht, routing, v_first = _rwkv_mixer(
                x2, bn, seq, v_first, odd_mix[o], odd_w_rkv[o], odd_w0[o], odd_w1[o], odd_w2[o], odd_a0[o], odd_a1[o],
                odd_a2[o], odd_g1[o], odd_g2[o], odd_k_k[o], odd_k_a[o], odd_r_k[o], odd_lnx_g[o], odd_lnx_b[o],
                odd_w_out[o], vres, ln_g[layer, 0], ln_b[layer, 0], router_w, router_b)
        x2 = _moe_layer(h, ht, routing, layer, moe_w_gate, moe_w_up, moe_w_down, ln_g[layer, 1], ln_b[layer, 1])
    return x2.reshape(bn, seq, d)
```
